```python
import math
import jax
import jax.numpy as jnp
from jax import lax
import numpy as np

D_MODEL = 2048
BATCH = 1
SEQ = 16384
DEPTH = 4

HEAD_DIM = 128
GROUP_WIDTH = D_MODEL // 4
N_GROUP_HEADS = GROUP_WIDTH // HEAD_DIM
MIX_WIDTH = 4 * GROUP_WIDTH

SSM_CH = GROUP_WIDTH
SSM_GROUP = 16
SSM_NG = SSM_CH // SSM_GROUP
SSM_STATE = 64

MOBA_BLOCK = 256
MOBA_TOPK = 3

NSA_CMP_LEN = 32
NSA_CMP_STRIDE = 16
NSA_CMP_HIDDEN = 256
NSA_SEL_BLOCK = 64
NSA_TOPK = 16
NSA_WINDOW = 512

N_MEM = 256

ROPE_THETA = 500000.0
ROPE_DIM = HEAD_DIM // 4

D_FF = 5632
N_EXPERTS = 8
TOP_K = 2
MOE_BLOCK = 256

Q_BLOCK = 128
LN_EPS = 1e-5
NEG_INF = -1e30
FORCE_SCORE = 1e9
ALPHA = (2.0 * DEPTH) ** 0.25
BETA = (8.0 * DEPTH) ** -0.25

OFF_SSM = 0
OFF_MOBA = OFF_SSM + SSM_CH
OFF_NSA_Q = OFF_MOBA + 3 * GROUP_WIDTH
OFF_NSA_KV = OFF_NSA_Q + GROUP_WIDTH
OFF_NSA_G = OFF_NSA_KV + 6 * HEAD_DIM
OFF_X_Q = OFF_NSA_G + 3 * N_GROUP_HEADS
IN_WIDTH = OFF_X_Q + GROUP_WIDTH

kernel_name = 'hybrid_s5_moba_nsa_xattn_deepnorm_moe'


def layer_norm(x, g, b):
    xf = x.astype(jnp.float32)
    mu = jnp.mean(xf, axis=-1, keepdims=True)
    var = jnp.mean(jnp.square(xf - mu), axis=-1, keepdims=True)
    y = (xf - mu) * lax.rsqrt(var + LN_EPS)
    return (y * g.astype(jnp.float32) + b.astype(jnp.float32)).astype(x.dtype)


def masked_softmax(s, mask):
    s = jnp.where(mask, s.astype(jnp.float32), NEG_INF)
    m = jnp.max(s, axis=-1, keepdims=True)
    e = jnp.where(mask, jnp.exp(s - m), 0.0)
    return e / jnp.maximum(jnp.sum(e, axis=-1, keepdims=True), 1e-30)


def rope_tables(seq_len):
    inv_freq = 1.0 / (ROPE_THETA ** (jnp.arange(0, ROPE_DIM, 2, dtype=jnp.float32) / ROPE_DIM))
    ang = jnp.arange(seq_len, dtype=jnp.float32)[:, None] * inv_freq[None, :]
    return jnp.cos(ang), jnp.sin(ang)


def partial_rope(t, cos, sin):
    half = ROPE_DIM // 2
    bshape = (t.shape[0],) + (1,) * (t.ndim - 2) + (half,)
    c = cos.reshape(bshape).astype(t.dtype)
    s = sin.reshape(bshape).astype(t.dtype)
    t1, t2, rest = t[..., :half], t[..., half:ROPE_DIM], t[..., ROPE_DIM:]
    return jnp.concatenate([t1 * c - t2 * s, t1 * s + t2 * c, rest], axis=-1)


def s5_mixer(u, a_re, a_im, log_dt, b_re, b_im, c_re, c_im, d_skip, w_glu):
    L = u.shape[0]
    f32 = jnp.float32
    uf = u.astype(f32).reshape(L, SSM_NG, SSM_GROUP)
    dt = jnp.exp(log_dt.astype(f32))[:, None]
    ar, ai = a_re.astype(f32), a_im.astype(f32)
    mag = jnp.exp(ar * dt)
    lam_re, lam_im = mag * jnp.cos(ai * dt), mag * jnp.sin(ai * dt)
    den = ar * ar + ai * ai
    nr, ni = lam_re - 1.0, lam_im
    coef_re = (nr * ar + ni * ai) / den
    coef_im = (ni * ar - nr * ai) / den
    br, bi = b_re.astype(f32), b_im.astype(f32)
    bbar_re = coef_re[..., None] * br - coef_im[..., None] * bi
    bbar_im = coef_re[..., None] * bi + coef_im[..., None] * br
    bu_re = jnp.einsum('lgc,gnc->lgn', uf, bbar_re)
    bu_im = jnp.einsum('lgc,gnc->lgn', uf, bbar_im)
    a_seq_re = jnp.broadcast_to(lam_re, bu_re.shape)
    a_seq_im = jnp.broadcast_to(lam_im, bu_im.shape)

    def combine(e1, e2):
        a1r, a1i, s1r, s1i = e1
        a2r, a2i, s2r, s2i = e2
        return (a2r * a1r - a2i * a1i, a2r * a1i + a2i * a1r,
                a2r * s1r - a2i * s1i + s2r, a2r * s1i + a2i * s1r + s2i)

    _, _, h_re, h_im = lax.associative_scan(combine, (a_seq_re, a_seq_im, bu_re, bu_im), axis=0)
    y = (jnp.einsum('lgn,gcn->lgc', h_re, c_re.astype(f32))
         - jnp.einsum('lgn,gcn->lgc', h_im, c_im.astype(f32))
         + d_skip.astype(f32).reshape(SSM_NG, SSM_GROUP) * uf).reshape(L, SSM_CH)
    z = jax.nn.gelu(y)
    out = z * jax.nn.sigmoid(z @ w_glu.astype(f32))
    return out.astype(u.dtype)


def moba_mixer(q, k, v, cos, sin):
    L, H, Dh = q.shape
    q = partial_rope(q, cos, sin)
    k = partial_rope(k, cos, sin)
    Lp = -(-L // MOBA_BLOCK) * MOBA_BLOCK
    k = jnp.pad(k, ((0, Lp - L), (0, 0), (0, 0)))
    v = jnp.pad(v, ((0, Lp - L), (0, 0), (0, 0)))
    nb = Lp // MOBA_BLOCK
    kb = k.reshape(nb, MOBA_BLOCK, H, Dh).transpose(2, 0, 1, 3)
    vb = v.reshape(nb, MOBA_BLOCK, H, Dh).transpose(2, 0, 1, 3)
    kmean = jnp.mean(kb.astype(jnp.float32), axis=2).astype(k.dtype)
    topk = min(MOBA_TOPK, nb)
    scale = HEAD_DIM ** -0.5
    h_idx = jnp.arange(H)[None, :, None]
    blk_ids = jnp.arange(nb)
    own_off = jnp.arange(MOBA_BLOCK)

    def one_block(c):
        start = c * Q_BLOCK
        qc = lax.dynamic_slice_in_dim(q, start, Q_BLOCK, 0)
        pos = start + jnp.arange(Q_BLOCK)
        own = start // MOBA_BLOCK
        gs = jnp.einsum('qhd,hbd->qhb', qc, kmean)
        gs = jnp.where(blk_ids < own, gs.astype(jnp.float32), NEG_INF)
        gval, idx = lax.top_k(gs, topk)
        sel_ok = gval > 0.5 * NEG_INF
        kg = kb[h_idx, idx]
        vg = vb[h_idx, idx]
        s_sel = jnp.einsum('qhd,qhkjd->qhkj', qc, kg).reshape(Q_BLOCK, H, topk * MOBA_BLOCK)
        m_sel = jnp.broadcast_to(sel_ok[..., None], (Q_BLOCK, H, topk, MOBA_BLOCK)).reshape(Q_BLOCK, H, topk * MOBA_BLOCK)
        k_own = lax.dynamic_slice_in_dim(k, own * MOBA_BLOCK, MOBA_BLOCK, 0)
        v_own = lax.dynamic_slice_in_dim(v, own * MOBA_BLOCK, MOBA_BLOCK, 0)
        s_own = jnp.einsum('qhd,jhd->qhj', qc, k_own)
        m_own = jnp.broadcast_to(((own * MOBA_BLOCK + own_off)[None, :] <= pos[:, None])[:, None, :], s_own.shape)
        p = masked_softmax(jnp.concatenate([s_sel, s_own], axis=-1) * scale,
                           jnp.concatenate([m_sel, m_own], axis=-1)).astype(v.dtype)
        p_sel = p[..., :topk * MOBA_BLOCK].reshape(Q_BLOCK, H, topk, MOBA_BLOCK)
        p_own = p[..., topk * MOBA_BLOCK:]
        return jnp.einsum('qhkj,qhkjd->qhd', p_sel, vg) + jnp.einsum('qhj,jhd->qhd', p_own, v_own)

    out = lax.map(one_block, jnp.arange(L // Q_BLOCK))
    return out.reshape(L, H, Dh)


def nsa_mixer(q, kv, gates, pos_k, pos_v, ck1, ck2, cv1, cv2, cos, sin):
    L, H, Dh = q.shape
    k_cmp, v_cmp, k_sel, v_sel, k_win, v_win = [kv[:, j] for j in range(6)]
    q_rope = partial_rope(q, cos, sin)
    k_sel = partial_rope(k_sel, cos, sin)
    k_win = partial_rope(k_win, cos, sin)
    scale = HEAD_DIM ** -0.5

    nc = (L - NSA_CMP_LEN) // NSA_CMP_STRIDE + 1
    cidx = jnp.arange(nc)[:, None] * NSA_CMP_STRIDE + jnp.arange(NSA_CMP_LEN)[None, :]

    def compress(t, pos_emb, w1, w2):
        blk = (t[cidx] + pos_emb).reshape(nc, NSA_CMP_LEN * HEAD_DIM)
        return jax.nn.gelu(blk @ w1) @ w2

    kc = compress(k_cmp, pos_k, ck1, ck2)
    vc = compress(v_cmp, pos_v, cv1, cv2)
    cmp_end = jnp.arange(nc) * NSA_CMP_STRIDE + NSA_CMP_LEN - 1

    ns = L // NSA_SEL_BLOCK
    kS = k_sel.reshape(ns, NSA_SEL_BLOCK, Dh)
    vS = v_sel.reshape(ns, NSA_SEL_BLOCK, Dh)
    topk = min(NSA_TOPK, ns)
    ratio = NSA_SEL_BLOCK // NSA_CMP_STRIDE
    lead = NSA_CMP_LEN // NSA_CMP_STRIDE - 1
    n_terms = ratio + lead
    right = max(ratio * ns - nc, 0)
    span = ratio * (ns - 1) + 1
    blk_start = jnp.arange(ns) * NSA_SEL_BLOCK
    blk_ids = jnp.arange(ns)
    sel_off = jnp.arange(NSA_SEL_BLOCK)

    k_wp = jnp.pad(k_win, ((NSA_WINDOW, 0), (0, 0)))
    v_wp = jnp.pad(v_win, ((NSA_WINDOW, 0), (0, 0)))
    win_off = jnp.arange(Q_BLOCK + NSA_WINDOW) - NSA_WINDOW

    def one_block(c):
        start = c * Q_BLOCK
        pos = start + jnp.arange(Q_BLOCK)
        qc = lax.dynamic_slice_in_dim(q, start, Q_BLOCK, 0)
        qr = lax.dynamic_slice_in_dim(q_rope, start, Q_BLOCK, 0)
        g = lax.dynamic_slice_in_dim(gates, start, Q_BLOCK, 0)
        m_c = (cmp_end[None, :] <= pos[:, None])[:, None, :]
        p_c = masked_softmax(jnp.einsum('qhd,cd->qhc', qc, kc) * scale, m_c)
        o_c = jnp.einsum('qhc,cd->qhd', p_c.astype(vc.dtype), vc)
        imp = jnp.pad(jnp.sum(p_c, axis=1), ((0, 0), (lead, right)))
        imp_sel = sum(imp[:, r:r + span:ratio] for r in range(n_terms))
        avail = blk_start[None, :] <= pos[:, None]
        own = (pos // NSA_SEL_BLOCK)[:, None] == blk_ids[None, :]
        imp_sel = jnp.where(own, FORCE_SCORE, jnp.where(avail, imp_sel, NEG_INF))
        sval, sidx = lax.top_k(imp_sel, topk)
        s_ok = sval > 0.5 * NEG_INF
        kg = kS[sidx]
        vg = vS[sidx].reshape(Q_BLOCK, topk * NSA_SEL_BLOCK, Dh)
        kpos = sidx[..., None] * NSA_SEL_BLOCK + sel_off
        m_s = (s_ok[..., None] & (kpos <= pos[:, None, None])).reshape(Q_BLOCK, 1, topk * NSA_SEL_BLOCK)
        s_s = jnp.einsum('qhd,qkjd->qhkj', qr, kg).reshape(Q_BLOCK, H, topk * NSA_SEL_BLOCK) * scale
        p_s = masked_softmax(s_s, m_s).astype(vg.dtype)
        o_s = jnp.einsum('qhn,qnd->qhd', p_s, vg)
        kw = lax.dynamic_slice_in_dim(k_wp, start, Q_BLOCK + NSA_WINDOW, 0)
        vw = lax.dynamic_slice_in_dim(v_wp, start, Q_BLOCK + NSA_WINDOW, 0)
        wpos = start + win_off
        m_w = ((wpos[None, :] <= pos[:, None]) & (wpos[None, :] > pos[:, None] - NSA_WINDOW)
               & (wpos[None, :] >= 0))[:, None, :]
        p_w = masked_softmax(jnp.einsum('qhd,jd->qhj', qr, kw) * scale, m_w).astype(vw.dtype)
        o_w = jnp.einsum('qhj,jd->qhd', p_w, vw)
        return g[..., 0:1] * o_c + g[..., 1:2] * o_s + g[..., 2:3] * o_w

    out = lax.map(one_block, jnp.arange(L // Q_BLOCK))
    return out.reshape(L, H, Dh)


def mixing_sublayer(x, mem, cos, sin, w_in, a_re, a_im, log_dt, b_re, b_im, c_re, c_im,
                    d_skip, w_glu, pos_k, pos_v, ck1, ck2, cv1, cv2, mem_wk, mem_wv, w_o):
    B, L, _ = x.shape
    H, Dh, GW = N_GROUP_HEADS, HEAD_DIM, GROUP_WIDTH
    h = x @ w_in
    u = h[..., OFF_SSM:OFF_SSM + SSM_CH]
    mq = h[..., OFF_MOBA:OFF_MOBA + GW].reshape(B, L, H, Dh)
    mk = h[..., OFF_MOBA + GW:OFF_MOBA + 2 * GW].reshape(B, L, H, Dh)
    mv = h[..., OFF_MOBA + 2 * GW:OFF_MOBA + 3 * GW].reshape(B, L, H, Dh)
    nq = h[..., OFF_NSA_Q:OFF_NSA_Q + GW].reshape(B, L, H, Dh)
    nkv = h[..., OFF_NSA_KV:OFF_NSA_KV + 6 * Dh].reshape(B, L, 6, Dh)
    ng = jax.nn.sigmoid(h[..., OFF_NSA_G:OFF_NSA_G + 3 * H]).reshape(B, L, H, 3)
    xq = h[..., OFF_X_Q:OFF_X_Q + GW].reshape(B, L, H, Dh)

    y_ssm = jax.vmap(lambda ub: s5_mixer(ub, a_re, a_im, log_dt, b_re, b_im, c_re, c_im, d_skip, w_glu))(u)
    y_moba = jax.vmap(lambda qb, kb, vb: moba_mixer(qb, kb, vb, cos, sin))(mq, mk, mv)
    y_nsa = jax.vmap(lambda qb, kvb, gb: nsa_mixer(qb, kvb, gb, pos_k, pos_v, ck1, ck2, cv1, cv2, cos, sin))(nq, nkv, ng)

    mem_k = (mem @ mem_wk).reshape(B, N_MEM, H, Dh)
    mem_v = (mem @ mem_wv).reshape(B, N_MEM, H, Dh)
    s_x = jnp.einsum('blhd,bmhd->blhm', xq, mem_k) * (HEAD_DIM ** -0.5)
    p_x = jax.nn.softmax(s_x.astype(jnp.float32), axis=-1).astype(mem_v.dtype)
    y_x = jnp.einsum('blhm,bmhd->blhd', p_x, mem_v)

    y = jnp.concatenate([y_ssm, y_moba.reshape(B, L, GW), y_nsa.reshape(B, L, GW),
                         y_x.reshape(B, L, GW)], axis=-1)
    return y @ w_o


def swiglu(x, wg, wu, wd):
    return (jax.nn.silu(x @ wg) * (x @ wu)) @ wd


def moe_swiglu(x, router, wg, wu, wd):
    B, L, D = x.shape
    T = B * L
    xt = x.reshape(T, D)
    logits = (xt @ router).astype(jnp.float32)
    top_val, top_idx = lax.top_k(logits, TOP_K)
    gate = jax.nn.softmax(top_val, axis=-1)
    e_flat = top_idx.reshape(-1)
    tok_flat = jnp.repeat(jnp.arange(T, dtype=jnp.int32), TOP_K)
    w_flat = gate.reshape(-1)
    order = jnp.argsort(e_flat)
    e_s, tok_s, w_s = e_flat[order], tok_flat[order], w_flat[order]
    counts = jnp.zeros((N_EXPERTS,), jnp.int32).at[e_flat].add(1)
    padded = (counts + MOE_BLOCK - 1) // MOE_BLOCK * MOE_BLOCK
    start = jnp.cumsum(counts) - counts
    pend = jnp.cumsum(padded)
    pstart = pend - padded
    dest = pstart[e_s] + (jnp.arange(T * TOP_K, dtype=jnp.int32) - start[e_s])
    n_blocks = -(-(T * TOP_K + N_EXPERTS * (MOE_BLOCK - 1)) // MOE_BLOCK)
    P = n_blocks * MOE_BLOCK
    buf_tok = jnp.zeros((P,), jnp.int32).at[dest].set(tok_s)
    buf_w = jnp.zeros((P,), jnp.float32).at[dest].set(w_s)
    blk_exp = jnp.minimum(jnp.searchsorted(pend, jnp.arange(n_blocks, dtype=jnp.int32) * MOE_BLOCK, side='right'),
                          N_EXPERTS - 1)

    def one_block(b):
        toks = lax.dynamic_slice_in_dim(buf_tok, b * MOE_BLOCK, MOE_BLOCK)
        e = blk_exp[b]
        xb = xt[toks]
        return (jax.nn.silu(xb @ wg[e]) * (xb @ wu[e])) @ wd[e]

    yb = lax.map(one_block, jnp.arange(n_blocks)).reshape(P, D)
    y = jnp.zeros_like(xt).at[buf_tok].add((yb * buf_w[:, None]).astype(xt.dtype))
    return y.reshape(B, L, D)


def setup_inputs(seed: int = 0) -> dict:
    key = jax.random.key(seed)
    ks = jax.random.split(key, 40)
    f32 = jnp.float32
    n_dense = (DEPTH + 1) // 2
    n_moe = DEPTH // 2

    def nrm(i, shape, scale):
        return jax.random.normal(ks[i], shape, f32) * scale

    a_im = math.pi * jnp.broadcast_to(jnp.arange(SSM_STATE, dtype=f32), (DEPTH, SSM_NG, SSM_STATE))
    return {
        'x': nrm(0, (BATCH, SEQ, D_MODEL), 1.0),
        'mem': nrm(1, (BATCH, N_MEM, D_MODEL), 1.0),
        'ln_in_g': 1.0 + nrm(2, (D_MODEL,), 0.02),
        'ln_in_b': nrm(3, (D_MODEL,), 0.02),
        'w_in': nrm(4, (DEPTH, D_MODEL, IN_WIDTH), D_MODEL ** -0.5),
        'ssm_a_re': -0.5 + nrm(5, (DEPTH, SSM_NG, SSM_STATE), 0.01),
        'ssm_a_im': a_im + nrm(6, (DEPTH, SSM_NG, SSM_STATE), 0.01),
        'ssm_log_dt': jax.random.uniform(ks[7], (DEPTH, SSM_NG), f32, math.log(1e-3), math.log(1e-1)),
        'ssm_b_re': nrm(8, (DEPTH, SSM_NG, SSM_STATE, SSM_GROUP), (2.0 * SSM_GROUP) ** -0.5),
        'ssm_b_im': nrm(9, (DEPTH, SSM_NG, SSM_STATE, SSM_GROUP), (2.0 * SSM_GROUP) ** -0.5),
        'ssm_c_re': nrm(10, (DEPTH, SSM_NG, SSM_GROUP, SSM_STATE), (2.0 * SSM_STATE) ** -0.5),
        'ssm_c_im': nrm(11, (DEPTH, SSM_NG, SSM_GROUP, SSM_STATE), (2.0 * SSM_STATE) ** -0.5),
        'ssm_d': nrm(12, (DEPTH, SSM_CH), 1.0),
        'ssm_w_glu': nrm(13, (DEPTH, SSM_CH, SSM_CH), SSM_CH ** -0.5),
        'nsa_pos_k': nrm(14, (DEPTH, NSA_CMP_LEN, HEAD_DIM), 0.02),
        'nsa_pos_v': nrm(15, (DEPTH, NSA_CMP_LEN, HEAD_DIM), 0.02),
        'nsa_ck1': nrm(16, (DEPTH, NSA_CMP_LEN * HEAD_DIM, NSA_CMP_HIDDEN), (NSA_CMP_LEN * HEAD_DIM) ** -0.5),
        'nsa_ck2': nrm(17, (DEPTH, NSA_CMP_HIDDEN, HEAD_DIM), NSA_CMP_HIDDEN ** -0.5),
        'nsa_cv1': nrm(18, (DEPTH, NSA_CMP_LEN * HEAD_DIM, NSA_CMP_HIDDEN), (NSA_CMP_LEN * HEAD_DIM) ** -0.5),
        'nsa_cv2': nrm(19, (DEPTH, NSA_CMP_HIDDEN, HEAD_DIM), NSA_CMP_HIDDEN ** -0.5),
        'mem_wk': nrm(20, (DEPTH, D_MODEL, GROUP_WIDTH), D_MODEL ** -0.5),
        'mem_wv': nrm(21, (DEPTH, D_MODEL, GROUP_WIDTH), D_MODEL ** -0.5),
        'w_o': nrm(22, (DEPTH, MIX_WIDTH, D_MODEL), BETA * MIX_WIDTH ** -0.5),
        'ln1_g': 1.0 + nrm(23, (DEPTH, D_MODEL), 0.02),
        'ln1_b': nrm(24, (DEPTH, D_MODEL), 0.02),
        'ln2_g': 1.0 + nrm(25, (DEPTH, D_MODEL), 0.02),
        'ln2_b': nrm(26, (DEPTH, D_MODEL), 0.02),
        'ffn_w_gate': nrm(27, (n_dense, D_MODEL, D_FF), D_MODEL ** -0.5),
        'ffn_w_up': nrm(28, (n_dense, D_MODEL, D_FF), D_MODEL ** -0.5),
        'ffn_w_down': nrm(29, (n_dense, D_FF, D_MODEL), BETA * D_FF ** -0.5),
        'moe_router': nrm(30, (n_moe, D_MODEL, N_EXPERTS), D_MODEL ** -0.5),
        'moe_w_gate': nrm(31, (n_moe, N_EXPERTS, D_MODEL, D_FF), D_MODEL ** -0.5),
        'moe_w_up': nrm(32, (n_moe, N_EXPERTS, D_MODEL, D_FF), D_MODEL ** -0.5),
        'moe_w_down': nrm(33, (n_moe, N_EXPERTS, D_FF, D_MODEL), BETA * D_FF ** -0.5),
    }


def reference(x, mem, ln_in_g, ln_in_b, w_in, ssm_a_re, ssm_a_im, ssm_log_dt, ssm_b_re, ssm_b_im,
              ssm_c_re, ssm_c_im, ssm_d, ssm_w_glu, nsa_pos_k, nsa_pos_v, nsa_ck1, nsa_ck2, nsa_cv1,
              nsa_cv2, mem_wk, mem_wv, w_o, ln1_g, ln1_b, ln2_g, ln2_b, ffn_w_gate, ffn_w_up,
              ffn_w_down, moe_router, moe_w_gate, moe_w_up, moe_w_down):
    x = layer_norm(x, ln_in_g, ln_in_b)
    cos, sin = rope_tables(x.shape[1])
    for i in range(DEPTH):
        y = mixing_sublayer(x, mem, cos, sin, w_in[i], ssm_a_re[i], ssm_a_im[i], ssm_log_dt[i],
                            ssm_b_re[i], ssm_b_im[i], ssm_c_re[i], ssm_c_im[i], ssm_d[i], ssm_w_glu[i],
                            nsa_pos_k[i], nsa_pos_v[i], nsa_ck1[i], nsa_ck2[i], nsa_cv1[i], nsa_cv2[i],
                            mem_wk[i], mem_wv[i], w_o[i])
        x = layer_norm(ALPHA * x + y, ln1_g[i], ln1_b[i])
        if i % 2 == 0:
            f = swiglu(x, ffn_w_gate[i // 2], ffn_w_up[i // 2], ffn_w_down[i // 2])
        else:
            f = moe_swiglu(x, moe_router[i // 2], moe_w_gate[i // 2], moe_w_up[i // 2], moe_w_down[i // 2])
        x = layer_norm(ALPHA * x + f, ln2_g[i], ln2_b[i])
    return x
```

```python
import functools
import math

import jax
import jax.numpy as jnp
from jax import lax
from jax.experimental import pallas as pl
from jax.experimental.pallas import tpu as pltpu

F32 = jnp.float32
BF16 = jnp.bfloat16

D_MODEL = 2048
DEPTH = 4
HEAD_DIM = 128
GROUP_WIDTH = 512
N_HEADS = 4
SSM_CH = 512
SSM_GROUP = 16
SSM_NG = 32
SSM_STATE = 64
SSM_NSTATE = SSM_NG * SSM_STATE
MOBA_BLOCK = 256
MOBA_TOPK = 3
NSA_CMP_LEN = 32
NSA_CMP_STRIDE = 16
NSA_SEL_BLOCK = 64
NSA_TOPK = 16
NSA_WINDOW = 512
ROPE_THETA = 500000.0
ROPE_DIM = 32
D_FF = 5632
N_EXPERTS = 8
TOP_K = 2
Q_BLOCK = 128
LN_EPS = 1e-5
NEG_INF = -1e30
FORCE_SCORE = 1e9
ALPHA = (2.0 * DEPTH) ** 0.25
SCALE = HEAD_DIM ** -0.5

ROW_TILE = 512
SCAN_ROWS = 512
SCAN_LANES = 256
SEL_CHUNK = 512
FF_TILE = 512
MOE_ROWS = 512
PROJ_WIDTH = 3968

_NT = (((1,), (1,)), ((), ()))


def _cparams(sem, vmem_mb=48):
    return pltpu.CompilerParams(dimension_semantics=sem, vmem_limit_bytes=vmem_mb * 1024 * 1024)


def _resident(block_shape, index_map):
    return pl.BlockSpec(block_shape, index_map, pipeline_mode=pl.Buffered(1))


def _layer_norm(r, g, b):
    mu = jnp.mean(r, axis=-1, keepdims=True)
    d = r - mu
    var = jnp.mean(d * d, axis=-1, keepdims=True)
    return d * lax.rsqrt(var + LN_EPS) * g + b


def _gelu_tanh(x):
    return x * (0.5 * (1.0 + jnp.tanh(math.sqrt(2.0 / math.pi) * (x + 0.044715 * (x * x * x)))))


def _sigmoid(x):
    return 1.0 / (1.0 + jnp.exp(-x))


def _split3(a):
    a1 = a.astype(BF16)
    r1 = a - a1.astype(F32)
    a2 = r1.astype(BF16)
    a3 = (r1 - a2.astype(F32)).astype(BF16)
    return a1, a2, a3


def _topk_mask(score, ids, k, n):
    def body(_, carry):
        sc, sel = carry
        m = jnp.max(sc, axis=1, keepdims=True)
        idx = jnp.min(jnp.where(sc == m, ids, float(n)), axis=1, keepdims=True)
        hit = ids == idx
        sel = jnp.where(jnp.logical_and(hit, m > 0.5 * NEG_INF), 1.0, sel)
        sc = jnp.where(hit, NEG_INF, sc)
        return sc, sel

    _, sel = lax.fori_loop(0, k, body, (score, jnp.zeros(score.shape, F32)))
    return sel


def _ln_in_kernel(x_ref, g_ref, b_ref, o_ref, ob_ref):
    y = _layer_norm(x_ref[...], g_ref[...], b_ref[...])
    o_ref[...] = y
    ob_ref[...] = y.astype(BF16)


def _ln_in(x, g, b):
    L = x.shape[0]
    row = pl.BlockSpec((ROW_TILE, D_MODEL), lambda i: (i, 0))
    vec = pl.BlockSpec((1, D_MODEL), lambda i: (0, 0))
    return pl.pallas_call(
        _ln_in_kernel,
        grid=(L // ROW_TILE,),
        in_specs=[row, vec, vec],
        out_specs=[row, row],
        out_shape=[jax.ShapeDtypeStruct((L, D_MODEL), F32), jax.ShapeDtypeStruct((L, D_MODEL), BF16)],
        compiler_params=_cparams(("parallel",)),
        name="ln_in",
    )(x, g.reshape(1, -1), b.reshape(1, -1))


def _proj_kernel(x_ref, w_ref, cos_ref, sa_ref, sb_ref,
                 u_ref, mq_ref, mk_ref, mv_ref, nqc_ref, nqr_ref, cmp_ref, nkv_ref, xq_ref, g_ref):
    x = x_ref[...]
    cos, sa, sb = cos_ref[...], sa_ref[...], sb_ref[...]

    def dot(c0, n):
        return jnp.dot(x, w_ref[:, c0:c0 + n], preferred_element_type=F32)

    def rope(t):
        return t * cos + pltpu.roll(t, HEAD_DIM - 16, 1) * sa + pltpu.roll(t, 16, 1) * sb

    u_ref[...] = dot(0, 512)
    for h in range(N_HEADS):
        c = h * HEAD_DIM
        mq_ref[:, c:c + HEAD_DIM] = (rope(dot(512 + c, HEAD_DIM)) * SCALE).astype(BF16)
        mk_ref[:, c:c + HEAD_DIM] = rope(dot(1024 + c, HEAD_DIM)).astype(BF16)
        t = dot(2048 + c, HEAD_DIM)
        nqc_ref[:, c:c + HEAD_DIM] = (t * SCALE).astype(BF16)
        nqr_ref[:, c:c + HEAD_DIM] = (rope(t) * SCALE).astype(BF16)
    mv_ref[...] = dot(1536, 512).astype(BF16)
    cmp_ref[0] = dot(2560, HEAD_DIM).astype(BF16)
    cmp_ref[1] = dot(2688, HEAD_DIM).astype(BF16)
    nkv_ref[:, 0:128] = rope(dot(2816, HEAD_DIM)).astype(BF16)
    nkv_ref[:, 128:256] = dot(2944, HEAD_DIM).astype(BF16)
    nkv_ref[:, 256:384] = rope(dot(3072, HEAD_DIM)).astype(BF16)
    nkv_ref[:, 384:512] = dot(3200, HEAD_DIM).astype(BF16)
    xq_ref[...] = (dot(3328, 512) * SCALE).astype(BF16)
    g_ref[...] = _sigmoid(dot(3840, HEAD_DIM))


def _proj(xb, w, cosf, sa, sb):
    L = xb.shape[0]
    tm = ROW_TILE
    row = lambda n: pl.BlockSpec((tm, n), lambda i: (i, 0))
    outs = [
        (jax.ShapeDtypeStruct((L, 512), F32), row(512)),
        (jax.ShapeDtypeStruct((L, 512), BF16), row(512)),
        (jax.ShapeDtypeStruct((L, 512), BF16), row(512)),
        (jax.ShapeDtypeStruct((L, 512), BF16), row(512)),
        (jax.ShapeDtypeStruct((L, 512), BF16), row(512)),
        (jax.ShapeDtypeStruct((L, 512), BF16), row(512)),
        (jax.ShapeDtypeStruct((2, L, 128), BF16), pl.BlockSpec((2, tm, 128), lambda i: (0, i, 0))),
        (jax.ShapeDtypeStruct((L, 512), BF16), row(512)),
        (jax.ShapeDtypeStruct((L, 512), BF16), row(512)),
        (jax.ShapeDtypeStruct((L, 128), F32), row(128)),
    ]
    return pl.pallas_call(
        _proj_kernel,
        grid=(L // tm,),
        in_specs=[row(D_MODEL), _resident((D_MODEL, PROJ_WIDTH), lambda i: (0, 0)),
                  row(128), row(128), row(128)],
        out_specs=[o[1] for o in outs],
        out_shape=[o[0] for o in outs],
        compiler_params=_cparams(("parallel",), 56),
        name="in_proj",
    )(xb, w, cosf, sa, sb)


def _s5_kernel(u_ref, bmat_ref, cmat_ref, pw_ref, d_ref, wglu_ref, o_ref, hr_ref, hi_ref, car_ref, cai_ref):
    T = u_ref.shape[0]
    W = SCAN_LANES

    @pl.when(pl.program_id(0) == 0)
    def _():
        car_ref[...] = jnp.zeros(car_ref.shape, F32)
        cai_ref[...] = jnp.zeros(cai_ref.shape, F32)

    u = u_ref[...]
    ub = u.astype(BF16)
    for b in range(4):
        bu = jnp.dot(ub[:, b * 128:(b + 1) * 128], bmat_ref[b], preferred_element_type=F32)
        hr_ref[:, b * 512:(b + 1) * 512] = bu[:, :512]
        hi_ref[:, b * 512:(b + 1) * 512] = bu[:, 512:]

    for cb in range(SSM_NSTATE // W):
        sl = slice(cb * W, (cb + 1) * W)
        steps = [(pw_ref[2 * j, :, sl], pw_ref[2 * j + 1, :, sl], 1 << j) for j in range(3)]
        p_r, p_i = pw_ref[6, :, sl], pw_ref[7, :, sl]

        def tile(t, carry, sl=sl, steps=steps, p_r=p_r, p_i=p_i):
            c_r, c_i = carry
            rows = pl.ds(pl.multiple_of(t * 8, 8), 8)
            x_r, x_i = hr_ref[rows, sl], hi_ref[rows, sl]
            for l_r, l_i, d in steps:
                s_r, s_i = pltpu.roll(x_r, d, 0), pltpu.roll(x_i, d, 0)
                x_r, x_i = x_r + l_r * s_r - l_i * s_i, x_i + l_r * s_i + l_i * s_r
            x_r, x_i = x_r + p_r * c_r - p_i * c_i, x_i + p_r * c_i + p_i * c_r
            hr_ref[rows, sl] = x_r
            hi_ref[rows, sl] = x_i
            return (jnp.broadcast_to(x_r[7:8, :], (8, W)), jnp.broadcast_to(x_i[7:8, :], (8, W)))

        c_r, c_i = lax.fori_loop(0, T // 8, tile, (car_ref[:, sl], cai_ref[:, sl]))
        car_ref[:, sl] = c_r
        cai_ref[:, sl] = c_i

    ys = []
    for b in range(4):
        h_r = hr_ref[:, b * 512:(b + 1) * 512].astype(BF16)
        h_i = hi_ref[:, b * 512:(b + 1) * 512].astype(BF16)
        ys.append(jnp.dot(h_r, cmat_ref[b, :512, :], preferred_element_type=F32)
                  + jnp.dot(h_i, cmat_ref[b, 512:, :], preferred_element_type=F32))
    y = jnp.concatenate(ys, axis=1) + d_ref[...] * u
    z = _gelu_tanh(y)
    gate = _sigmoid(jnp.dot(z.astype(BF16), wglu_ref[...], preferred_element_type=F32))
    o_ref[...] = (z * gate).astype(BF16)


def _s5_params(a_re, a_im, log_dt, b_re, b_im, c_re, c_im):
    dt = jnp.exp(log_dt)[:, None]
    mag = jnp.exp(a_re * dt)
    lam_re, lam_im = mag * jnp.cos(a_im * dt), mag * jnp.sin(a_im * dt)
    den = a_re * a_re + a_im * a_im
    nr, ni = lam_re - 1.0, lam_im
    coef_re = (nr * a_re + ni * a_im) / den
    coef_im = (ni * a_re - nr * a_im) / den
    bbar_re = coef_re[..., None] * b_re - coef_im[..., None] * b_im
    bbar_im = coef_re[..., None] * b_im + coef_im[..., None] * b_re
    eye = jnp.eye(8, dtype=F32)
    to_b = lambda m: jnp.einsum('bgnc,gh->bgchn', m.reshape(4, 8, SSM_STATE, SSM_GROUP), eye).reshape(4, 128, 512)
    bmat = jnp.concatenate([to_b(bbar_re), to_b(bbar_im)], axis=-1).astype(BF16)
    to_c = lambda m: jnp.einsum('bgcn,gh->bgnhc', m.reshape(4, 8, SSM_GROUP, SSM_STATE), eye).reshape(4, 512, 128)
    cmat = jnp.concatenate([to_c(c_re), -to_c(c_im)], axis=1).astype(BF16)
    lr, li = lam_re.reshape(-1), lam_im.reshape(-1)
    pr, pi = [lr], [li]
    for _ in range(7):
        pr, pi = pr + [pr[-1] * lr - pi[-1] * li], pi + [pr[-1] * li + pi[-1] * lr]
    row = jnp.arange(8)[:, None]
    tabs = []
    for d in (1, 2, 4):
        tabs += [jnp.where(row >= d, pr[d - 1][None, :], 0.0), jnp.where(row >= d, pi[d - 1][None, :], 0.0)]
    tabs += [jnp.stack(pr, axis=0), jnp.stack(pi, axis=0)]
    return bmat, cmat, jnp.stack(tabs, axis=0)


def _s5(u, bmat, cmat, pw, d_skip, w_glu):
    L = u.shape[0]
    T = SCAN_ROWS
    row = pl.BlockSpec((T, 512), lambda i: (i, 0))
    full = lambda shape: pl.BlockSpec(shape, lambda i: (0,) * len(shape))
    return pl.pallas_call(
        _s5_kernel,
        grid=(L // T,),
        in_specs=[row, full((4, 128, 1024)), full((4, 1024, 128)), full((8, 8, SSM_NSTATE)),
                  full((1, 512)), full((512, 512))],
        out_specs=row,
        out_shape=jax.ShapeDtypeStruct((L, 512), BF16),
        scratch_shapes=[pltpu.VMEM((T, SSM_NSTATE), F32), pltpu.VMEM((T, SSM_NSTATE), F32),
                        pltpu.VMEM((8, SSM_NSTATE), F32), pltpu.VMEM((8, SSM_NSTATE), F32)],
        compiler_params=_cparams(("arbitrary",)),
        name="s5",
    )(u, bmat, cmat, pw, d_skip.reshape(1, -1), w_glu.astype(BF16))


def _kmean_kernel(k_ref, o_ref):
    rows = k_ref.shape[0]
    k = k_ref[...].astype(F32).reshape(rows // MOBA_BLOCK, MOBA_BLOCK, GROUP_WIDTH)
    o_ref[...] = jnp.mean(k, axis=1).astype(BF16)


def _kmean(k):
    L = k.shape[0]
    rows = 8 * MOBA_BLOCK if L % (8 * MOBA_BLOCK) == 0 else L
    return pl.pallas_call(
        _kmean_kernel,
        grid=(L // rows,),
        in_specs=[pl.BlockSpec((rows, GROUP_WIDTH), lambda i: (i, 0))],
        out_specs=pl.BlockSpec((rows // MOBA_BLOCK, GROUP_WIDTH), lambda i: (i, 0)),
        out_shape=jax.ShapeDtypeStruct((L // MOBA_BLOCK, GROUP_WIDTH), BF16),
        compiler_params=_cparams(("parallel",)),
        name="moba_kmean",
    )(k)


def _moba_kernel(q_ref, k_ref, v_ref, km_ref, o_ref):
    TQ = q_ref.shape[0]
    nb = km_ref.shape[0]
    i = pl.program_id(1)
    start = i * TQ
    own = start // MOBA_BLOCK
    q = q_ref[...]

    blk = lax.broadcasted_iota(jnp.int32, (TQ, nb), 1)
    gs = lax.dot_general(q, km_ref[...], _NT, preferred_element_type=F32)
    gs = jnp.where(blk < own, gs, NEG_INF)
    blk_f = blk.astype(F32)
    sel = _topk_mask(gs, blk_f, MOBA_TOPK, nb)

    own_rows = pl.ds(pl.multiple_of(own * MOBA_BLOCK, MOBA_BLOCK), MOBA_BLOCK)
    s = lax.dot_general(q, k_ref[own_rows, :], _NT, preferred_element_type=F32)
    kpos = own * MOBA_BLOCK + lax.broadcasted_iota(jnp.int32, (TQ, MOBA_BLOCK), 1)
    qpos = start + lax.broadcasted_iota(jnp.int32, (TQ, MOBA_BLOCK), 0)
    s = jnp.where(kpos <= qpos, s, NEG_INF)
    m = jnp.max(s, axis=1, keepdims=True)
    p = jnp.exp(s - m)
    l = jnp.sum(p, axis=1, keepdims=True)
    acc = jnp.dot(p.astype(BF16), v_ref[own_rows, :], preferred_element_type=F32)

    def body(j, carry):
        m, l, acc = carry
        rows = pl.ds(pl.multiple_of(j * MOBA_BLOCK, MOBA_BLOCK), MOBA_BLOCK)
        s = lax.dot_general(q, k_ref[rows, :], _NT, preferred_element_type=F32)
        picked = jnp.sum(jnp.where(blk == j, sel, 0.0), axis=1, keepdims=True)
        s = jnp.where(picked > 0.5, s, NEG_INF)
        m_new = jnp.maximum(m, jnp.max(s, axis=1, keepdims=True))
        a = jnp.exp(m - m_new)
        p = jnp.exp(s - m_new)
        l = a * l + jnp.sum(p, axis=1, keepdims=True)
        acc = a * acc + jnp.dot(p.astype(BF16), v_ref[rows, :], preferred_element_type=F32)
        return m_new, l, acc

    m, l, acc = lax.fori_loop(0, own, body, (m, l, acc))
    o_ref[...] = (acc / l).astype(BF16)


def _moba(q, k, v, kmean):
    L = q.shape[0]
    TQ = Q_BLOCK
    nb = L // MOBA_BLOCK
    return pl.pallas_call(
        _moba_kernel,
        grid=(N_HEADS, L // TQ),
        in_specs=[pl.BlockSpec((TQ, HEAD_DIM), lambda h, i: (i, h)),
                  pl.BlockSpec((L, HEAD_DIM), lambda h, i: (0, h)),
                  pl.BlockSpec((L, HEAD_DIM), lambda h, i: (0, h)),
                  pl.BlockSpec((nb, HEAD_DIM), lambda h, i: (0, h))],
        out_specs=pl.BlockSpec((TQ, HEAD_DIM), lambda h, i: (i, h)),
        out_shape=jax.ShapeDtypeStruct((L, GROUP_WIDTH), BF16),
        compiler_params=_cparams(("parallel", "parallel")),
        name="moba",
    )(q, k, v, kmean)


def _compress_kernel(c_ref, w1_ref, w2_ref, pos_ref, o_ref):
    n = c_ref.shape[1]
    half = w1_ref.shape[1] // 2
    c = c_ref[0]
    a = jnp.dot(c, w1_ref[0, :half, :], preferred_element_type=F32)
    b = jnp.dot(c, w1_ref[0, half:, :], preferred_element_type=F32)
    bias = jnp.dot(pos_ref[0], w1_ref[0], preferred_element_type=F32)[0:1, :]
    hid = _gelu_tanh(a + pltpu.roll(b, n - 1, 0) + bias)
    out = jnp.dot(hid.astype(BF16), w2_ref[0], preferred_element_type=F32)
    last = lax.broadcasted_iota(jnp.int32, out.shape, 0) == n - 1
    o_ref[0] = jnp.where(last, 0.0, out).astype(BF16)


def _compress(cmp_kv, w1, w2, pos):
    L = cmp_kv.shape[1]
    n = L // NSA_CMP_STRIDE
    wide = NSA_CMP_STRIDE * HEAD_DIM
    chunks = cmp_kv.reshape(2, n, wide)
    hidden = w1.shape[-1]
    return pl.pallas_call(
        _compress_kernel,
        grid=(2,),
        in_specs=[pl.BlockSpec((1, n, wide), lambda j: (j, 0, 0)),
                  pl.BlockSpec((1, 2 * wide, hidden), lambda j: (j, 0, 0)),
                  pl.BlockSpec((1, hidden, HEAD_DIM), lambda j: (j, 0, 0)),
                  pl.BlockSpec((1, 8, 2 * wide), lambda j: (j, 0, 0))],
        out_specs=pl.BlockSpec((1, n, HEAD_DIM), lambda j: (j, 0, 0)),
        out_shape=jax.ShapeDtypeStruct((2, n, HEAD_DIM), BF16),
        compiler_params=_cparams(("parallel",)),
        name="nsa_compress",
    )(chunks, w1, w2, pos)


def _stack_heads(ref):
    return jnp.concatenate([ref[:, h * HEAD_DIM:(h + 1) * HEAD_DIM] for h in range(N_HEADS)], axis=0)


def _nsa_cmp_kernel(q_ref, kc_ref, vc_ref, agg_ref, o_ref, sel_ref):
    TQ = q_ref.shape[0]
    nc = kc_ref.shape[1]
    ns = agg_ref.shape[1]
    start = pl.program_id(0) * TQ
    qs = _stack_heads(q_ref)
    s = lax.dot_general(qs, kc_ref[0], _NT, preferred_element_type=F32)
    cend = lax.broadcasted_iota(jnp.int32, s.shape, 1) * NSA_CMP_STRIDE + (NSA_CMP_LEN - 1)
    ok = cend <= start + (lax.broadcasted_iota(jnp.int32, s.shape, 0) & (TQ - 1))
    s = jnp.where(ok, s, NEG_INF)
    m = jnp.max(s, axis=1, keepdims=True)
    e = jnp.where(ok, jnp.exp(s - m), 0.0)
    p = e / jnp.maximum(jnp.sum(e, axis=1, keepdims=True), 1e-30)
    oc = jnp.dot(p.astype(BF16), vc_ref[0], preferred_element_type=F32)
    for h in range(N_HEADS):
        o_ref[:, h * HEAD_DIM:(h + 1) * HEAD_DIM] = oc[h * TQ:(h + 1) * TQ, :]

    imp = p[0:TQ] + p[TQ:2 * TQ] + p[2 * TQ:3 * TQ] + p[3 * TQ:4 * TQ]
    agg = agg_ref[...]
    imp_sel = sum(jnp.dot(t, agg, preferred_element_type=F32) for t in _split3(imp))
    blk = lax.broadcasted_iota(jnp.int32, (TQ, ns), 1)
    own = (start + lax.broadcasted_iota(jnp.int32, (TQ, ns), 0)) // NSA_SEL_BLOCK
    score = jnp.where(blk == own, FORCE_SCORE, jnp.where(blk < own, imp_sel, NEG_INF))
    sel_ref[...] = _topk_mask(score, blk.astype(F32), min(NSA_TOPK, ns), ns).astype(BF16)


def _nsa_cmp(nq, kvc, agg):
    L = nq.shape[0]
    TQ = Q_BLOCK
    nc = kvc.shape[1]
    ns = L // NSA_SEL_BLOCK
    return pl.pallas_call(
        _nsa_cmp_kernel,
        grid=(L // TQ,),
        in_specs=[pl.BlockSpec((TQ, GROUP_WIDTH), lambda i: (i, 0)),
                  pl.BlockSpec((1, nc, HEAD_DIM), lambda i: (0, 0, 0)),
                  pl.BlockSpec((1, nc, HEAD_DIM), lambda i: (1, 0, 0)),
                  pl.BlockSpec((nc, ns), lambda i: (0, 0))],
        out_specs=[pl.BlockSpec((TQ, GROUP_WIDTH), lambda i: (i, 0)),
                   pl.BlockSpec((TQ, ns), lambda i: (i, 0))],
        out_shape=[jax.ShapeDtypeStruct((L, GROUP_WIDTH), F32), jax.ShapeDtypeStruct((L, ns), BF16)],
        compiler_params=_cparams(("parallel",)),
        name="nsa_cmp",
    )(nq, kvc, kvc, agg)


def _nsa_sel_kernel(q_ref, ks_ref, vs_ref, kw_ref, vw_ref, sel_ref, exp_ref, oc_ref, g_ref, o_ref):
    TQ = q_ref.shape[0]
    L = ks_ref.shape[0]
    CK = min(SEL_CHUNK, L)
    start = pl.program_id(0) * TQ
    qs = _stack_heads(q_ref)
    sel = sel_ref[...]
    qpos = start + (lax.broadcasted_iota(jnp.int32, (N_HEADS * TQ, 1), 0) & (TQ - 1))

    n_chunks = start // CK + 1

    def body(t, carry):
        m, l, acc = carry
        c = n_chunks - 1 - t
        rows = pl.ds(pl.multiple_of(c * CK, CK), CK)
        s = lax.dot_general(qs, ks_ref[rows, :], _NT, preferred_element_type=F32)
        picked = jnp.dot(sel, exp_ref[:, rows], preferred_element_type=F32)
        picked = jnp.concatenate([picked] * N_HEADS, axis=0)
        kpos = c * CK + lax.broadcasted_iota(jnp.int32, s.shape, 1)
        s = jnp.where(jnp.logical_and(picked > 0.5, kpos <= qpos), s, NEG_INF)
        m_new = jnp.maximum(m, jnp.max(s, axis=1, keepdims=True))
        a = jnp.exp(m - m_new)
        p = jnp.exp(s - m_new)
        l = a * l + jnp.sum(p, axis=1, keepdims=True)
        acc = a * acc + jnp.dot(p.astype(BF16), vs_ref[rows, :], preferred_element_type=F32)
        return m_new, l, acc

    init = (jnp.full((N_HEADS * TQ, 1), NEG_INF, F32), jnp.zeros((N_HEADS * TQ, 1), F32),
            jnp.zeros((N_HEADS * TQ, HEAD_DIM), F32))
    _, l, acc = lax.fori_loop(0, n_chunks, body, init)
    o_sel = acc / l

    span = TQ + NSA_WINDOW
    w0 = jnp.maximum(start - NSA_WINDOW, 0)
    rows = pl.ds(pl.multiple_of(w0, TQ), span)
    s = lax.dot_general(qs, kw_ref[rows, :], _NT, preferred_element_type=F32)
    wpos = w0 + lax.broadcasted_iota(jnp.int32, s.shape, 1)
    s = jnp.where(jnp.logical_and(wpos <= qpos, wpos > qpos - NSA_WINDOW), s, NEG_INF)
    p = jnp.exp(s - jnp.max(s, axis=1, keepdims=True))
    o_win = jnp.dot(p.astype(BF16), vw_ref[rows, :], preferred_element_type=F32) / jnp.sum(p, axis=1, keepdims=True)

    g = g_ref[...]
    for h in range(N_HEADS):
        hs = slice(h * TQ, (h + 1) * TQ)
        cs = slice(h * HEAD_DIM, (h + 1) * HEAD_DIM)
        o_ref[:, cs] = (g[:, 3 * h:3 * h + 1] * oc_ref[:, cs] + g[:, 3 * h + 1:3 * h + 2] * o_sel[hs]
                        + g[:, 3 * h + 2:3 * h + 3] * o_win[hs]).astype(BF16)


def _nsa_sel(nq_rope, nkv, sel, expand, o_cmp, gates):
    L = nq_rope.shape[0]
    TQ = Q_BLOCK
    ns = L // NSA_SEL_BLOCK
    col = lambda j: _resident((L, HEAD_DIM), lambda i, j=j: (0, j))
    return pl.pallas_call(
        _nsa_sel_kernel,
        grid=(L // TQ,),
        in_specs=[pl.BlockSpec((TQ, GROUP_WIDTH), lambda i: (i, 0)),
                  col(0), col(1), col(2), col(3),
                  pl.BlockSpec((TQ, ns), lambda i: (i, 0)),
                  _resident((ns, L), lambda i: (0, 0)),
                  pl.BlockSpec((TQ, GROUP_WIDTH), lambda i: (i, 0)),
                  pl.BlockSpec((TQ, HEAD_DIM), lambda i: (i, 0))],
        out_specs=pl.BlockSpec((TQ, GROUP_WIDTH), lambda i: (i, 0)),
        out_shape=jax.ShapeDtypeStruct((L, GROUP_WIDTH), BF16),
        compiler_params=_cparams(("parallel",)),
        name="nsa_sel_win",
    )(nq_rope, nkv, nkv, nkv, nkv, sel, expand, o_cmp, gates)


def _matmul_kernel(a_ref, b_ref, o_ref):
    o_ref[...] = jnp.dot(a_ref[...], b_ref[...], preferred_element_type=F32).astype(o_ref.dtype)


def _mem_proj(mem_b, w):
    n, width = mem_b.shape[0], w.shape[1]
    return pl.pallas_call(
        _matmul_kernel,
        grid=(width // 512,),
        in_specs=[pl.BlockSpec((n, D_MODEL), lambda j: (0, 0)), pl.BlockSpec((D_MODEL, 512), lambda j: (0, j))],
        out_specs=pl.BlockSpec((n, 512), lambda j: (0, j)),
        out_shape=jax.ShapeDtypeStruct((n, width), BF16),
        compiler_params=_cparams(("parallel",)),
        name="mem_proj",
    )(mem_b, w)


def _xattn_kernel(q_ref, k_ref, v_ref, o_ref):
    for h in range(N_HEADS):
        cs = slice(h * HEAD_DIM, (h + 1) * HEAD_DIM)
        s = lax.dot_general(q_ref[:, cs], k_ref[:, cs], _NT, preferred_element_type=F32)
        p = jnp.exp(s - jnp.max(s, axis=1, keepdims=True))
        o = jnp.dot(p.astype(BF16), v_ref[:, cs], preferred_element_type=F32)
        o_ref[:, cs] = (o / jnp.sum(p, axis=1, keepdims=True)).astype(BF16)


def _xattn(xq, mem_kv):
    L = xq.shape[0]
    n = mem_kv.shape[0]
    row = pl.BlockSpec((ROW_TILE, GROUP_WIDTH), lambda i: (i, 0))
    return pl.pallas_call(
        _xattn_kernel,
        grid=(L // ROW_TILE,),
        in_specs=[row, pl.BlockSpec((n, GROUP_WIDTH), lambda i: (0, 0)),
                  pl.BlockSpec((n, GROUP_WIDTH), lambda i: (0, 1))],
        out_specs=row,
        out_shape=jax.ShapeDtypeStruct((L, GROUP_WIDTH), BF16),
        compiler_params=_cparams(("parallel",)),
        name="xattn",
    )(xq, mem_kv, mem_kv)


def _out_proj_kernel(y0_ref, y1_ref, y2_ref, y3_ref, w_ref, x_ref, g_ref, b_ref, o_ref, ob_ref):
    acc = ALPHA * x_ref[...]
    for j, y_ref in enumerate((y0_ref, y1_ref, y2_ref, y3_ref)):
        acc = acc + jnp.dot(y_ref[...], w_ref[j * GROUP_WIDTH:(j + 1) * GROUP_WIDTH, :], preferred_element_type=F32)
    y = _layer_norm(acc, g_ref[...], b_ref[...])
    o_ref[...] = y
    ob_ref[...] = y.astype(BF16)


def _out_proj(ys, w_o, x, g, b):
    L = x.shape[0]
    tm = ROW_TILE
    part = pl.BlockSpec((tm, GROUP_WIDTH), lambda i: (i, 0))
    row = pl.BlockSpec((tm, D_MODEL), lambda i: (i, 0))
    vec = pl.BlockSpec((1, D_MODEL), lambda i: (0, 0))
    return pl.pallas_call(
        _out_proj_kernel,
        grid=(L // tm,),
        in_specs=[part, part, part, part, _resident((D_MODEL, D_MODEL), lambda i: (0, 0)), row, vec, vec],
        out_specs=[row, row],
        out_shape=[jax.ShapeDtypeStruct((L, D_MODEL), F32), jax.ShapeDtypeStruct((L, D_MODEL), BF16)],
        compiler_params=_cparams(("parallel",)),
        name="out_proj_ln",
    )(*ys, w_o, x, g.reshape(1, -1), b.reshape(1, -1))


def _swiglu_step(x_ref, wg_ref, wu_ref, wd_ref, acc_ref):
    f = pl.program_id(1)
    x = x_ref[...]
    gate = jnp.dot(x, wg_ref[0], preferred_element_type=F32)
    up = jnp.dot(x, wu_ref[0], preferred_element_type=F32)
    hid = (gate * _sigmoid(gate) * up).astype(BF16)
    part = jnp.dot(hid, wd_ref[0], preferred_element_type=F32)

    @pl.when(f == 0)
    def _():
        acc_ref[...] = part

    @pl.when(f > 0)
    def _():
        acc_ref[...] += part


def _ffn_dense_kernel(x_ref, wg_ref, wu_ref, wd_ref, xr_ref, g_ref, b_ref, o_ref, ob_ref, acc_ref):
    _swiglu_step(x_ref, wg_ref, wu_ref, wd_ref, acc_ref)

    @pl.when(pl.program_id(1) == pl.num_programs(1) - 1)
    def _():
        y = _layer_norm(ALPHA * xr_ref[...] + acc_ref[...], g_ref[...], b_ref[...])
        o_ref[...] = y
        ob_ref[...] = y.astype(BF16)


def _ffn_dense(xb, x, wg, wu, wd, g, b):
    L = x.shape[0]
    tm, tf = ROW_TILE, FF_TILE
    row = pl.BlockSpec((tm, D_MODEL), lambda i, f: (i, 0))
    vec = pl.BlockSpec((1, D_MODEL), lambda i, f: (0, 0))
    return pl.pallas_call(
        _ffn_dense_kernel,
        grid=(L // tm, D_FF // tf),
        in_specs=[row,
                  pl.BlockSpec((1, D_MODEL, tf), lambda i, f: (0, 0, f)),
                  pl.BlockSpec((1, D_MODEL, tf), lambda i, f: (0, 0, f)),
                  pl.BlockSpec((1, tf, D_MODEL), lambda i, f: (0, f, 0)),
                  row, vec, vec],
        out_specs=[row, row],
        out_shape=[jax.ShapeDtypeStruct((L, D_MODEL), F32), jax.ShapeDtypeStruct((L, D_MODEL), BF16)],
        scratch_shapes=[pltpu.VMEM((tm, D_MODEL), F32)],
        compiler_params=_cparams(("parallel", "arbitrary")),
        name="ffn_dense_ln",
    )(xb, wg[None], wu[None], wd[None], x, g.reshape(1, -1), b.reshape(1, -1))


def _ffn_moe_kernel(exp_ref, used_ref, x_ref, wg_ref, wu_ref, wd_ref, o_ref, acc_ref):
    live = pl.program_id(0) < used_ref[0]

    @pl.when(live)
    def _():
        _swiglu_step(x_ref, wg_ref, wu_ref, wd_ref, acc_ref)

    @pl.when(pl.program_id(1) == pl.num_programs(1) - 1)
    def _():
        o_ref[...] = jnp.where(live, acc_ref[...], 0.0)


def _ffn_moe(xg, blk_exp, n_used, wg, wu, wd):
    P = xg.shape[0]
    tm, tf = MOE_ROWS, FF_TILE
    nf = D_FF // tf
    ff = lambda b, f, used: jnp.where(b < used[0], f, nf - 1)
    grid_spec = pltpu.PrefetchScalarGridSpec(
        num_scalar_prefetch=2,
        grid=(P // tm, nf),
        in_specs=[pl.BlockSpec((tm, D_MODEL), lambda b, f, e, u: (b, 0)),
                  pl.BlockSpec((1, D_MODEL, tf), lambda b, f, e, u: (e[b], 0, ff(b, f, u))),
                  pl.BlockSpec((1, D_MODEL, tf), lambda b, f, e, u: (e[b], 0, ff(b, f, u))),
                  pl.BlockSpec((1, tf, D_MODEL), lambda b, f, e, u: (e[b], ff(b, f, u), 0))],
        out_specs=pl.BlockSpec((tm, D_MODEL), lambda b, f, e, u: (b, 0)),
        scratch_shapes=[pltpu.VMEM((tm, D_MODEL), F32)],
    )
    return pl.pallas_call(
        _ffn_moe_kernel,
        grid_spec=grid_spec,
        out_shape=jax.ShapeDtypeStruct((P, D_MODEL), F32),
        compiler_params=_cparams(("parallel", "arbitrary")),
        name="ffn_moe",
    )(blk_exp, n_used, xg, wg, wu, wd)


def _router_kernel(x_ref, r_ref, o_ref):
    xs = _split3(x_ref[...])
    rs = (r_ref[0], r_ref[1], r_ref[2])
    logits = None
    for i, j in ((0, 0), (0, 1), (1, 0), (1, 1), (0, 2), (2, 0)):
        t = jnp.dot(xs[i], rs[j], preferred_element_type=F32)
        logits = t if logits is None else logits + t
    lane = lax.broadcasted_iota(jnp.int32, logits.shape, 1)
    lane_f = lane.astype(F32)
    logits = jnp.where(lane < N_EXPERTS, logits, NEG_INF)
    m1 = jnp.max(logits, axis=1, keepdims=True)
    i1 = jnp.min(jnp.where(logits == m1, lane_f, 128.0), axis=1, keepdims=True)
    rest = jnp.where(lane_f == i1, NEG_INF, logits)
    m2 = jnp.max(rest, axis=1, keepdims=True)
    i2 = jnp.min(jnp.where(rest == m2, lane_f, 128.0), axis=1, keepdims=True)
    e2 = jnp.exp(m2 - m1)
    g1 = 1.0 / (1.0 + e2)
    g2 = e2 / (1.0 + e2)
    o_ref[...] = jnp.where(lane == 0, i1, jnp.where(lane == 1, i2, jnp.where(lane == 2, g1, jnp.where(lane == 3, g2, 0.0))))


def _router(x, router3):
    L = x.shape[0]
    return pl.pallas_call(
        _router_kernel,
        grid=(L // ROW_TILE,),
        in_specs=[pl.BlockSpec((ROW_TILE, D_MODEL), lambda i: (i, 0)),
                  pl.BlockSpec((3, D_MODEL, 128), lambda i: (0, 0, 0))],
        out_specs=pl.BlockSpec((ROW_TILE, 128), lambda i: (i, 0)),
        out_shape=jax.ShapeDtypeStruct((L, 128), F32),
        compiler_params=_cparams(("parallel",)),
        name="moe_router",
    )(x, router3)


def _combine_kernel(x_ref, ya_ref, yb_ref, gw_ref, g_ref, b_ref, o_ref, ob_ref):
    gw = gw_ref[...]
    f = gw[:, 2:3] * ya_ref[...] + gw[:, 3:4] * yb_ref[...]
    y = _layer_norm(ALPHA * x_ref[...] + f, g_ref[...], b_ref[...])
    o_ref[...] = y
    ob_ref[...] = y.astype(BF16)


def _combine(x, ya, yb, route, g, b):
    L = x.shape[0]
    row = pl.BlockSpec((ROW_TILE, D_MODEL), lambda i: (i, 0))
    vec = pl.BlockSpec((1, D_MODEL), lambda i: (0, 0))
    return pl.pallas_call(
        _combine_kernel,
        grid=(L // ROW_TILE,),
        in_specs=[row, row, row, pl.BlockSpec((ROW_TILE, 128), lambda i: (i, 0)), vec, vec],
        out_specs=[row, row],
        out_shape=[jax.ShapeDtypeStruct((L, D_MODEL), F32), jax.ShapeDtypeStruct((L, D_MODEL), BF16)],
        compiler_params=_cparams(("parallel",)),
        name="moe_combine_ln",
    )(x, ya, yb, route, g.reshape(1, -1), b.reshape(1, -1))


def _moe(x, xb, router, wg, wu, wd, g, b):
    T = x.shape[0]
    r32 = jnp.pad(router, ((0, 0), (0, 128 - N_EXPERTS)))
    r1 = r32.astype(BF16)
    r2 = (r32 - r1.astype(F32)).astype(BF16)
    r3 = (r32 - r1.astype(F32) - r2.astype(F32)).astype(BF16)
    route = _router(x, jnp.stack([r1, r2, r3]))
    e_flat = route[:, :TOP_K].astype(jnp.int32).reshape(-1)
    onehot = (e_flat[:, None] == jnp.arange(N_EXPERTS, dtype=jnp.int32)[None, :]).astype(jnp.int32)
    csum = jnp.cumsum(onehot, axis=0)
    rank = jnp.sum(onehot * csum, axis=1) - 1
    counts = csum[-1]
    padded = (counts + MOE_ROWS - 1) // MOE_ROWS * MOE_ROWS
    pend = jnp.cumsum(padded)
    dest = (pend - padded)[e_flat] + rank
    n_blocks = -(-(T * TOP_K + N_EXPERTS * (MOE_ROWS - 1)) // MOE_ROWS)
    P = n_blocks * MOE_ROWS
    tok = jnp.repeat(jnp.arange(T, dtype=jnp.int32), TOP_K)
    buf_tok = jnp.zeros((P,), jnp.int32).at[dest].set(tok)
    n_used = (pend[-1] // MOE_ROWS).astype(jnp.int32).reshape(1)
    blk_start = jnp.arange(n_blocks, dtype=jnp.int32) * MOE_ROWS
    blk_exp = jnp.minimum(jnp.searchsorted(pend, jnp.minimum(blk_start, pend[-1] - 1), side='right'),
                          N_EXPERTS - 1).astype(jnp.int32)
    xg = jnp.take(xb, buf_tok, axis=0)
    yb = _ffn_moe(xg, blk_exp, n_used, wg, wu, wd)
    d2 = dest.reshape(T, TOP_K)
    return _combine(x, jnp.take(yb, d2[:, 0], axis=0), jnp.take(yb, d2[:, 1], axis=0), route, g, b)


def _rope_tables(L):
    inv_freq = 1.0 / (ROPE_THETA ** (jnp.arange(0, ROPE_DIM, 2, dtype=F32) / ROPE_DIM))
    ang = jnp.arange(L, dtype=F32)[:, None] * inv_freq[None, :]
    cos, sin = jnp.cos(ang), jnp.sin(ang)
    z = lambda n: jnp.zeros((L, n), F32)
    cosf = jnp.concatenate([cos, cos, jnp.ones((L, HEAD_DIM - ROPE_DIM), F32)], axis=1)
    sa = jnp.concatenate([-sin, z(HEAD_DIM - 16)], axis=1)
    sb = jnp.concatenate([z(16), sin, z(HEAD_DIM - ROPE_DIM)], axis=1)
    return cosf, sa, sb


def _reorder_w_in(w):
    return jnp.concatenate([w[:, :3328], w[:, 3340:3852], w[:, 3328:3340],
                            jnp.zeros((D_MODEL, 128 - 12), w.dtype)], axis=1).astype(BF16)


def _trunk(x, mem, p):
    L = x.shape[0]
    ns, nc = L // NSA_SEL_BLOCK, L // NSA_CMP_STRIDE
    cosf, sa, sb = _rope_tables(L)
    cj = jnp.arange(nc)[:, None] - 4 * jnp.arange(ns)[None, :]
    agg = jnp.logical_and(cj >= -1, cj <= 3).astype(BF16)
    expand = (jnp.arange(L)[None, :] // NSA_SEL_BLOCK == jnp.arange(ns)[:, None]).astype(BF16)
    mem_b = mem.astype(BF16)

    x, xb = _ln_in(x, p['ln_in_g'], p['ln_in_b'])
    for i in range(DEPTH):
        u, mq, mk, mv, nqc, nqr, cmp_kv, nkv, xq, gates = _proj(xb, _reorder_w_in(p['w_in'][i]), cosf, sa, sb)
        bmat, cmat, pw = _s5_params(p['ssm_a_re'][i], p['ssm_a_im'][i], p['ssm_log_dt'][i], p['ssm_b_re'][i],
                                    p['ssm_b_im'][i], p['ssm_c_re'][i], p['ssm_c_im'][i])
        y_ssm = _s5(u, bmat, cmat, pw, p['ssm_d'][i], p['ssm_w_glu'][i])
        y_moba = _moba(mq, mk, mv, _kmean(mk))
        pos = jnp.stack([p['nsa_pos_k'][i].reshape(1, -1), p['nsa_pos_v'][i].reshape(1, -1)])
        pos = jnp.broadcast_to(pos, (2, 8, pos.shape[-1])).astype(BF16)
        kvc = _compress(cmp_kv, jnp.stack([p['nsa_ck1'][i], p['nsa_cv1'][i]]).astype(BF16),
                        jnp.stack([p['nsa_ck2'][i], p['nsa_cv2'][i]]).astype(BF16), pos)
        o_cmp, sel = _nsa_cmp(nqc, kvc, agg)
        y_nsa = _nsa_sel(nqr, nkv, sel, expand, o_cmp, gates)
        mem_kv = _mem_proj(mem_b, jnp.concatenate([p['mem_wk'][i], p['mem_wv'][i]], axis=1).astype(BF16))
        y_x = _xattn(xq, mem_kv)
        x, xb = _out_proj((y_ssm, y_moba, y_nsa, y_x), p['w_o'][i].astype(BF16), x, p['ln1_g'][i], p['ln1_b'][i])
        j = i // 2
        if i % 2 == 0:
            x, xb = _ffn_dense(xb, x, p['ffn_w_gate'][j].astype(BF16), p['ffn_w_up'][j].astype(BF16),
                               p['ffn_w_down'][j].astype(BF16), p['ln2_g'][i], p['ln2_b'][i])
        else:
            x, xb = _moe(x, xb, p['moe_router'][j], p['moe_w_gate'][j].astype(BF16), p['moe_w_up'][j].astype(BF16),
                         p['moe_w_down'][j].astype(BF16), p['ln2_g'][i], p['ln2_b'][i])
    return x


def kernel(x, mem, ln_in_g, ln_in_b, w_in, ssm_a_re, ssm_a_im, ssm_log_dt, ssm_b_re, ssm_b_im, ssm_c_re, ssm_c_im,
           ssm_d, ssm_w_glu, nsa_pos_k, nsa_pos_v, nsa_ck1, nsa_ck2, nsa_cv1, nsa_cv2, mem_wk, mem_wv, w_o, ln1_g,
           ln1_b, ln2_g, ln2_b, ffn_w_gate, ffn_w_up, ffn_w_down, moe_router, moe_w_gate, moe_w_up, moe_w_down):
    p = dict(ln_in_g=ln_in_g, ln_in_b=ln_in_b, w_in=w_in, ssm_a_re=ssm_a_re, ssm_a_im=ssm_a_im,
             ssm_log_dt=ssm_log_dt, ssm_b_re=ssm_b_re, ssm_b_im=ssm_b_im, ssm_c_re=ssm_c_re, ssm_c_im=ssm_c_im,
             ssm_d=ssm_d, ssm_w_glu=ssm_w_glu, nsa_pos_k=nsa_pos_k, nsa_pos_v=nsa_pos_v, nsa_ck1=nsa_ck1,
             nsa_ck2=nsa_ck2, nsa_cv1=nsa_cv1, nsa_cv2=nsa_cv2, mem_wk=mem_wk, mem_wv=mem_wv, w_o=w_o,
             ln1_g=ln1_g, ln1_b=ln1_b, ln2_g=ln2_g, ln2_b=ln2_b, ffn_w_gate=ffn_w_gate, ffn_w_up=ffn_w_up,
             ffn_w_down=ffn_w_down, moe_router=moe_router, moe_w_gate=moe_w_gate, moe_w_up=moe_w_up,
             moe_w_down=moe_w_down)
    assert x.shape[0] == 1 and mem.shape[0] == 1
    return _trunk(x[0], mem[0], p)[None]
```

```python
import functools
import math

import jax
import jax.numpy as jnp
from jax import lax
from jax.experimental import pallas as pl
from jax.experimental.pallas import tpu as pltpu

F32 = jnp.float32
BF16 = jnp.bfloat16

D_MODEL = 2048
DEPTH = 4
HEAD_DIM = 128
GROUP_WIDTH = 512
N_HEADS = 4
SSM_CH = 512
SSM_GROUP = 16
SSM_NG = 32
SSM_STATE = 64
SSM_NSTATE = SSM_NG * SSM_STATE
MOBA_BLOCK = 256
MOBA_TOPK = 3
NSA_CMP_LEN = 32
NSA_CMP_STRIDE = 16
NSA_SEL_BLOCK = 64
NSA_TOPK = 16
NSA_WINDOW = 512
ROPE_THETA = 500000.0
ROPE_DIM = 32
D_FF = 5632
N_EXPERTS = 8
TOP_K = 2
Q_BLOCK = 128
LN_EPS = 1e-5
NEG_INF = -1e30
FORCE_SCORE = 1e9
ALPHA = (2.0 * DEPTH) ** 0.25
SCALE = HEAD_DIM ** -0.5 * math.log2(math.e)

ROW_TILE = 512
SCAN_ROWS = 512
SCAN_LANES = 256
SEL_CHUNK = 512
FF_TILE = 512
MOE_ROWS = 1024
MOE_FF_TILE = 256
PROJ_WIDTH = 3968

_NT = (((1,), (1,)), ((), ()))


def _cparams(sem, vmem_mb=48):
    return pltpu.CompilerParams(dimension_semantics=sem, vmem_limit_bytes=vmem_mb * 1024 * 1024)


def _resident(block_shape, index_map):
    return pl.BlockSpec(block_shape, index_map, pipeline_mode=pl.Buffered(1))


def _layer_norm(r, g, b):
    mu = jnp.mean(r, axis=-1, keepdims=True)
    d = r - mu
    var = jnp.mean(d * d, axis=-1, keepdims=True)
    return d * lax.rsqrt(var + LN_EPS) * g + b


def _gelu_tanh(x):
    return x * (0.5 * (1.0 + jnp.tanh(math.sqrt(2.0 / math.pi) * (x + 0.044715 * (x * x * x)))))


def _sigmoid(x):
    return 1.0 / (1.0 + jnp.exp(-x))


def _split3(a):
    a1 = a.astype(BF16)
    r1 = a - a1.astype(F32)
    a2 = r1.astype(BF16)
    a3 = (r1 - a2.astype(F32)).astype(BF16)
    return a1, a2, a3


def _topk_mask(score, ids, k, n):
    def body(_, carry):
        sc, sel = carry
        m = jnp.max(sc, axis=1, keepdims=True)
        idx = jnp.min(jnp.where(sc == m, ids, float(n)), axis=1, keepdims=True)
        hit = ids == idx
        sel = jnp.where(jnp.logical_and(hit, m > 0.5 * NEG_INF), 1.0, sel)
        sc = jnp.where(hit, NEG_INF, sc)
        return sc, sel

    _, sel = lax.fori_loop(0, k, body, (score, jnp.zeros(score.shape, F32)))
    return sel


def _ln_in_kernel(x_ref, g_ref, b_ref, o_ref, ob_ref):
    y = _layer_norm(x_ref[...], g_ref[...], b_ref[...])
    o_ref[...] = y
    ob_ref[...] = y.astype(BF16)


def _ln_in(x, g, b):
    L = x.shape[0]
    row = pl.BlockSpec((ROW_TILE, D_MODEL), lambda i: (i, 0))
    vec = pl.BlockSpec((1, D_MODEL), lambda i: (0, 0))
    return pl.pallas_call(
        _ln_in_kernel,
        grid=(L // ROW_TILE,),
        in_specs=[row, vec, vec],
        out_specs=[row, row],
        out_shape=[jax.ShapeDtypeStruct((L, D_MODEL), F32), jax.ShapeDtypeStruct((L, D_MODEL), BF16)],
        compiler_params=_cparams(("parallel",)),
        name="ln_in",
    )(x, g.reshape(1, -1), b.reshape(1, -1))


def _proj_kernel(x_ref, w_ref, cos_ref, sa_ref, sb_ref,
                 u_ref, mq_ref, mk_ref, mv_ref, nqc_ref, nqr_ref, cmp_ref, nkv_ref, xq_ref, g_ref):
    x = x_ref[...]
    cos, sa, sb = cos_ref[...], sa_ref[...], sb_ref[...]

    def dot(c0, n):
        return jnp.dot(x, w_ref[:, c0:c0 + n], preferred_element_type=F32)

    def rope(t):
        return t * cos + pltpu.roll(t, HEAD_DIM - 16, 1) * sa + pltpu.roll(t, 16, 1) * sb

    u_ref[...] = dot(0, 512)
    for h in range(N_HEADS):
        c = h * HEAD_DIM
        mq_ref[:, c:c + HEAD_DIM] = (rope(dot(512 + c, HEAD_DIM)) * SCALE).astype(BF16)
        mk_ref[:, c:c + HEAD_DIM] = rope(dot(1024 + c, HEAD_DIM)).astype(BF16)
        t = dot(2048 + c, HEAD_DIM)
        nqc_ref[:, c:c + HEAD_DIM] = (t * SCALE).astype(BF16)
        nqr_ref[:, c:c + HEAD_DIM] = (rope(t) * SCALE).astype(BF16)
    mv_ref[...] = dot(1536, 512).astype(BF16)
    cmp_ref[0] = dot(2560, HEAD_DIM).astype(BF16)
    cmp_ref[1] = dot(2688, HEAD_DIM).astype(BF16)
    nkv_ref[:, 0:128] = rope(dot(2816, HEAD_DIM)).astype(BF16)
    nkv_ref[:, 128:256] = dot(2944, HEAD_DIM).astype(BF16)
    nkv_ref[:, 256:384] = rope(dot(3072, HEAD_DIM)).astype(BF16)
    nkv_ref[:, 384:512] = dot(3200, HEAD_DIM).astype(BF16)
    xq_ref[...] = (dot(3328, 512) * SCALE).astype(BF16)
    g_ref[...] = _sigmoid(dot(3840, HEAD_DIM))


def _proj(xb, w, cosf, sa, sb):
    L = xb.shape[0]
    tm = ROW_TILE
    row = lambda n: pl.BlockSpec((tm, n), lambda i: (i, 0))
    outs = [
        (jax.ShapeDtypeStruct((L, 512), F32), row(512)),
        (jax.ShapeDtypeStruct((L, 512), BF16), row(512)),
        (jax.ShapeDtypeStruct((L, 512), BF16), row(512)),
        (jax.ShapeDtypeStruct((L, 512), BF16), row(512)),
        (jax.ShapeDtypeStruct((L, 512), BF16), row(512)),
        (jax.ShapeDtypeStruct((L, 512), BF16), row(512)),
        (jax.ShapeDtypeStruct((2, L, 128), BF16), pl.BlockSpec((2, tm, 128), lambda i: (0, i, 0))),
        (jax.ShapeDtypeStruct((L, 512), BF16), row(512)),
        (jax.ShapeDtypeStruct((L, 512), BF16), row(512)),
        (jax.ShapeDtypeStruct((L, 128), F32), row(128)),
    ]
    return pl.pallas_call(
        _proj_kernel,
        grid=(L // tm,),
        in_specs=[row(D_MODEL), _resident((D_MODEL, PROJ_WIDTH), lambda i: (0, 0)),
                  row(128), row(128), row(128)],
        out_specs=[o[1] for o in outs],
        out_shape=[o[0] for o in outs],
        compiler_params=_cparams(("parallel",), 56),
        name="in_proj",
    )(xb, w, cosf, sa, sb)


def _s5_kernel(u_ref, bmat_ref, cmat_ref, pw_ref, d_ref, wglu_ref, o_ref, hr_ref, hi_ref, car_ref, cai_ref):
    T = u_ref.shape[0]
    W = SCAN_LANES

    @pl.when(pl.program_id(0) == 0)
    def _():
        car_ref[...] = jnp.zeros(car_ref.shape, F32)
        cai_ref[...] = jnp.zeros(cai_ref.shape, F32)

    u = u_ref[...]
    ub = u.astype(BF16)
    for b in range(4):
        bu = jnp.dot(ub[:, b * 128:(b + 1) * 128], bmat_ref[b], preferred_element_type=F32)
        hr_ref[:, b * 512:(b + 1) * 512] = bu[:, :512]
        hi_ref[:, b * 512:(b + 1) * 512] = bu[:, 512:]

    for cb in range(SSM_NSTATE // W):
        sl = slice(cb * W, (cb + 1) * W)
        steps = [(pw_ref[2 * j, :, sl], pw_ref[2 * j + 1, :, sl], 1 << j) for j in range(3)]
        p_r, p_i = pw_ref[6, :, sl], pw_ref[7, :, sl]

        def tile(t, carry, sl=sl, steps=steps, p_r=p_r, p_i=p_i):
            c_r, c_i = carry
            rows = pl.ds(pl.multiple_of(t * 8, 8), 8)
            x_r, x_i = hr_ref[rows, sl], hi_ref[rows, sl]
            for l_r, l_i, d in steps:
                s_r, s_i = pltpu.roll(x_r, d, 0), pltpu.roll(x_i, d, 0)
                x_r, x_i = x_r + l_r * s_r - l_i * s_i, x_i + l_r * s_i + l_i * s_r
            x_r, x_i = x_r + p_r * c_r - p_i * c_i, x_i + p_r * c_i + p_i * c_r
            hr_ref[rows, sl] = x_r
            hi_ref[rows, sl] = x_i
            return (jnp.broadcast_to(x_r[7:8, :], (8, W)), jnp.broadcast_to(x_i[7:8, :], (8, W)))

        c_r, c_i = lax.fori_loop(0, T // 8, tile, (car_ref[:, sl], cai_ref[:, sl]))
        car_ref[:, sl] = c_r
        cai_ref[:, sl] = c_i

    ys = []
    for b in range(4):
        h_r = hr_ref[:, b * 512:(b + 1) * 512].astype(BF16)
        h_i = hi_ref[:, b * 512:(b + 1) * 512].astype(BF16)
        ys.append(jnp.dot(h_r, cmat_ref[b, :512, :], preferred_element_type=F32)
                  + jnp.dot(h_i, cmat_ref[b, 512:, :], preferred_element_type=F32))
    y = jnp.concatenate(ys, axis=1) + d_ref[...] * u
    z = _gelu_tanh(y)
    gate = _sigmoid(jnp.dot(z.astype(BF16), wglu_ref[...], preferred_element_type=F32))
    o_ref[...] = (z * gate).astype(BF16)


def _s5_params(a_re, a_im, log_dt, b_re, b_im, c_re, c_im):
    dt = jnp.exp(log_dt)[:, None]
    mag = jnp.exp(a_re * dt)
    lam_re, lam_im = mag * jnp.cos(a_im * dt), mag * jnp.sin(a_im * dt)
    den = a_re * a_re + a_im * a_im
    nr, ni = lam_re - 1.0, lam_im
    coef_re = (nr * a_re + ni * a_im) / den
    coef_im = (ni * a_re - nr * a_im) / den
    bbar_re = coef_re[..., None] * b_re - coef_im[..., None] * b_im
    bbar_im = coef_re[..., None] * b_im + coef_im[..., None] * b_re
    eye = jnp.eye(8, dtype=F32)
    to_b = lambda m: jnp.einsum('bgnc,gh->bgchn', m.reshape(4, 8, SSM_STATE, SSM_GROUP), eye).reshape(4, 128, 512)
    bmat = jnp.concatenate([to_b(bbar_re), to_b(bbar_im)], axis=-1).astype(BF16)
    to_c = lambda m: jnp.einsum('bgcn,gh->bgnhc', m.reshape(4, 8, SSM_GROUP, SSM_STATE), eye).reshape(4, 512, 128)
    cmat = jnp.concatenate([to_c(c_re), -to_c(c_im)], axis=1).astype(BF16)
    lr, li = lam_re.reshape(-1), lam_im.reshape(-1)
    pr, pi = [lr], [li]
    for _ in range(7):
        pr, pi = pr + [pr[-1] * lr - pi[-1] * li], pi + [pr[-1] * li + pi[-1] * lr]
    row = jnp.arange(8)[:, None]
    tabs = []
    for d in (1, 2, 4):
        tabs += [jnp.where(row >= d, pr[d - 1][None, :], 0.0), jnp.where(row >= d, pi[d - 1][None, :], 0.0)]
    tabs += [jnp.stack(pr, axis=0), jnp.stack(pi, axis=0)]
    return bmat, cmat, jnp.stack(tabs, axis=0)


def _s5(u, bmat, cmat, pw, d_skip, w_glu):
    L = u.shape[0]
    T = SCAN_ROWS
    row = pl.BlockSpec((T, 512), lambda i: (i, 0))
    full = lambda shape: pl.BlockSpec(shape, lambda i: (0,) * len(shape))
    return pl.pallas_call(
        _s5_kernel,
        grid=(L // T,),
        in_specs=[row, full((4, 128, 1024)), full((4, 1024, 128)), full((8, 8, SSM_NSTATE)),
                  full((1, 512)), full((512, 512))],
        out_specs=row,
        out_shape=jax.ShapeDtypeStruct((L, 512), BF16),
        scratch_shapes=[pltpu.VMEM((T, SSM_NSTATE), F32), pltpu.VMEM((T, SSM_NSTATE), F32),
                        pltpu.VMEM((8, SSM_NSTATE), F32), pltpu.VMEM((8, SSM_NSTATE), F32)],
        compiler_params=_cparams(("arbitrary",)),
        name="s5",
    )(u, bmat, cmat, pw, d_skip.reshape(1, -1), w_glu.astype(BF16))


def _kmean_kernel(k_ref, o_ref):
    rows = k_ref.shape[0]
    k = k_ref[...].astype(F32).reshape(rows // MOBA_BLOCK, MOBA_BLOCK, GROUP_WIDTH)
    o_ref[...] = jnp.mean(k, axis=1).astype(BF16)


def _kmean(k):
    L = k.shape[0]
    rows = 8 * MOBA_BLOCK if L % (8 * MOBA_BLOCK) == 0 else L
    return pl.pallas_call(
        _kmean_kernel,
        grid=(L // rows,),
        in_specs=[pl.BlockSpec((rows, GROUP_WIDTH), lambda i: (i, 0))],
        out_specs=pl.BlockSpec((rows // MOBA_BLOCK, GROUP_WIDTH), lambda i: (i, 0)),
        out_shape=jax.ShapeDtypeStruct((L // MOBA_BLOCK, GROUP_WIDTH), BF16),
        compiler_params=_cparams(("parallel",)),
        name="moba_kmean",
    )(k)


def _with_ones(v):
    return jnp.concatenate([v, jnp.ones(v.shape, v.dtype)], axis=1)


def _flash_step(s, v_ones, m_ref, acc_ref, idx, first):
    row_max = jnp.max(s, axis=1, keepdims=True)
    if first:
        m_new = jnp.broadcast_to(row_max, (s.shape[0], HEAD_DIM))
    else:
        m_old = m_ref[idx]
        m_new = jnp.maximum(m_old, row_max)
    p = jnp.concatenate([jnp.exp2(s[:, c:c + HEAD_DIM] - m_new) for c in range(0, s.shape[1], HEAD_DIM)], axis=1)
    pv = jnp.dot(p.astype(BF16), v_ones, preferred_element_type=F32)
    if first:
        acc_ref[idx] = pv
    else:
        a = jnp.exp2(m_old - m_new)
        acc_ref[idx] = jnp.concatenate([a, a], axis=1) * acc_ref[idx] + pv
    m_ref[idx] = m_new


def _flash_result(acc_ref, idx):
    acc = acc_ref[idx]
    return acc[:, :HEAD_DIM] / acc[:, HEAD_DIM:]


def _moba_kernel(q_ref, k_ref, v_ref, km_ref, o_ref, qa_ref, m_ref, acc_ref):
    TQ = q_ref.shape[0]
    own = pl.program_id(0)
    heads = [slice(h * HEAD_DIM, (h + 1) * HEAD_DIM) for h in range(N_HEADS)]

    gs = jnp.concatenate([lax.dot_general(q_ref[:, cs], km_ref[:, cs], _NT, preferred_element_type=F32)
                          for cs in heads], axis=0)
    blk = lax.broadcasted_iota(jnp.int32, gs.shape, 1)
    sel = _topk_mask(jnp.where(blk < own, gs, NEG_INF), blk.astype(F32), MOBA_TOPK, HEAD_DIM)
    bias = jnp.where(sel > 0.5, 0.0, NEG_INF).astype(BF16)
    for h, cs in enumerate(heads):
        qa_ref[h] = jnp.concatenate([q_ref[:, cs], bias[h * TQ:(h + 1) * TQ]], axis=1)

    own_rows = pl.ds(pl.multiple_of(own * MOBA_BLOCK, MOBA_BLOCK), MOBA_BLOCK)
    causal = (lax.broadcasted_iota(jnp.int32, (TQ, MOBA_BLOCK), 1)
              <= lax.broadcasted_iota(jnp.int32, (TQ, MOBA_BLOCK), 0))
    for h, cs in enumerate(heads):
        s = lax.dot_general(q_ref[:, cs], k_ref[own_rows, cs], _NT, preferred_element_type=F32)
        _flash_step(jnp.where(causal, s, NEG_INF), _with_ones(v_ref[own_rows, cs]), m_ref, acc_ref, h, True)

    lane = lax.broadcasted_iota(jnp.int32, (MOBA_BLOCK, HEAD_DIM), 1)

    def body(j, carry):
        rows = pl.ds(pl.multiple_of(j * MOBA_BLOCK, MOBA_BLOCK), MOBA_BLOCK)
        tag = jnp.where(lane == j, 1.0, 0.0).astype(BF16)
        for h, cs in enumerate(heads):
            k_aug = jnp.concatenate([k_ref[rows, cs], tag], axis=1)
            s = lax.dot_general(qa_ref[h], k_aug, _NT, preferred_element_type=F32)
            _flash_step(s, _with_ones(v_ref[rows, cs]), m_ref, acc_ref, h, False)
        return carry

    lax.fori_loop(0, own, body, 0)
    for h, cs in enumerate(heads):
        o_ref[:, cs] = _flash_result(acc_ref, h).astype(BF16)


def _moba(q, k, v, kmean):
    L = q.shape[0]
    TQ = MOBA_BLOCK
    nb = L // MOBA_BLOCK
    assert nb <= HEAD_DIM
    km = jnp.pad(kmean, ((0, HEAD_DIM - nb), (0, 0)))
    tile = pl.BlockSpec((TQ, GROUP_WIDTH), lambda i: (i, 0))
    return pl.pallas_call(
        _moba_kernel,
        grid=(L // TQ,),
        in_specs=[tile, _resident((L, GROUP_WIDTH), lambda i: (0, 0)), _resident((L, GROUP_WIDTH), lambda i: (0, 0)),
                  _resident((HEAD_DIM, GROUP_WIDTH), lambda i: (0, 0))],
        out_specs=tile,
        out_shape=jax.ShapeDtypeStruct((L, GROUP_WIDTH), BF16),
        scratch_shapes=[pltpu.VMEM((N_HEADS, TQ, 2 * HEAD_DIM), BF16), pltpu.VMEM((N_HEADS, TQ, HEAD_DIM), F32),
                        pltpu.VMEM((N_HEADS, TQ, 2 * HEAD_DIM), F32)],
        compiler_params=_cparams(("parallel",)),
        name="moba",
    )(q, k, v, km)


def _compress_kernel(c_ref, w1_ref, w2_ref, pos_ref, o_ref):
    n = c_ref.shape[1]
    half = w1_ref.shape[1] // 2
    c = c_ref[0]
    a = jnp.dot(c, w1_ref[0, :half, :], preferred_element_type=F32)
    b = jnp.dot(c, w1_ref[0, half:, :], preferred_element_type=F32)
    bias = jnp.dot(pos_ref[0], w1_ref[0], preferred_element_type=F32)[0:1, :]
    hid = _gelu_tanh(a + pltpu.roll(b, n - 1, 0) + bias)
    out = jnp.dot(hid.astype(BF16), w2_ref[0], preferred_element_type=F32)
    last = lax.broadcasted_iota(jnp.int32, out.shape, 0) == n - 1
    o_ref[0] = jnp.where(last, 0.0, out).astype(BF16)


def _compress(cmp_kv, w1, w2, pos):
    L = cmp_kv.shape[1]
    n = L // NSA_CMP_STRIDE
    wide = NSA_CMP_STRIDE * HEAD_DIM
    chunks = cmp_kv.reshape(2, n, wide)
    hidden = w1.shape[-1]
    return pl.pallas_call(
        _compress_kernel,
        grid=(2,),
        in_specs=[pl.BlockSpec((1, n, wide), lambda j: (j, 0, 0)),
                  pl.BlockSpec((1, 2 * wide, hidden), lambda j: (j, 0, 0)),
                  pl.BlockSpec((1, hidden, HEAD_DIM), lambda j: (j, 0, 0)),
                  pl.BlockSpec((1, 8, 2 * wide), lambda j: (j, 0, 0))],
        out_specs=pl.BlockSpec((1, n, HEAD_DIM), lambda j: (j, 0, 0)),
        out_shape=jax.ShapeDtypeStruct((2, n, HEAD_DIM), BF16),
        compiler_params=_cparams(("parallel",)),
        name="nsa_compress",
    )(chunks, w1, w2, pos)


def _stack_heads(ref):
    return jnp.concatenate([ref[:, h * HEAD_DIM:(h + 1) * HEAD_DIM] for h in range(N_HEADS)], axis=0)


def _nsa_cmp_kernel(q_ref, kc_ref, vc_ref, agg_ref, o_ref, sel_ref):
    TQ = q_ref.shape[0]
    nc = kc_ref.shape[1]
    ns = nc * NSA_CMP_STRIDE // NSA_SEL_BLOCK
    start = pl.program_id(0) * TQ
    qs = _stack_heads(q_ref)
    s = lax.dot_general(qs, kc_ref[0], _NT, preferred_element_type=F32)
    cend = lax.broadcasted_iota(jnp.int32, s.shape, 1) * NSA_CMP_STRIDE + (NSA_CMP_LEN - 1)
    ok = cend <= start + (lax.broadcasted_iota(jnp.int32, s.shape, 0) & (TQ - 1))
    s = jnp.where(ok, s, NEG_INF)
    m = jnp.max(s, axis=1, keepdims=True)
    e = jnp.where(ok, jnp.exp2(s - m), 0.0)
    p = e / jnp.maximum(jnp.sum(e, axis=1, keepdims=True), 1e-30)
    oc = jnp.dot(p.astype(BF16), vc_ref[0], preferred_element_type=F32)
    for h in range(N_HEADS):
        o_ref[:, h * HEAD_DIM:(h + 1) * HEAD_DIM] = oc[h * TQ:(h + 1) * TQ, :]

    imp = p[0:TQ] + p[TQ:2 * TQ] + p[2 * TQ:3 * TQ] + p[3 * TQ:4 * TQ]
    agg = agg_ref[...]
    imp_sel = sum(jnp.dot(t, agg, preferred_element_type=F32) for t in _split3(imp))
    blk = lax.broadcasted_iota(jnp.int32, imp_sel.shape, 1)
    own = (start + lax.broadcasted_iota(jnp.int32, imp_sel.shape, 0)) // NSA_SEL_BLOCK
    score = jnp.where(blk == own, FORCE_SCORE, jnp.where(blk < own, imp_sel, NEG_INF))
    sel = _topk_mask(score, blk.astype(F32), min(NSA_TOPK, ns), imp_sel.shape[1])
    sel_ref[...] = jnp.where(sel > 0.5, 0.0, NEG_INF).astype(BF16)


def _nsa_cmp(nq, kvc, agg):
    L = nq.shape[0]
    TQ = 2 * Q_BLOCK
    nc = kvc.shape[1]
    lanes = agg.shape[1]
    return pl.pallas_call(
        _nsa_cmp_kernel,
        grid=(L // TQ,),
        in_specs=[pl.BlockSpec((TQ, GROUP_WIDTH), lambda i: (i, 0)),
                  pl.BlockSpec((1, nc, HEAD_DIM), lambda i: (0, 0, 0)),
                  pl.BlockSpec((1, nc, HEAD_DIM), lambda i: (1, 0, 0)),
                  pl.BlockSpec((nc, lanes), lambda i: (0, 0))],
        out_specs=[pl.BlockSpec((TQ, GROUP_WIDTH), lambda i: (i, 0)),
                   pl.BlockSpec((TQ, lanes), lambda i: (i, 0))],
        out_shape=[jax.ShapeDtypeStruct((L, GROUP_WIDTH), F32), jax.ShapeDtypeStruct((L, lanes), BF16)],
        compiler_params=_cparams(("parallel",)),
        name="nsa_cmp",
    )(nq, kvc, kvc, agg)


def _nsa_sel_kernel(q_ref, ks_ref, vs_ref, kw_ref, vw_ref, sb_ref, tag_ref, oc_ref, g_ref, o_ref,
                    qa_ref, m_ref, acc_ref):
    TQ = q_ref.shape[0]
    CK = tag_ref.shape[1]
    cpg = HEAD_DIM // (CK // NSA_SEL_BLOCK)
    R = N_HEADS * TQ
    halves = (slice(0, R // 2), slice(R // 2, R))
    start = pl.program_id(0) * TQ
    qs = _stack_heads(q_ref)
    qpos = start + (lax.broadcasted_iota(jnp.int32, (R, 1), 0) & (TQ - 1))

    bias = sb_ref[...]
    for g in range(sb_ref.shape[1] // HEAD_DIM):
        qa_ref[g] = jnp.concatenate(
            [qs, jnp.concatenate([bias[:, g * HEAD_DIM:(g + 1) * HEAD_DIM]] * N_HEADS, axis=0)], axis=1)

    def scores(c):
        rows = pl.ds(pl.multiple_of(c * CK, CK), CK)
        k_aug = jnp.concatenate([ks_ref[rows, :], tag_ref[c % cpg]], axis=1)
        return rows, [lax.dot_general(qa_ref[c // cpg, hs, :], k_aug, _NT, preferred_element_type=F32)
                      for hs in halves]

    c_last = start // CK
    rows, ss = scores(c_last)
    kpos = c_last * CK + lax.broadcasted_iota(jnp.int32, ss[0].shape, 1)
    v_ones = _with_ones(vs_ref[rows, :])
    for idx, hs in enumerate(halves):
        _flash_step(jnp.where(kpos <= qpos[hs], ss[idx], NEG_INF), v_ones, m_ref, acc_ref, idx, True)

    def body(c, carry):
        rows, ss = scores(c)
        v_ones = _with_ones(vs_ref[rows, :])
        for idx in range(2):
            _flash_step(ss[idx], v_ones, m_ref, acc_ref, idx, False)
        return carry

    lax.fori_loop(0, c_last, body, 0)
    o_sel = jnp.concatenate([_flash_result(acc_ref, 0), _flash_result(acc_ref, 1)], axis=0)

    span = TQ + NSA_WINDOW
    w0 = jnp.maximum(start - NSA_WINDOW, 0)
    rows = pl.ds(pl.multiple_of(w0, TQ), span)
    s = lax.dot_general(qs, kw_ref[rows, :], _NT, preferred_element_type=F32)
    wpos = w0 + lax.broadcasted_iota(jnp.int32, s.shape, 1)
    s = jnp.where(jnp.logical_and(wpos <= qpos, wpos > qpos - NSA_WINDOW), s, NEG_INF)
    p = jnp.exp2(s - jnp.max(s, axis=1, keepdims=True))
    o_win = jnp.dot(p.astype(BF16), vw_ref[rows, :], preferred_element_type=F32) / jnp.sum(p, axis=1, keepdims=True)

    g = g_ref[...]
    for h in range(N_HEADS):
        hs = slice(h * TQ, (h + 1) * TQ)
        cs = slice(h * HEAD_DIM, (h + 1) * HEAD_DIM)
        o_ref[:, cs] = (g[:, 3 * h:3 * h + 1] * oc_ref[:, cs] + g[:, 3 * h + 1:3 * h + 2] * o_sel[hs]
                        + g[:, 3 * h + 2:3 * h + 3] * o_win[hs]).astype(BF16)


def _nsa_sel(nq_rope, nkv, sel_bias, tags, o_cmp, gates):
    L = nq_rope.shape[0]
    TQ = Q_BLOCK
    lanes = sel_bias.shape[1]
    R = N_HEADS * TQ
    col = lambda j: _resident((L, HEAD_DIM), lambda i, j=j: (0, j))
    return pl.pallas_call(
        _nsa_sel_kernel,
        grid=(L // TQ,),
        in_specs=[pl.BlockSpec((TQ, GROUP_WIDTH), lambda i: (i, 0)),
                  col(0), col(1), col(2), col(3),
                  pl.BlockSpec((TQ, lanes), lambda i: (i, 0)),
                  _resident(tags.shape, lambda i: (0, 0, 0)),
                  pl.BlockSpec((TQ, GROUP_WIDTH), lambda i: (i, 0)),
                  pl.BlockSpec((TQ, HEAD_DIM), lambda i: (i, 0))],
        out_specs=pl.BlockSpec((TQ, GROUP_WIDTH), lambda i: (i, 0)),
        out_shape=jax.ShapeDtypeStruct((L, GROUP_WIDTH), BF16),
        scratch_shapes=[pltpu.VMEM((lanes // HEAD_DIM, R, 2 * HEAD_DIM), BF16), pltpu.VMEM((2, R // 2, HEAD_DIM), F32),
                        pltpu.VMEM((2, R // 2, 2 * HEAD_DIM), F32)],
        compiler_params=_cparams(("parallel",)),
        name="nsa_sel_win",
    )(nq_rope, nkv, nkv, nkv, nkv, sel_bias, tags, o_cmp, gates)


def _matmul_kernel(a_ref, b_ref, o_ref):
    o_ref[...] = jnp.dot(a_ref[...], b_ref[...], preferred_element_type=F32).astype(o_ref.dtype)


def _mem_proj(mem_b, w):
    n, width = mem_b.shape[0], w.shape[1]
    return pl.pallas_call(
        _matmul_kernel,
        grid=(width // 512,),
        in_specs=[pl.BlockSpec((n, D_MODEL), lambda j: (0, 0)), pl.BlockSpec((D_MODEL, 512), lambda j: (0, j))],
        out_specs=pl.BlockSpec((n, 512), lambda j: (0, j)),
        out_shape=jax.ShapeDtypeStruct((n, width), BF16),
        compiler_params=_cparams(("parallel",)),
        name="mem_proj",
    )(mem_b, w)


def _xattn_kernel(q_ref, k_ref, v_ref, o_ref):
    for h in range(N_HEADS):
        cs = slice(h * HEAD_DIM, (h + 1) * HEAD_DIM)
        s = lax.dot_general(q_ref[:, cs], k_ref[:, cs], _NT, preferred_element_type=F32)
        p = jnp.exp2(s - jnp.max(s, axis=1, keepdims=True))
        o = jnp.dot(p.astype(BF16), v_ref[:, cs], preferred_element_type=F32)
        o_ref[:, cs] = (o / jnp.sum(p, axis=1, keepdims=True)).astype(BF16)


def _xattn(xq, mem_kv):
    L = xq.shape[0]
    n = mem_kv.shape[0]
    row = pl.BlockSpec((ROW_TILE, GROUP_WIDTH), lambda i: (i, 0))
    return pl.pallas_call(
        _xattn_kernel,
        grid=(L // ROW_TILE,),
        in_specs=[row, pl.BlockSpec((n, GROUP_WIDTH), lambda i: (0, 0)),
                  pl.BlockSpec((n, GROUP_WIDTH), lambda i: (0, 1))],
        out_specs=row,
        out_shape=jax.ShapeDtypeStruct((L, GROUP_WIDTH), BF16),
        compiler_params=_cparams(("parallel",)),
        name="xattn",
    )(xq, mem_kv, mem_kv)


def _out_proj_kernel(y0_ref, y1_ref, y2_ref, y3_ref, w_ref, x_ref, g_ref, b_ref, o_ref, ob_ref):
    acc = ALPHA * x_ref[...]
    for j, y_ref in enumerate((y0_ref, y1_ref, y2_ref, y3_ref)):
        acc = acc + jnp.dot(y_ref[...], w_ref[j * GROUP_WIDTH:(j + 1) * GROUP_WIDTH, :], preferred_element_type=F32)
    y = _layer_norm(acc, g_ref[...], b_ref[...])
    o_ref[...] = y
    ob_ref[...] = y.astype(BF16)


def _out_proj(ys, w_o, x, g, b):
    L = x.shape[0]
    tm = ROW_TILE
    part = pl.BlockSpec((tm, GROUP_WIDTH), lambda i: (i, 0))
    row = pl.BlockSpec((tm, D_MODEL), lambda i: (i, 0))
    vec = pl.BlockSpec((1, D_MODEL), lambda i: (0, 0))
    return pl.pallas_call(
        _out_proj_kernel,
        grid=(L // tm,),
        in_specs=[part, part, part, part, _resident((D_MODEL, D_MODEL), lambda i: (0, 0)), row, vec, vec],
        out_specs=[row, row],
        out_shape=[jax.ShapeDtypeStruct((L, D_MODEL), F32), jax.ShapeDtypeStruct((L, D_MODEL), BF16)],
        compiler_params=_cparams(("parallel",)),
        name="out_proj_ln",
    )(*ys, w_o, x, g.reshape(1, -1), b.reshape(1, -1))


def _swiglu_step(x_ref, wg_ref, wu_ref, wd_ref, acc_ref):
    f = pl.program_id(1)
    x = x_ref[...]
    gate = jnp.dot(x, wg_ref[0].astype(BF16), preferred_element_type=F32)
    up = jnp.dot(x, wu_ref[0].astype(BF16), preferred_element_type=F32)
    hid = (gate * _sigmoid(gate) * up).astype(BF16)
    part = jnp.dot(hid, wd_ref[0].astype(BF16), preferred_element_type=F32)

    @pl.when(f == 0)
    def _():
        acc_ref[...] = part

    @pl.when(f > 0)
    def _():
        acc_ref[...] += part


def _ffn_dense_kernel(x_ref, wg_ref, wu_ref, wd_ref, xr_ref, g_ref, b_ref, o_ref, ob_ref, acc_ref):
    _swiglu_step(x_ref, wg_ref, wu_ref, wd_ref, acc_ref)

    @pl.when(pl.program_id(1) == pl.num_programs(1) - 1)
    def _():
        y = _layer_norm(ALPHA * xr_ref[...] + acc_ref[...], g_ref[...], b_ref[...])
        o_ref[...] = y
        ob_ref[...] = y.astype(BF16)


def _ffn_dense(xb, x, wg, wu, wd, g, b):
    L = x.shape[0]
    tm, tf = ROW_TILE, FF_TILE
    row = pl.BlockSpec((tm, D_MODEL), lambda i, f: (i, 0))
    vec = pl.BlockSpec((1, D_MODEL), lambda i, f: (0, 0))
    return pl.pallas_call(
        _ffn_dense_kernel,
        grid=(L // tm, D_FF // tf),
        in_specs=[row,
                  pl.BlockSpec((1, D_MODEL, tf), lambda i, f: (0, 0, f)),
                  pl.BlockSpec((1, D_MODEL, tf), lambda i, f: (0, 0, f)),
                  pl.BlockSpec((1, tf, D_MODEL), lambda i, f: (0, f, 0)),
                  row, vec, vec],
        out_specs=[row, row],
        out_shape=[jax.ShapeDtypeStruct((L, D_MODEL), F32), jax.ShapeDtypeStruct((L, D_MODEL), BF16)],
        scratch_shapes=[pltpu.VMEM((tm, D_MODEL), F32)],
        compiler_params=_cparams(("parallel", "arbitrary")),
        name="ffn_dense_ln",
    )(xb, wg[None], wu[None], wd[None], x, g.reshape(1, -1), b.reshape(1, -1))


def _ffn_moe_kernel(exp_ref, used_ref, x_ref, wg_ref, wu_ref, wd_ref, o_ref, acc_ref):
    live = pl.program_id(0) < used_ref[0]

    @pl.when(live)
    def _():
        _swiglu_step(x_ref, wg_ref, wu_ref, wd_ref, acc_ref)

    @pl.when(pl.program_id(1) == pl.num_programs(1) - 1)
    def _():
        o_ref[...] = jnp.where(live, acc_ref[...], 0.0)


def _ffn_moe(xg, blk_exp, n_used, wg, wu, wd, layer):
    P = xg.shape[0]
    tm, tf = MOE_ROWS, MOE_FF_TILE
    nf = D_FF // tf
    ff = lambda b, f, used: jnp.where(b < used[0], f, nf - 1)
    grid_spec = pltpu.PrefetchScalarGridSpec(
        num_scalar_prefetch=2,
        grid=(P // tm, nf),
        in_specs=[pl.BlockSpec((tm, D_MODEL), lambda b, f, e, u: (b, 0)),
                  pl.BlockSpec((None, 1, D_MODEL, tf), lambda b, f, e, u: (layer, e[b], 0, ff(b, f, u))),
                  pl.BlockSpec((None, 1, D_MODEL, tf), lambda b, f, e, u: (layer, e[b], 0, ff(b, f, u))),
                  pl.BlockSpec((None, 1, tf, D_MODEL), lambda b, f, e, u: (layer, e[b], ff(b, f, u), 0))],
        out_specs=pl.BlockSpec((tm, D_MODEL), lambda b, f, e, u: (b, 0)),
        scratch_shapes=[pltpu.VMEM((tm, D_MODEL), F32)],
    )
    return pl.pallas_call(
        _ffn_moe_kernel,
        grid_spec=grid_spec,
        out_shape=jax.ShapeDtypeStruct((P, D_MODEL), F32),
        compiler_params=_cparams(("parallel", "arbitrary"), 56),
        name="ffn_moe",
    )(blk_exp, n_used, xg, wg, wu, wd)


def _router_kernel(x_ref, r_ref, o_ref):
    xs = _split3(x_ref[...])
    rs = (r_ref[0], r_ref[1], r_ref[2])
    logits = None
    for i, j in ((0, 0), (0, 1), (1, 0), (1, 1), (0, 2), (2, 0)):
        t = jnp.dot(xs[i], rs[j], preferred_element_type=F32)
        logits = t if logits is None else logits + t
    lane = lax.broadcasted_iota(jnp.int32, logits.shape, 1)
    lane_f = lane.astype(F32)
    logits = jnp.where(lane < N_EXPERTS, logits, NEG_INF)
    m1 = jnp.max(logits, axis=1, keepdims=True)
    i1 = jnp.min(jnp.where(logits == m1, lane_f, 128.0), axis=1, keepdims=True)
    rest = jnp.where(lane_f == i1, NEG_INF, logits)
    m2 = jnp.max(rest, axis=1, keepdims=True)
    i2 = jnp.min(jnp.where(rest == m2, lane_f, 128.0), axis=1, keepdims=True)
    e2 = jnp.exp(m2 - m1)
    g1 = 1.0 / (1.0 + e2)
    g2 = e2 / (1.0 + e2)
    o_ref[...] = jnp.where(lane == 0, i1, jnp.where(lane == 1, i2, jnp.where(lane == 2, g1, jnp.where(lane == 3, g2, 0.0))))


def _router(x, router3):
    L = x.shape[0]
    return pl.pallas_call(
        _router_kernel,
        grid=(L // ROW_TILE,),
        in_specs=[pl.BlockSpec((ROW_TILE, D_MODEL), lambda i: (i, 0)),
                  pl.BlockSpec((3, D_MODEL, 128), lambda i: (0, 0, 0))],
        out_specs=pl.BlockSpec((ROW_TILE, 128), lambda i: (i, 0)),
        out_shape=jax.ShapeDtypeStruct((L, 128), F32),
        compiler_params=_cparams(("parallel",)),
        name="moe_router",
    )(x, router3)


def _combine_kernel(x_ref, ya_ref, yb_ref, gw_ref, g_ref, b_ref, o_ref, ob_ref):
    gw = gw_ref[...]
    f = gw[:, 2:3] * ya_ref[...] + gw[:, 3:4] * yb_ref[...]
    y = _layer_norm(ALPHA * x_ref[...] + f, g_ref[...], b_ref[...])
    o_ref[...] = y
    ob_ref[...] = y.astype(BF16)


def _combine(x, ya, yb, route, g, b):
    L = x.shape[0]
    row = pl.BlockSpec((ROW_TILE, D_MODEL), lambda i: (i, 0))
    vec = pl.BlockSpec((1, D_MODEL), lambda i: (0, 0))
    return pl.pallas_call(
        _combine_kernel,
        grid=(L // ROW_TILE,),
        in_specs=[row, row, row, pl.BlockSpec((ROW_TILE, 128), lambda i: (i, 0)), vec, vec],
        out_specs=[row, row],
        out_shape=[jax.ShapeDtypeStruct((L, D_MODEL), F32), jax.ShapeDtypeStruct((L, D_MODEL), BF16)],
        compiler_params=_cparams(("parallel",)),
        name="moe_combine_ln",
    )(x, ya, yb, route, g.reshape(1, -1), b.reshape(1, -1))


def _moe(x, xb, router, wg, wu, wd, layer, g, b):
    T = x.shape[0]
    r32 = jnp.pad(router, ((0, 0), (0, 128 - N_EXPERTS)))
    r1 = r32.astype(BF16)
    r2 = (r32 - r1.astype(F32)).astype(BF16)
    r3 = (r32 - r1.astype(F32) - r2.astype(F32)).astype(BF16)
    route = _router(x, jnp.stack([r1, r2, r3]))
    e_flat = route[:, :TOP_K].astype(jnp.int32).reshape(-1)
    onehot = (e_flat[:, None] == jnp.arange(N_EXPERTS, dtype=jnp.int32)[None, :]).astype(jnp.int32)
    csum = jnp.cumsum(onehot, axis=0)
    rank = jnp.sum(onehot * csum, axis=1) - 1
    counts = csum[-1]
    padded = (counts + MOE_ROWS - 1) // MOE_ROWS * MOE_ROWS
    pend = jnp.cumsum(padded)
    dest = (pend - padded)[e_flat] + rank
    n_blocks = -(-(T * TOP_K + N_EXPERTS * (MOE_ROWS - 1)) // MOE_ROWS)
    P = n_blocks * MOE_ROWS
    tok = jnp.repeat(jnp.arange(T, dtype=jnp.int32), TOP_K)
    buf_tok = jnp.zeros((P,), jnp.int32).at[dest].set(tok)
    n_used = (pend[-1] // MOE_ROWS).astype(jnp.int32).reshape(1)
    blk_start = jnp.arange(n_blocks, dtype=jnp.int32) * MOE_ROWS
    blk_exp = jnp.minimum(jnp.searchsorted(pend, jnp.minimum(blk_start, pend[-1] - 1), side='right'),
                          N_EXPERTS - 1).astype(jnp.int32)
    xg = jnp.take(xb, buf_tok, axis=0)
    yb = _ffn_moe(xg, blk_exp, n_used, wg, wu, wd, layer)
    d2 = dest.reshape(T, TOP_K)
    return _combine(x, jnp.take(yb, d2[:, 0], axis=0), jnp.take(yb, d2[:, 1], axis=0), route, g, b)


def _rope_tables(L):
    inv_freq = 1.0 / (ROPE_THETA ** (jnp.arange(0, ROPE_DIM, 2, dtype=F32) / ROPE_DIM))
    ang = jnp.arange(L, dtype=F32)[:, None] * inv_freq[None, :]
    cos, sin = jnp.cos(ang), jnp.sin(ang)
    z = lambda n: jnp.zeros((L, n), F32)
    cosf = jnp.concatenate([cos, cos, jnp.ones((L, HEAD_DIM - ROPE_DIM), F32)], axis=1)
    sa = jnp.concatenate([-sin, z(HEAD_DIM - 16)], axis=1)
    sb = jnp.concatenate([z(16), sin, z(HEAD_DIM - ROPE_DIM)], axis=1)
    return cosf, sa, sb


def _reorder_w_in(w):
    return jnp.concatenate([w[:, :3328], w[:, 3340:3852], w[:, 3328:3340],
                            jnp.zeros((D_MODEL, 128 - 12), w.dtype)], axis=1).astype(BF16)


def _trunk(x, mem, p):
    L = x.shape[0]
    ns, nc = L // NSA_SEL_BLOCK, L // NSA_CMP_STRIDE
    cosf, sa, sb = _rope_tables(L)
    lanes = -(-ns // HEAD_DIM) * HEAD_DIM
    cj = jnp.arange(nc)[:, None] - 4 * jnp.arange(lanes)[None, :]
    agg = jnp.logical_and(cj >= -1, cj <= 3).astype(BF16)
    ck = min(SEL_CHUNK, L)
    per = ck // NSA_SEL_BLOCK
    n_tag = min(HEAD_DIM // per, L // ck)
    blk_id = jnp.arange(n_tag)[:, None, None] * per + jnp.arange(ck)[None, :, None] // NSA_SEL_BLOCK
    tags = (blk_id == jnp.arange(HEAD_DIM)[None, None, :]).astype(BF16)
    mem_b = mem.astype(BF16)

    x, xb = _ln_in(x, p['ln_in_g'], p['ln_in_b'])
    for i in range(DEPTH):
        u, mq, mk, mv, nqc, nqr, cmp_kv, nkv, xq, gates = _proj(xb, _reorder_w_in(p['w_in'][i]), cosf, sa, sb)
        bmat, cmat, pw = _s5_params(p['ssm_a_re'][i], p['ssm_a_im'][i], p['ssm_log_dt'][i], p['ssm_b_re'][i],
                                    p['ssm_b_im'][i], p['ssm_c_re'][i], p['ssm_c_im'][i])
        y_ssm = _s5(u, bmat, cmat, pw, p['ssm_d'][i], p['ssm_w_glu'][i])
        y_moba = _moba(mq, mk, mv, _kmean(mk))
        pos = jnp.stack([p['nsa_pos_k'][i].reshape(1, -1), p['nsa_pos_v'][i].reshape(1, -1)])
        pos = jnp.broadcast_to(pos, (2, 8, pos.shape[-1])).astype(BF16)
        kvc = _compress(cmp_kv, jnp.stack([p['nsa_ck1'][i], p['nsa_cv1'][i]]).astype(BF16),
                        jnp.stack([p['nsa_ck2'][i], p['nsa_cv2'][i]]).astype(BF16), pos)
        o_cmp, sel_bias = _nsa_cmp(nqc, kvc, agg)
        y_nsa = _nsa_sel(nqr, nkv, sel_bias, tags, o_cmp, gates)
        mem_kv = _mem_proj(mem_b, jnp.concatenate([p['mem_wk'][i], p['mem_wv'][i]], axis=1).astype(BF16))
        y_x = _xattn(xq, mem_kv)
        x, xb = _out_proj((y_ssm, y_moba, y_nsa, y_x), p['w_o'][i].astype(BF16), x, p['ln1_g'][i], p['ln1_b'][i])
        j = i // 2
        if i % 2 == 0:
            x, xb = _ffn_dense(xb, x, p['ffn_w_gate'][j].astype(BF16), p['ffn_w_up'][j].astype(BF16),
                               p['ffn_w_down'][j].astype(BF16), p['ln2_g'][i], p['ln2_b'][i])
        else:
            x, xb = _moe(x, xb, p['moe_router'][j], p['moe_w_gate'], p['moe_w_up'], p['moe_w_down'], j,
                         p['ln2_g'][i], p['ln2_b'][i])
    return x


def kernel(x, mem, ln_in_g, ln_in_b, w_in, ssm_a_re, ssm_a_im, ssm_log_dt, ssm_b_re, ssm_b_im, ssm_c_re, ssm_c_im,
           ssm_d, ssm_w_glu, nsa_pos_k, nsa_pos_v, nsa_ck1, nsa_ck2, nsa_cv1, nsa_cv2, mem_wk, mem_wv, w_o, ln1_g,
           ln1_b, ln2_g, ln2_b, ffn_w_gate, ffn_w_up, ffn_w_down, moe_router, moe_w_gate, moe_w_up, moe_w_down):
    p = dict(ln_in_g=ln_in_g, ln_in_b=ln_in_b, w_in=w_in, ssm_a_re=ssm_a_re, ssm_a_im=ssm_a_im,
             ssm_log_dt=ssm_log_dt, ssm_b_re=ssm_b_re, ssm_b_im=ssm_b_im, ssm_c_re=ssm_c_re, ssm_c_im=ssm_c_im,
             ssm_d=ssm_d, ssm_w_glu=ssm_w_glu, nsa_pos_k=nsa_pos_k, nsa_pos_v=nsa_pos_v, nsa_ck1=nsa_ck1,
             nsa_ck2=nsa_ck2, nsa_cv1=nsa_cv1, nsa_cv2=nsa_cv2, mem_wk=mem_wk, mem_wv=mem_wv, w_o=w_o,
             ln1_g=ln1_g, ln1_b=ln1_b, ln2_g=ln2_g, ln2_b=ln2_b, ffn_w_gate=ffn_w_gate, ffn_w_up=ffn_w_up,
             ffn_w_down=ffn_w_down, moe_router=moe_router, moe_w_gate=moe_w_gate, moe_w_up=moe_w_up,
             moe_w_down=moe_w_down)
    assert x.shape[0] == 1 and mem.shape[0] == 1
    return _trunk(x[0], mem[0], p)[None]
```

```python
import functools
import math

import jax
import jax.numpy as jnp
from jax import lax
from jax.experimental import pallas as pl
from jax.experimental.pallas import tpu as pltpu

F32 = jnp.float32
BF16 = jnp.bfloat16

D_MODEL = 2048
DEPTH = 4
HEAD_DIM = 128
GROUP_WIDTH = 512
N_HEADS = 4
SSM_CH = 512
SSM_GROUP = 16
SSM_NG = 32
SSM_STATE = 64
SSM_NSTATE = SSM_NG * SSM_STATE
MOBA_BLOCK = 256
MOBA_TOPK = 3
NSA_CMP_LEN = 32
NSA_CMP_STRIDE = 16
NSA_SEL_BLOCK = 64
NSA_TOPK = 16
NSA_WINDOW = 512
ROPE_THETA = 500000.0
ROPE_DIM = 32
D_FF = 5632
N_EXPERTS = 8
TOP_K = 2
Q_BLOCK = 128
LN_EPS = 1e-5
NEG_INF = -1e30
FORCE_SCORE = 1e9
ALPHA = (2.0 * DEPTH) ** 0.25
SCALE = HEAD_DIM ** -0.5 * math.log2(math.e)

ROW_TILE = 512
SCAN_ROWS = 512
SCAN_LANES = 256
SEL_CHUNK = 512
SEL_STEP_CHUNKS = 4
MOBA_STEP_BLOCKS = 8
FF_TILE = 512
MOE_ROWS = 1024
MOE_FF_TILE = 256
COMBINE_ROWS = 256
PROJ_WIDTH = 3968

_NT = (((1,), (1,)), ((), ()))


def _cparams(sem, vmem_mb=48):
    return pltpu.CompilerParams(dimension_semantics=sem, vmem_limit_bytes=vmem_mb * 1024 * 1024)


def _resident(block_shape, index_map):
    return pl.BlockSpec(block_shape, index_map, pipeline_mode=pl.Buffered(1))


def _layer_norm(r, g, b):
    mu = jnp.mean(r, axis=-1, keepdims=True)
    d = r - mu
    var = jnp.mean(d * d, axis=-1, keepdims=True)
    return d * lax.rsqrt(var + LN_EPS) * g + b


def _gelu_tanh(x):
    return x * (0.5 * (1.0 + jnp.tanh(math.sqrt(2.0 / math.pi) * (x + 0.044715 * (x * x * x)))))


def _sigmoid(x):
    return 1.0 / (1.0 + jnp.exp(-x))


def _split3(a):
    a1 = a.astype(BF16)
    r1 = a - a1.astype(F32)
    a2 = r1.astype(BF16)
    a3 = (r1 - a2.astype(F32)).astype(BF16)
    return a1, a2, a3


def _topk_mask(score, ids, k, n):
    def body(_, carry):
        sc, sel = carry
        m = jnp.max(sc, axis=1, keepdims=True)
        idx = jnp.min(jnp.where(sc == m, ids, float(n)), axis=1, keepdims=True)
        hit = ids == idx
        sel = jnp.where(jnp.logical_and(hit, m > 0.5 * NEG_INF), 1.0, sel)
        sc = jnp.where(hit, NEG_INF, sc)
        return sc, sel

    _, sel = lax.fori_loop(0, k, body, (score, jnp.zeros(score.shape, F32)))
    return sel


def _ln_in_kernel(x_ref, g_ref, b_ref, o_ref, ob_ref):
    y = _layer_norm(x_ref[...], g_ref[...], b_ref[...])
    o_ref[...] = y
    ob_ref[...] = y.astype(BF16)


def _ln_in(x, g, b):
    L = x.shape[0]
    row = pl.BlockSpec((ROW_TILE, D_MODEL), lambda i: (i, 0))
    vec = pl.BlockSpec((1, D_MODEL), lambda i: (0, 0))
    return pl.pallas_call(
        _ln_in_kernel,
        grid=(L // ROW_TILE,),
        in_specs=[row, vec, vec],
        out_specs=[row, row],
        out_shape=[jax.ShapeDtypeStruct((L, D_MODEL), F32), jax.ShapeDtypeStruct((L, D_MODEL), BF16)],
        compiler_params=_cparams(("parallel",)),
        name="ln_in",
    )(x, g.reshape(1, -1), b.reshape(1, -1))


def _proj_kernel(x_ref, w_ref, cos_ref, sa_ref, sb_ref,
                 u_ref, mq_ref, mk_ref, mv_ref, nqc_ref, nqr_ref, cmp_ref, nkv_ref, xq_ref, g_ref):
    x = x_ref[...]
    cos, sa, sb = cos_ref[...], sa_ref[...], sb_ref[...]

    def dot(c0, n):
        return jnp.dot(x, w_ref[:, c0:c0 + n], preferred_element_type=F32)

    def rope(t):
        return t * cos + pltpu.roll(t, HEAD_DIM - 16, 1) * sa + pltpu.roll(t, 16, 1) * sb

    heads = [slice(h * HEAD_DIM, (h + 1) * HEAD_DIM) for h in range(N_HEADS)]
    u_ref[...] = dot(0, 512)
    t = dot(512, 512)
    for cs in heads:
        mq_ref[:, cs] = (rope(t[:, cs]) * SCALE).astype(BF16)
    t = dot(1024, 512)
    for cs in heads:
        mk_ref[:, cs] = rope(t[:, cs]).astype(BF16)
    mv_ref[...] = dot(1536, 512).astype(BF16)
    t = dot(2048, 512)
    nqc_ref[...] = (t * SCALE).astype(BF16)
    for cs in heads:
        nqr_ref[:, cs] = (rope(t[:, cs]) * SCALE).astype(BF16)
    t = dot(2560, 512)
    cmp_ref[0] = t[:, heads[0]].astype(BF16)
    cmp_ref[1] = t[:, heads[1]].astype(BF16)
    nkv_ref[:, heads[0]] = rope(t[:, heads[2]]).astype(BF16)
    nkv_ref[:, heads[1]] = t[:, heads[3]].astype(BF16)
    t = dot(3072, 256)
    nkv_ref[:, heads[2]] = rope(t[:, heads[0]]).astype(BF16)
    nkv_ref[:, heads[3]] = t[:, heads[1]].astype(BF16)
    t = dot(3328, 640)
    xq_ref[...] = (t[:, :512] * SCALE).astype(BF16)
    g_ref[...] = _sigmoid(t[:, 512:])


def _proj(xb, w, cosf, sa, sb):
    L = xb.shape[0]
    tm = ROW_TILE
    row = lambda n: pl.BlockSpec((tm, n), lambda i: (i, 0))
    outs = [
        (jax.ShapeDtypeStruct((L, 512), F32), row(512)),
        (jax.ShapeDtypeStruct((L, 512), BF16), row(512)),
        (jax.ShapeDtypeStruct((L, 512), BF16), row(512)),
        (jax.ShapeDtypeStruct((L, 512), BF16), row(512)),
        (jax.ShapeDtypeStruct((L, 512), BF16), row(512)),
        (jax.ShapeDtypeStruct((L, 512), BF16), row(512)),
        (jax.ShapeDtypeStruct((2, L, 128), BF16), pl.BlockSpec((2, tm, 128), lambda i: (0, i, 0))),
        (jax.ShapeDtypeStruct((L, 512), BF16), row(512)),
        (jax.ShapeDtypeStruct((L, 512), BF16), row(512)),
        (jax.ShapeDtypeStruct((L, 128), F32), row(128)),
    ]
    return pl.pallas_call(
        _proj_kernel,
        grid=(L // tm,),
        in_specs=[row(D_MODEL), _resident((D_MODEL, PROJ_WIDTH), lambda i: (0, 0)),
                  row(128), row(128), row(128)],
        out_specs=[o[1] for o in outs],
        out_shape=[o[0] for o in outs],
        compiler_params=_cparams(("parallel",), 56),
        name="in_proj",
    )(xb, w, cosf, sa, sb)


def _s5_kernel(u_ref, bmat_ref, cmat_ref, pw_ref, d_ref, wglu_ref, o_ref, hr_ref, hi_ref, car_ref, cai_ref):
    T = u_ref.shape[0]
    W = SCAN_LANES

    @pl.when(pl.program_id(0) == 0)
    def _():
        car_ref[...] = jnp.zeros(car_ref.shape, F32)
        cai_ref[...] = jnp.zeros(cai_ref.shape, F32)

    u = u_ref[...]
    ub = u.astype(BF16)
    for b in range(4):
        bu = jnp.dot(ub[:, b * 128:(b + 1) * 128], bmat_ref[b], preferred_element_type=F32)
        hr_ref[:, b * 512:(b + 1) * 512] = bu[:, :512]
        hi_ref[:, b * 512:(b + 1) * 512] = bu[:, 512:]

    for cb in range(SSM_NSTATE // W):
        sl = slice(cb * W, (cb + 1) * W)
        steps = [(pw_ref[2 * j, :, sl], pw_ref[2 * j + 1, :, sl], 1 << j) for j in range(3)]
        p_r, p_i = pw_ref[6, :, sl], pw_ref[7, :, sl]

        def tile(t, carry, sl=sl, steps=steps, p_r=p_r, p_i=p_i):
            c_r, c_i = carry
            rows = pl.ds(pl.multiple_of(t * 8, 8), 8)
            x_r, x_i = hr_ref[rows, sl], hi_ref[rows, sl]
            for l_r, l_i, d in steps:
                s_r, s_i = pltpu.roll(x_r, d, 0), pltpu.roll(x_i, d, 0)
                x_r, x_i = x_r + l_r * s_r - l_i * s_i, x_i + l_r * s_i + l_i * s_r
            x_r, x_i = x_r + p_r * c_r - p_i * c_i, x_i + p_r * c_i + p_i * c_r
            hr_ref[rows, sl] = x_r
            hi_ref[rows, sl] = x_i
            return (jnp.broadcast_to(x_r[7:8, :], (8, W)), jnp.broadcast_to(x_i[7:8, :], (8, W)))

        c_r, c_i = lax.fori_loop(0, T // 8, tile, (car_ref[:, sl], cai_ref[:, sl]))
        car_ref[:, sl] = c_r
        cai_ref[:, sl] = c_i

    ys = []
    for b in range(4):
        h_r = hr_ref[:, b * 512:(b + 1) * 512].astype(BF16)
        h_i = hi_ref[:, b * 512:(b + 1) * 512].astype(BF16)
        ys.append(jnp.dot(h_r, cmat_ref[b, :512, :], preferred_element_type=F32)
                  + jnp.dot(h_i, cmat_ref[b, 512:, :], preferred_element_type=F32))
    y = jnp.concatenate(ys, axis=1) + d_ref[...] * u
    z = _gelu_tanh(y)
    gate = _sigmoid(jnp.dot(z.astype(BF16), wglu_ref[...], preferred_element_type=F32))
    o_ref[...] = (z * gate).astype(BF16)


def _s5_params(a_re, a_im, log_dt, b_re, b_im, c_re, c_im):
    dt = jnp.exp(log_dt)[:, None]
    mag = jnp.exp(a_re * dt)
    lam_re, lam_im = mag * jnp.cos(a_im * dt), mag * jnp.sin(a_im * dt)
    den = a_re * a_re + a_im * a_im
    nr, ni = lam_re - 1.0, lam_im
    coef_re = (nr * a_re + ni * a_im) / den
    coef_im = (ni * a_re - nr * a_im) / den
    bbar_re = coef_re[..., None] * b_re - coef_im[..., None] * b_im
    bbar_im = coef_re[..., None] * b_im + coef_im[..., None] * b_re
    eye = jnp.eye(8, dtype=F32)
    to_b = lambda m: jnp.einsum('bgnc,gh->bgchn', m.reshape(4, 8, SSM_STATE, SSM_GROUP), eye).reshape(4, 128, 512)
    bmat = jnp.concatenate([to_b(bbar_re), to_b(bbar_im)], axis=-1).astype(BF16)
    to_c = lambda m: jnp.einsum('bgcn,gh->bgnhc', m.reshape(4, 8, SSM_GROUP, SSM_STATE), eye).reshape(4, 512, 128)
    cmat = jnp.concatenate([to_c(c_re), -to_c(c_im)], axis=1).astype(BF16)
    lr, li = lam_re.reshape(-1), lam_im.reshape(-1)
    pr, pi = [lr], [li]
    for _ in range(7):
        pr, pi = pr + [pr[-1] * lr - pi[-1] * li], pi + [pr[-1] * li + pi[-1] * lr]
    row = jnp.arange(8)[:, None]
    tabs = []
    for d in (1, 2, 4):
        tabs += [jnp.where(row >= d, pr[d - 1][None, :], 0.0), jnp.where(row >= d, pi[d - 1][None, :], 0.0)]
    tabs += [jnp.stack(pr, axis=0), jnp.stack(pi, axis=0)]
    return bmat, cmat, jnp.stack(tabs, axis=0)


def _s5(u, bmat, cmat, pw, d_skip, w_glu):
    L = u.shape[0]
    T = SCAN_ROWS
    row = pl.BlockSpec((T, 512), lambda i: (i, 0))
    full = lambda shape: pl.BlockSpec(shape, lambda i: (0,) * len(shape))
    return pl.pallas_call(
        _s5_kernel,
        grid=(L // T,),
        in_specs=[row, full((4, 128, 1024)), full((4, 1024, 128)), full((8, 8, SSM_NSTATE)),
                  full((1, 512)), full((512, 512))],
        out_specs=row,
        out_shape=jax.ShapeDtypeStruct((L, 512), BF16),
        scratch_shapes=[pltpu.VMEM((T, SSM_NSTATE), F32), pltpu.VMEM((T, SSM_NSTATE), F32),
                        pltpu.VMEM((8, SSM_NSTATE), F32), pltpu.VMEM((8, SSM_NSTATE), F32)],
        compiler_params=_cparams(("arbitrary",)),
        name="s5",
    )(u, bmat, cmat, pw, d_skip.reshape(1, -1), w_glu.astype(BF16))


def _kmean_kernel(k_ref, o_ref):
    rows = k_ref.shape[0]
    k = k_ref[...].astype(F32).reshape(rows // MOBA_BLOCK, MOBA_BLOCK, GROUP_WIDTH)
    o_ref[...] = jnp.mean(k, axis=1).astype(BF16)


def _kmean(k):
    L = k.shape[0]
    rows = 8 * MOBA_BLOCK if L % (8 * MOBA_BLOCK) == 0 else L
    return pl.pallas_call(
        _kmean_kernel,
        grid=(L // rows,),
        in_specs=[pl.BlockSpec((rows, GROUP_WIDTH), lambda i: (i, 0))],
        out_specs=pl.BlockSpec((rows // MOBA_BLOCK, GROUP_WIDTH), lambda i: (i, 0)),
        out_shape=jax.ShapeDtypeStruct((L // MOBA_BLOCK, GROUP_WIDTH), BF16),
        compiler_params=_cparams(("parallel",)),
        name="moba_kmean",
    )(k)


def _with_ones(v):
    return jnp.concatenate([v, jnp.ones(v.shape, v.dtype)], axis=1)


def _flash_step(s, v_ones, m_ref, acc_ref, idx, first):
    row_max = jnp.max(s, axis=1, keepdims=True)
    if first:
        m_new = jnp.broadcast_to(row_max, (s.shape[0], HEAD_DIM))
    else:
        m_old = m_ref[idx]
        m_new = jnp.maximum(m_old, row_max)
    p = jnp.concatenate([jnp.exp2(s[:, c:c + HEAD_DIM] - m_new) for c in range(0, s.shape[1], HEAD_DIM)], axis=1)
    pv = jnp.dot(p.astype(BF16), v_ones, preferred_element_type=F32)
    if first:
        acc_ref[idx] = pv
    else:
        a = jnp.exp2(m_old - m_new)
        acc_ref[idx] = jnp.concatenate([a, a], axis=1) * acc_ref[idx] + pv
    m_ref[idx] = m_new


def _flash_result(acc_ref, idx):
    acc = acc_ref[idx]
    return acc[:, :HEAD_DIM] / acc[:, HEAD_DIM:]


def _moba_kernel(q_ref, k_ref, v_ref, km_ref, o_ref, qa_ref, m_ref, acc_ref):
    TQ = q_ref.shape[0]
    own = pl.program_id(0)
    heads = [slice(h * HEAD_DIM, (h + 1) * HEAD_DIM) for h in range(N_HEADS)]

    gs = jnp.concatenate([lax.dot_general(q_ref[:, cs], km_ref[:, cs], _NT, preferred_element_type=F32)
                          for cs in heads], axis=0)
    blk = lax.broadcasted_iota(jnp.int32, gs.shape, 1)
    sel = _topk_mask(jnp.where(blk < own, gs, NEG_INF), blk.astype(F32), MOBA_TOPK, HEAD_DIM)
    bias = jnp.where(sel > 0.5, 0.0, NEG_INF).astype(BF16)
    for h, cs in enumerate(heads):
        qa_ref[h] = jnp.concatenate([q_ref[:, cs], bias[h * TQ:(h + 1) * TQ]], axis=1)

    own_rows = pl.ds(pl.multiple_of(own * MOBA_BLOCK, MOBA_BLOCK), MOBA_BLOCK)
    causal = (lax.broadcasted_iota(jnp.int32, (TQ, MOBA_BLOCK), 1)
              <= lax.broadcasted_iota(jnp.int32, (TQ, MOBA_BLOCK), 0))
    for h, cs in enumerate(heads):
        s = lax.dot_general(q_ref[:, cs], k_ref[own_rows, cs], _NT, preferred_element_type=F32)
        _flash_step(jnp.where(causal, s, NEG_INF), _with_ones(v_ref[own_rows, cs]), m_ref, acc_ref, h, True)

    def past(first_block, n_blocks):
        n = n_blocks * MOBA_BLOCK
        rows = pl.ds(pl.multiple_of(first_block * MOBA_BLOCK, MOBA_BLOCK), n)
        blk = first_block + lax.broadcasted_iota(jnp.int32, (n, HEAD_DIM), 0) // MOBA_BLOCK
        tag = jnp.where(lax.broadcasted_iota(jnp.int32, (n, HEAD_DIM), 1) == blk, 1.0, 0.0).astype(BF16)
        for h, cs in enumerate(heads):
            k_aug = jnp.concatenate([k_ref[rows, cs], tag], axis=1)
            s = lax.dot_general(qa_ref[h], k_aug, _NT, preferred_element_type=F32)
            _flash_step(s, _with_ones(v_ref[rows, cs]), m_ref, acc_ref, h, False)

    def group(t, carry):
        past(MOBA_STEP_BLOCKS * t, MOBA_STEP_BLOCKS)
        return carry

    lax.fori_loop(0, own // MOBA_STEP_BLOCKS, group, 0)
    size = MOBA_STEP_BLOCKS // 2
    while size >= 1:
        @pl.when((own & size) != 0)
        def _(size=size):
            past(own & ~(2 * size - 1), size)
        size //= 2

    for h, cs in enumerate(heads):
        o_ref[:, cs] = _flash_result(acc_ref, h).astype(BF16)


def _moba(q, k, v, kmean):
    L = q.shape[0]
    TQ = MOBA_BLOCK
    nb = L // MOBA_BLOCK
    assert nb <= HEAD_DIM
    km = jnp.pad(kmean, ((0, HEAD_DIM - nb), (0, 0)))
    tile = pl.BlockSpec((TQ, GROUP_WIDTH), lambda i: (i, 0))
    return pl.pallas_call(
        _moba_kernel,
        grid=(L // TQ,),
        in_specs=[tile, _resident((L, GROUP_WIDTH), lambda i: (0, 0)), _resident((L, GROUP_WIDTH), lambda i: (0, 0)),
                  _resident((HEAD_DIM, GROUP_WIDTH), lambda i: (0, 0))],
        out_specs=tile,
        out_shape=jax.ShapeDtypeStruct((L, GROUP_WIDTH), BF16),
        scratch_shapes=[pltpu.VMEM((N_HEADS, TQ, 2 * HEAD_DIM), BF16), pltpu.VMEM((N_HEADS, TQ, HEAD_DIM), F32),
                        pltpu.VMEM((N_HEADS, TQ, 2 * HEAD_DIM), F32)],
        compiler_params=_cparams(("parallel",)),
        name="moba",
    )(q, k, v, km)


def _compress_kernel(c_ref, w1_ref, w2_ref, pos_ref, o_ref):
    n = c_ref.shape[1]
    half = w1_ref.shape[1] // 2
    c = c_ref[0]
    a = jnp.dot(c, w1_ref[0, :half, :], preferred_element_type=F32)
    b = jnp.dot(c, w1_ref[0, half:, :], preferred_element_type=F32)
    bias = jnp.dot(pos_ref[0], w1_ref[0], preferred_element_type=F32)[0:1, :]
    hid = _gelu_tanh(a + pltpu.roll(b, n - 1, 0) + bias)
    out = jnp.dot(hid.astype(BF16), w2_ref[0], preferred_element_type=F32)
    last = lax.broadcasted_iota(jnp.int32, out.shape, 0) == n - 1
    o_ref[0] = jnp.where(last, 0.0, out).astype(BF16)


def _compress(cmp_kv, w1, w2, pos):
    L = cmp_kv.shape[1]
    n = L // NSA_CMP_STRIDE
    wide = NSA_CMP_STRIDE * HEAD_DIM
    chunks = cmp_kv.reshape(2, n, wide)
    hidden = w1.shape[-1]
    return pl.pallas_call(
        _compress_kernel,
        grid=(2,),
        in_specs=[pl.BlockSpec((1, n, wide), lambda j: (j, 0, 0)),
                  pl.BlockSpec((1, 2 * wide, hidden), lambda j: (j, 0, 0)),
                  pl.BlockSpec((1, hidden, HEAD_DIM), lambda j: (j, 0, 0)),
                  pl.BlockSpec((1, 8, 2 * wide), lambda j: (j, 0, 0))],
        out_specs=pl.BlockSpec((1, n, HEAD_DIM), lambda j: (j, 0, 0)),
        out_shape=jax.ShapeDtypeStruct((2, n, HEAD_DIM), BF16),
        compiler_params=_cparams(("parallel",)),
        name="nsa_compress",
    )(chunks, w1, w2, pos)


def _stack_heads(ref):
    return jnp.concatenate([ref[:, h * HEAD_DIM:(h + 1) * HEAD_DIM] for h in range(N_HEADS)], axis=0)


def _nsa_cmp_kernel(q_ref, kc_ref, vc_ref, agg_ref, o_ref, sel_ref):
    TQ = q_ref.shape[0]
    nc = kc_ref.shape[1]
    ns = nc * NSA_CMP_STRIDE // NSA_SEL_BLOCK
    start = pl.program_id(0) * TQ
    qs = _stack_heads(q_ref)
    s = lax.dot_general(qs, kc_ref[0], _NT, preferred_element_type=F32)
    cend = lax.broadcasted_iota(jnp.int32, s.shape, 1) * NSA_CMP_STRIDE + (NSA_CMP_LEN - 1)
    ok = cend <= start + (lax.broadcasted_iota(jnp.int32, s.shape, 0) & (TQ - 1))
    s = jnp.where(ok, s, NEG_INF)
    m = jnp.max(s, axis=1, keepdims=True)
    e = jnp.where(ok, jnp.exp2(s - m), 0.0)
    p = e / jnp.maximum(jnp.sum(e, axis=1, keepdims=True), 1e-30)
    oc = jnp.dot(p.astype(BF16), vc_ref[0], preferred_element_type=F32)
    for h in range(N_HEADS):
        o_ref[:, h * HEAD_DIM:(h + 1) * HEAD_DIM] = oc[h * TQ:(h + 1) * TQ, :]

    imp = p[0:TQ] + p[TQ:2 * TQ] + p[2 * TQ:3 * TQ] + p[3 * TQ:4 * TQ]
    agg = agg_ref[...]
    imp_sel = sum(jnp.dot(t, agg, preferred_element_type=F32) for t in _split3(imp))
    blk = lax.broadcasted_iota(jnp.int32, imp_sel.shape, 1)
    own = (start + lax.broadcasted_iota(jnp.int32, imp_sel.shape, 0)) // NSA_SEL_BLOCK
    score = jnp.where(blk == own, FORCE_SCORE, jnp.where(blk < own, imp_sel, NEG_INF))
    sel = _topk_mask(score, blk.astype(F32), min(NSA_TOPK, ns), imp_sel.shape[1])
    sel_ref[...] = jnp.where(sel > 0.5, 0.0, NEG_INF).astype(BF16)


def _nsa_cmp(nq, kvc, agg):
    L = nq.shape[0]
    TQ = 2 * Q_BLOCK
    nc = kvc.shape[1]
    lanes = agg.shape[1]
    return pl.pallas_call(
        _nsa_cmp_kernel,
        grid=(L // TQ,),
        in_specs=[pl.BlockSpec((TQ, GROUP_WIDTH), lambda i: (i, 0)),
                  pl.BlockSpec((1, nc, HEAD_DIM), lambda i: (0, 0, 0)),
                  pl.BlockSpec((1, nc, HEAD_DIM), lambda i: (1, 0, 0)),
                  pl.BlockSpec((nc, lanes), lambda i: (0, 0))],
        out_specs=[pl.BlockSpec((TQ, GROUP_WIDTH), lambda i: (i, 0)),
                   pl.BlockSpec((TQ, lanes), lambda i: (i, 0))],
        out_shape=[jax.ShapeDtypeStruct((L, GROUP_WIDTH), F32), jax.ShapeDtypeStruct((L, lanes), BF16)],
        compiler_params=_cparams(("parallel",)),
        name="nsa_cmp",
    )(nq, kvc, kvc, agg)


def _nsa_sel_kernel(q_ref, ks_ref, vs_ref, kw_ref, vw_ref, sb_ref, tag_ref, oc_ref, g_ref, o_ref,
                    qa_ref, m_ref, acc_ref):
    TQ = q_ref.shape[0]
    CK = tag_ref.shape[1]
    cpg = HEAD_DIM // (CK // NSA_SEL_BLOCK)
    R = N_HEADS * TQ
    halves = (slice(0, R // 2), slice(R // 2, R))
    start = pl.program_id(0) * TQ
    qs = _stack_heads(q_ref)
    qpos = start + (lax.broadcasted_iota(jnp.int32, (R, 1), 0) & (TQ - 1))

    bias = sb_ref[...]
    for g in range(sb_ref.shape[1] // HEAD_DIM):
        qa_ref[g] = jnp.concatenate(
            [qs, jnp.concatenate([bias[:, g * HEAD_DIM:(g + 1) * HEAD_DIM]] * N_HEADS, axis=0)], axis=1)

    def scores(c, n):
        rows = pl.ds(pl.multiple_of(c * CK, CK), n * CK)
        tags = tag_ref[pl.ds(c % cpg, n)].reshape(n * CK, HEAD_DIM)
        k_aug = jnp.concatenate([ks_ref[rows, :], tags], axis=1)
        return rows, [lax.dot_general(qa_ref[c // cpg, hs, :], k_aug, _NT, preferred_element_type=F32)
                      for hs in halves]

    c_last = start // CK
    rows, ss = scores(c_last, 1)
    kpos = c_last * CK + lax.broadcasted_iota(jnp.int32, ss[0].shape, 1)
    v_ones = _with_ones(vs_ref[rows, :])
    for idx, hs in enumerate(halves):
        _flash_step(jnp.where(kpos <= qpos[hs], ss[idx], NEG_INF), v_ones, m_ref, acc_ref, idx, True)

    def past(c, n):
        rows, ss = scores(c, n)
        v_ones = _with_ones(vs_ref[rows, :])
        for idx in range(2):
            _flash_step(ss[idx], v_ones, m_ref, acc_ref, idx, False)

    def group(t, carry):
        past(SEL_STEP_CHUNKS * t, SEL_STEP_CHUNKS)
        return carry

    lax.fori_loop(0, c_last // SEL_STEP_CHUNKS, group, 0)
    size = SEL_STEP_CHUNKS // 2
    while size >= 1:
        @pl.when((c_last & size) != 0)
        def _(size=size):
            past(c_last & ~(2 * size - 1), size)
        size //= 2
    o_sel = jnp.concatenate([_flash_result(acc_ref, 0), _flash_result(acc_ref, 1)], axis=0)

    span = TQ + NSA_WINDOW
    w0 = jnp.maximum(start - NSA_WINDOW, 0)
    rows = pl.ds(pl.multiple_of(w0, TQ), span)
    s = lax.dot_general(qs, kw_ref[rows, :], _NT, preferred_element_type=F32)
    wpos = w0 + lax.broadcasted_iota(jnp.int32, s.shape, 1)
    s = jnp.where(jnp.logical_and(wpos <= qpos, wpos > qpos - NSA_WINDOW), s, NEG_INF)
    p = jnp.exp2(s - jnp.max(s, axis=1, keepdims=True))
    o_win = jnp.dot(p.astype(BF16), vw_ref[rows, :], preferred_element_type=F32) / jnp.sum(p, axis=1, keepdims=True)

    g = g_ref[...]
    for h in range(N_HEADS):
        hs = slice(h * TQ, (h + 1) * TQ)
        cs = slice(h * HEAD_DIM, (h + 1) * HEAD_DIM)
        o_ref[:, cs] = (g[:, 3 * h:3 * h + 1] * oc_ref[:, cs] + g[:, 3 * h + 1:3 * h + 2] * o_sel[hs]
                        + g[:, 3 * h + 2:3 * h + 3] * o_win[hs]).astype(BF16)


def _nsa_sel(nq_rope, nkv, sel_bias, tags, o_cmp, gates):
    L = nq_rope.shape[0]
    TQ = Q_BLOCK
    lanes = sel_bias.shape[1]
    R = N_HEADS * TQ
    col = lambda j: _resident((L, HEAD_DIM), lambda i, j=j: (0, j))
    return pl.pallas_call(
        _nsa_sel_kernel,
        grid=(L // TQ,),
        in_specs=[pl.BlockSpec((TQ, GROUP_WIDTH), lambda i: (i, 0)),
                  col(0), col(1), col(2), col(3),
                  pl.BlockSpec((TQ, lanes), lambda i: (i, 0)),
                  _resident(tags.shape, lambda i: (0, 0, 0)),
                  pl.BlockSpec((TQ, GROUP_WIDTH), lambda i: (i, 0)),
                  pl.BlockSpec((TQ, HEAD_DIM), lambda i: (i, 0))],
        out_specs=pl.BlockSpec((TQ, GROUP_WIDTH), lambda i: (i, 0)),
        out_shape=jax.ShapeDtypeStruct((L, GROUP_WIDTH), BF16),
        scratch_shapes=[pltpu.VMEM((lanes // HEAD_DIM, R, 2 * HEAD_DIM), BF16), pltpu.VMEM((2, R // 2, HEAD_DIM), F32),
                        pltpu.VMEM((2, R // 2, 2 * HEAD_DIM), F32)],
        compiler_params=_cparams(("parallel",)),
        name="nsa_sel_win",
    )(nq_rope, nkv, nkv, nkv, nkv, sel_bias, tags, o_cmp, gates)


def _matmul_kernel(a_ref, b_ref, o_ref):
    o_ref[...] = jnp.dot(a_ref[...], b_ref[...], preferred_element_type=F32).astype(o_ref.dtype)


def _mem_proj(mem_b, w):
    n, width = mem_b.shape[0], w.shape[1]
    return pl.pallas_call(
        _matmul_kernel,
        grid=(width // 512,),
        in_specs=[pl.BlockSpec((n, D_MODEL), lambda j: (0, 0)), pl.BlockSpec((D_MODEL, 512), lambda j: (0, j))],
        out_specs=pl.BlockSpec((n, 512), lambda j: (0, j)),
        out_shape=jax.ShapeDtypeStruct((n, width), BF16),
        compiler_params=_cparams(("parallel",)),
        name="mem_proj",
    )(mem_b, w)


def _xattn_kernel(q_ref, k_ref, v_ref, o_ref):
    for h in range(N_HEADS):
        cs = slice(h * HEAD_DIM, (h + 1) * HEAD_DIM)
        s = lax.dot_general(q_ref[:, cs], k_ref[:, cs], _NT, preferred_element_type=F32)
        p = jnp.exp2(s - jnp.max(s, axis=1, keepdims=True))
        o = jnp.dot(p.astype(BF16), v_ref[:, cs], preferred_element_type=F32)
        o_ref[:, cs] = (o / jnp.sum(p, axis=1, keepdims=True)).astype(BF16)


def _xattn(xq, mem_kv):
    L = xq.shape[0]
    n = mem_kv.shape[0]
    row = pl.BlockSpec((ROW_TILE, GROUP_WIDTH), lambda i: (i, 0))
    return pl.pallas_call(
        _xattn_kernel,
        grid=(L // ROW_TILE,),
        in_specs=[row, pl.BlockSpec((n, GROUP_WIDTH), lambda i: (0, 0)),
                  pl.BlockSpec((n, GROUP_WIDTH), lambda i: (0, 1))],
        out_specs=row,
        out_shape=jax.ShapeDtypeStruct((L, GROUP_WIDTH), BF16),
        compiler_params=_cparams(("parallel",)),
        name="xattn",
    )(xq, mem_kv, mem_kv)


def _out_proj_kernel(y0_ref, y1_ref, y2_ref, y3_ref, w_ref, x_ref, g_ref, b_ref, o_ref, ob_ref):
    acc = ALPHA * x_ref[...]
    for j, y_ref in enumerate((y0_ref, y1_ref, y2_ref, y3_ref)):
        acc = acc + jnp.dot(y_ref[...], w_ref[j * GROUP_WIDTH:(j + 1) * GROUP_WIDTH, :], preferred_element_type=F32)
    y = _layer_norm(acc, g_ref[...], b_ref[...])
    o_ref[...] = y
    ob_ref[...] = y.astype(BF16)


def _out_proj(ys, w_o, x, g, b):
    L = x.shape[0]
    tm = ROW_TILE
    part = pl.BlockSpec((tm, GROUP_WIDTH), lambda i: (i, 0))
    row = pl.BlockSpec((tm, D_MODEL), lambda i: (i, 0))
    vec = pl.BlockSpec((1, D_MODEL), lambda i: (0, 0))
    return pl.pallas_call(
        _out_proj_kernel,
        grid=(L // tm,),
        in_specs=[part, part, part, part, _resident((D_MODEL, D_MODEL), lambda i: (0, 0)), row, vec, vec],
        out_specs=[row, row],
        out_shape=[jax.ShapeDtypeStruct((L, D_MODEL), F32), jax.ShapeDtypeStruct((L, D_MODEL), BF16)],
        compiler_params=_cparams(("parallel",)),
        name="out_proj_ln",
    )(*ys, w_o, x, g.reshape(1, -1), b.reshape(1, -1))


def _swiglu_step(x_ref, wg_ref, wu_ref, wd_ref, acc_ref):
    f = pl.program_id(1)
    x = x_ref[...]
    gate = jnp.dot(x, wg_ref[0].astype(BF16), preferred_element_type=F32)
    up = jnp.dot(x, wu_ref[0].astype(BF16), preferred_element_type=F32)
    hid = (gate * _sigmoid(gate) * up).astype(BF16)
    part = jnp.dot(hid, wd_ref[0].astype(BF16), preferred_element_type=F32)

    @pl.when(f == 0)
    def _():
        acc_ref[...] = part

    @pl.when(f > 0)
    def _():
        acc_ref[...] += part


def _ffn_dense_kernel(x_ref, wg_ref, wu_ref, wd_ref, xr_ref, g_ref, b_ref, o_ref, ob_ref, acc_ref):
    _swiglu_step(x_ref, wg_ref, wu_ref, wd_ref, acc_ref)

    @pl.when(pl.program_id(1) == pl.num_programs(1) - 1)
    def _():
        y = _layer_norm(ALPHA * xr_ref[...] + acc_ref[...], g_ref[...], b_ref[...])
        o_ref[...] = y
        ob_ref[...] = y.astype(BF16)


def _ffn_dense(xb, x, wg, wu, wd, g, b):
    L = x.shape[0]
    tm, tf = ROW_TILE, FF_TILE
    row = pl.BlockSpec((tm, D_MODEL), lambda i, f: (i, 0))
    vec = pl.BlockSpec((1, D_MODEL), lambda i, f: (0, 0))
    return pl.pallas_call(
        _ffn_dense_kernel,
        grid=(L // tm, D_FF // tf),
        in_specs=[row,
                  pl.BlockSpec((1, D_MODEL, tf), lambda i, f: (0, 0, f)),
                  pl.BlockSpec((1, D_MODEL, tf), lambda i, f: (0, 0, f)),
                  pl.BlockSpec((1, tf, D_MODEL), lambda i, f: (0, f, 0)),
                  row, vec, vec],
        out_specs=[row, row],
        out_shape=[jax.ShapeDtypeStruct((L, D_MODEL), F32), jax.ShapeDtypeStruct((L, D_MODEL), BF16)],
        scratch_shapes=[pltpu.VMEM((tm, D_MODEL), F32)],
        compiler_params=_cparams(("parallel", "arbitrary")),
        name="ffn_dense_ln",
    )(xb, wg[None], wu[None], wd[None], x, g.reshape(1, -1), b.reshape(1, -1))


def _start_row_gather(idx_ref, base, n, src_hbm, dst_ref, sem):
    def body(r, carry):
        pltpu.make_async_copy(src_hbm.at[pl.ds(idx_ref[base + r], 1)], dst_ref.at[pl.ds(r, 1)], sem).start()
        return carry

    lax.fori_loop(0, n, body, 0)


def _wait_row_gather(n, src_hbm, dst_ref, sem):
    pltpu.make_async_copy(src_hbm.at[pl.ds(0, n)], dst_ref.at[pl.ds(0, n)], sem).wait()


def _ffn_moe_kernel(exp_ref, used_ref, tok_ref, x_hbm, wg_ref, wu_ref, wd_ref, o_ref, stage_ref, xb_ref, sem):
    b, f = pl.program_id(0), pl.program_id(1)
    tm = stage_ref.shape[0]
    live = b < used_ref[0]

    @pl.when(jnp.logical_and(live, f == 0))
    def _():
        @pl.when(b == 0)
        def _():
            _start_row_gather(tok_ref, 0, tm, x_hbm, stage_ref, sem)

        _wait_row_gather(tm, x_hbm, stage_ref, sem)
        xb_ref[...] = stage_ref[...].astype(BF16)

        @pl.when(b + 1 < used_ref[0])
        def _():
            _start_row_gather(tok_ref, (b + 1) * tm, tm, x_hbm, stage_ref, sem)

    @pl.when(live)
    def _():
        _swiglu_step(xb_ref, wg_ref, wu_ref, wd_ref, o_ref)

    @pl.when(jnp.logical_and(jnp.logical_not(live), f == 0))
    def _():
        o_ref[...] = jnp.zeros(o_ref.shape, F32)


def _ffn_moe(x, buf_tok, blk_exp, n_used, wg, wu, wd, layer):
    P = buf_tok.shape[0]
    tm, tf = MOE_ROWS, MOE_FF_TILE
    nf = D_FF // tf
    ff = lambda b, f, used: jnp.where(b < used[0], f, nf - 1)
    grid_spec = pltpu.PrefetchScalarGridSpec(
        num_scalar_prefetch=3,
        grid=(P // tm, nf),
        in_specs=[pl.BlockSpec(memory_space=pl.ANY),
                  pl.BlockSpec((None, 1, D_MODEL, tf), lambda b, f, e, u, t: (layer, e[b], 0, ff(b, f, u))),
                  pl.BlockSpec((None, 1, D_MODEL, tf), lambda b, f, e, u, t: (layer, e[b], 0, ff(b, f, u))),
                  pl.BlockSpec((None, 1, tf, D_MODEL), lambda b, f, e, u, t: (layer, e[b], ff(b, f, u), 0))],
        out_specs=pl.BlockSpec((tm, D_MODEL), lambda b, f, e, u, t: (b, 0)),
        scratch_shapes=[pltpu.VMEM((tm, D_MODEL), F32), pltpu.VMEM((tm, D_MODEL), BF16),
                        pltpu.SemaphoreType.DMA(())],
    )
    return pl.pallas_call(
        _ffn_moe_kernel,
        grid_spec=grid_spec,
        out_shape=jax.ShapeDtypeStruct((P, D_MODEL), F32),
        compiler_params=_cparams(("arbitrary", "arbitrary"), 56),
        name="ffn_moe",
    )(blk_exp, n_used, buf_tok, x, wg, wu, wd)


def _router_kernel(x_ref, r_ref, o_ref):
    xs = _split3(x_ref[...])
    rs = (r_ref[0], r_ref[1], r_ref[2])
    logits = None
    for i, j in ((0, 0), (0, 1), (1, 0), (1, 1), (0, 2), (2, 0)):
        t = jnp.dot(xs[i], rs[j], preferred_element_type=F32)
        logits = t if logits is None else logits + t
    lane = lax.broadcasted_iota(jnp.int32, logits.shape, 1)
    lane_f = lane.astype(F32)
    logits = jnp.where(lane < N_EXPERTS, logits, NEG_INF)
    m1 = jnp.max(logits, axis=1, keepdims=True)
    i1 = jnp.min(jnp.where(logits == m1, lane_f, 128.0), axis=1, keepdims=True)
    rest = jnp.where(lane_f == i1, NEG_INF, logits)
    m2 = jnp.max(rest, axis=1, keepdims=True)
    i2 = jnp.min(jnp.where(rest == m2, lane_f, 128.0), axis=1, keepdims=True)
    e2 = jnp.exp(m2 - m1)
    g1 = 1.0 / (1.0 + e2)
    g2 = e2 / (1.0 + e2)
    o_ref[...] = jnp.where(lane == 0, i1, jnp.where(lane == 1, i2, jnp.where(lane == 2, g1, jnp.where(lane == 3, g2, 0.0))))


def _router(x, router3):
    L = x.shape[0]
    return pl.pallas_call(
        _router_kernel,
        grid=(L // ROW_TILE,),
        in_specs=[pl.BlockSpec((ROW_TILE, D_MODEL), lambda i: (i, 0)),
                  pl.BlockSpec((3, D_MODEL, 128), lambda i: (0, 0, 0))],
        out_specs=pl.BlockSpec((ROW_TILE, 128), lambda i: (i, 0)),
        out_shape=jax.ShapeDtypeStruct((L, 128), F32),
        compiler_params=_cparams(("parallel",)),
        name="moe_router",
    )(x, router3)


def _combine_kernel(da_ref, db_ref, x_ref, gw_ref, g_ref, b_ref, y_hbm, o_ref, ob_ref, ya_ref, yb_ref, sem):
    i, n = pl.program_id(0), pl.num_programs(0)
    tm = x_ref.shape[0]
    slot = i % 2

    def start(step, slot):
        _start_row_gather(da_ref, step * tm, tm, y_hbm, ya_ref.at[slot], sem.at[slot])
        _start_row_gather(db_ref, step * tm, tm, y_hbm, yb_ref.at[slot], sem.at[slot])

    @pl.when(i == 0)
    def _():
        start(0, 0)

    @pl.when(i + 1 < n)
    def _():
        start(i + 1, 1 - slot)

    _wait_row_gather(tm, y_hbm, ya_ref.at[slot], sem.at[slot])
    _wait_row_gather(tm, y_hbm, yb_ref.at[slot], sem.at[slot])
    gw = gw_ref[...]
    f = gw[:, 2:3] * ya_ref[slot] + gw[:, 3:4] * yb_ref[slot]
    y = _layer_norm(ALPHA * x_ref[...] + f, g_ref[...], b_ref[...])
    o_ref[...] = y
    ob_ref[...] = y.astype(BF16)


def _combine(x, y_rows, dest_a, dest_b, route, g, b):
    L = x.shape[0]
    tm = COMBINE_ROWS
    row = pl.BlockSpec((tm, D_MODEL), lambda i, a, b: (i, 0))
    vec = pl.BlockSpec((1, D_MODEL), lambda i, a, b: (0, 0))
    grid_spec = pltpu.PrefetchScalarGridSpec(
        num_scalar_prefetch=2,
        grid=(L // tm,),
        in_specs=[row, pl.BlockSpec((tm, 128), lambda i, a, b: (i, 0)), vec, vec, pl.BlockSpec(memory_space=pl.ANY)],
        out_specs=[row, row],
        scratch_shapes=[pltpu.VMEM((2, tm, D_MODEL), F32), pltpu.VMEM((2, tm, D_MODEL), F32),
                        pltpu.SemaphoreType.DMA((2,))],
    )
    return pl.pallas_call(
        _combine_kernel,
        grid_spec=grid_spec,
        out_shape=[jax.ShapeDtypeStruct((L, D_MODEL), F32), jax.ShapeDtypeStruct((L, D_MODEL), BF16)],
        compiler_params=_cparams(("arbitrary",)),
        name="moe_combine_ln",
    )(dest_a, dest_b, x, route, g.reshape(1, -1), b.reshape(1, -1), y_rows)


def _moe(x, router, wg, wu, wd, layer, g, b):
    T = x.shape[0]
    r32 = jnp.pad(router, ((0, 0), (0, 128 - N_EXPERTS)))
    r1 = r32.astype(BF16)
    r2 = (r32 - r1.astype(F32)).astype(BF16)
    r3 = (r32 - r1.astype(F32) - r2.astype(F32)).astype(BF16)
    route = _router(x, jnp.stack([r1, r2, r3]))
    e_flat = route[:, :TOP_K].astype(jnp.int32).reshape(-1)
    onehot = (e_flat[:, None] == jnp.arange(N_EXPERTS, dtype=jnp.int32)[None, :]).astype(jnp.int32)
    csum = jnp.cumsum(onehot, axis=0)
    rank = jnp.sum(onehot * csum, axis=1) - 1
    counts = csum[-1]
    padded = (counts + MOE_ROWS - 1) // MOE_ROWS * MOE_ROWS
    pend = jnp.cumsum(padded)
    dest = (pend - padded)[e_flat] + rank
    n_blocks = -(-(T * TOP_K + N_EXPERTS * (MOE_ROWS - 1)) // MOE_ROWS)
    P = n_blocks * MOE_ROWS
    tok = jnp.repeat(jnp.arange(T, dtype=jnp.int32), TOP_K)
    buf_tok = jnp.zeros((P,), jnp.int32).at[dest].set(tok)
    n_used = (pend[-1] // MOE_ROWS).astype(jnp.int32).reshape(1)
    blk_start = jnp.arange(n_blocks, dtype=jnp.int32) * MOE_ROWS
    blk_exp = jnp.minimum(jnp.searchsorted(pend, jnp.minimum(blk_start, pend[-1] - 1), side='right'),
                          N_EXPERTS - 1).astype(jnp.int32)
    y_rows = _ffn_moe(x, buf_tok, blk_exp, n_used, wg, wu, wd, layer)
    d2 = dest.reshape(T, TOP_K)
    return _combine(x, y_rows, d2[:, 0], d2[:, 1], route, g, b)


def _rope_tables(L):
    inv_freq = 1.0 / (ROPE_THETA ** (jnp.arange(0, ROPE_DIM, 2, dtype=F32) / ROPE_DIM))
    ang = jnp.arange(L, dtype=F32)[:, None] * inv_freq[None, :]
    cos, sin = jnp.cos(ang), jnp.sin(ang)
    z = lambda n: jnp.zeros((L, n), F32)
    cosf = jnp.concatenate([cos, cos, jnp.ones((L, HEAD_DIM - ROPE_DIM), F32)], axis=1)
    sa = jnp.concatenate([-sin, z(HEAD_DIM - 16)], axis=1)
    sb = jnp.concatenate([z(16), sin, z(HEAD_DIM - ROPE_DIM)], axis=1)
    return cosf, sa, sb


def _reorder_w_in(w):
    return jnp.concatenate([w[:, :3328], w[:, 3340:3852], w[:, 3328:3340],
                            jnp.zeros((D_MODEL, 128 - 12), w.dtype)], axis=1).astype(BF16)


def _trunk(x, mem, p):
    L = x.shape[0]
    ns, nc = L // NSA_SEL_BLOCK, L // NSA_CMP_STRIDE
    cosf, sa, sb = _rope_tables(L)
    lanes = -(-ns // HEAD_DIM) * HEAD_DIM
    cj = jnp.arange(nc)[:, None] - 4 * jnp.arange(lanes)[None, :]
    agg = jnp.logical_and(cj >= -1, cj <= 3).astype(BF16)
    ck = min(SEL_CHUNK, L)
    per = ck // NSA_SEL_BLOCK
    n_tag = min(HEAD_DIM // per, L // ck)
    blk_id = jnp.arange(n_tag)[:, None, None] * per + jnp.arange(ck)[None, :, None] // NSA_SEL_BLOCK
    tags = (blk_id == jnp.arange(HEAD_DIM)[None, None, :]).astype(BF16)
    mem_b = mem.astype(BF16)

    x, xb = _ln_in(x, p['ln_in_g'], p['ln_in_b'])
    for i in range(DEPTH):
        u, mq, mk, mv, nqc, nqr, cmp_kv, nkv, xq, gates = _proj(xb, _reorder_w_in(p['w_in'][i]), cosf, sa, sb)
        bmat, cmat, pw = _s5_params(p['ssm_a_re'][i], p['ssm_a_im'][i], p['ssm_log_dt'][i], p['ssm_b_re'][i],
                                    p['ssm_b_im'][i], p['ssm_c_re'][i], p['ssm_c_im'][i])
        y_ssm = _s5(u, bmat, cmat, pw, p['ssm_d'][i], p['ssm_w_glu'][i])
        y_moba = _moba(mq, mk, mv, _kmean(mk))
        pos = jnp.stack([p['nsa_pos_k'][i].reshape(1, -1), p['nsa_pos_v'][i].reshape(1, -1)])
        pos = jnp.broadcast_to(pos, (2, 8, pos.shape[-1])).astype(BF16)
        kvc = _compress(cmp_kv, jnp.stack([p['nsa_ck1'][i], p['nsa_cv1'][i]]).astype(BF16),
                        jnp.stack([p['nsa_ck2'][i], p['nsa_cv2'][i]]).astype(BF16), pos)
        o_cmp, sel_bias = _nsa_cmp(nqc, kvc, agg)
        y_nsa = _nsa_sel(nqr, nkv, sel_bias, tags, o_cmp, gates)
        mem_kv = _mem_proj(mem_b, jnp.concatenate([p['mem_wk'][i], p['mem_wv'][i]], axis=1).astype(BF16))
        y_x = _xattn(xq, mem_kv)
        x, xb = _out_proj((y_ssm, y_moba, y_nsa, y_x), p['w_o'][i].astype(BF16), x, p['ln1_g'][i], p['ln1_b'][i])
        j = i // 2
        if i % 2 == 0:
            x, xb = _ffn_dense(xb, x, p['ffn_w_gate'][j].astype(BF16), p['ffn_w_up'][j].astype(BF16),
                               p['ffn_w_down'][j].astype(BF16), p['ln2_g'][i], p['ln2_b'][i])
        else:
            x, xb = _moe(x, p['moe_router'][j], p['moe_w_gate'], p['moe_w_up'], p['moe_w_down'], j,
                         p['ln2_g'][i], p['ln2_b'][i])
    return x


def kernel(x, mem, ln_in_g, ln_in_b, w_in, ssm_a_re, ssm_a_im, ssm_log_dt, ssm_b_re, ssm_b_im, ssm_c_re, ssm_c_im,
           ssm_d, ssm_w_glu, nsa_pos_k, nsa_pos_v, nsa_ck1, nsa_ck2, nsa_cv1, nsa_cv2, mem_wk, mem_wv, w_o, ln1_g,
           ln1_b, ln2_g, ln2_b, ffn_w_gate, ffn_w_up, ffn_w_down, moe_router, moe_w_gate, moe_w_up, moe_w_down):
    p = dict(ln_in_g=ln_in_g, ln_in_b=ln_in_b, w_in=w_in, ssm_a_re=ssm_a_re, ssm_a_im=ssm_a_im,
             ssm_log_dt=ssm_log_dt, ssm_b_re=ssm_b_re, ssm_b_im=ssm_b_im, ssm_c_re=ssm_c_re, ssm_c_im=ssm_c_im,
             ssm_d=ssm_d, ssm_w_glu=ssm_w_glu, nsa_pos_k=nsa_pos_k, nsa_pos_v=nsa_pos_v, nsa_ck1=nsa_ck1,
             nsa_ck2=nsa_ck2, nsa_cv1=nsa_cv1, nsa_cv2=nsa_cv2, mem_wk=mem_wk, mem_wv=mem_wv, w_o=w_o,
             ln1_g=ln1_g, ln1_b=ln1_b, ln2_g=ln2_g, ln2_b=ln2_b, ffn_w_gate=ffn_w_gate, ffn_w_up=ffn_w_up,
             ffn_w_down=ffn_w_down, moe_router=moe_router, moe_w_gate=moe_w_gate, moe_w_up=moe_w_up,
             moe_w_down=moe_w_down)
    assert x.shape[0] == 1 and mem.shape[0] == 1
    return _trunk(x[0], mem[0], p)[None]
```

```python
import functools
import math

import jax
import jax.numpy as jnp
from jax import lax
from jax.experimental import pallas as pl
from jax.experimental.pallas import tpu as pltpu

F32 = jnp.float32
BF16 = jnp.bfloat16

D_MODEL = 2048
DEPTH = 4
HEAD_DIM = 128
GROUP_WIDTH = 512
N_HEADS = 4
SSM_CH = 512
SSM_GROUP = 16
SSM_NG = 32
SSM_STATE = 64
SSM_NSTATE = SSM_NG * SSM_STATE
MOBA_BLOCK = 256
MOBA_TOPK = 3
NSA_CMP_LEN = 32
NSA_CMP_STRIDE = 16
NSA_SEL_BLOCK = 64
NSA_TOPK = 16
NSA_WINDOW = 512
ROPE_THETA = 500000.0
ROPE_DIM = 32
D_FF = 5632
N_EXPERTS = 8
TOP_K = 2
Q_BLOCK = 128
LN_EPS = 1e-5
NEG_INF = -1e30
FORCE_SCORE = 1e9
ALPHA = (2.0 * DEPTH) ** 0.25
SCALE = HEAD_DIM ** -0.5 * math.log2(math.e)

ROW_TILE = 512
SCAN_ROWS = 512
SCAN_LANES = 512
SEL_CHUNK = 512
SEL_STEP_CHUNKS = 4
MOBA_STEP_BLOCKS = 8
FF_TILE = 512
MOE_ROWS = 1024
MOE_FF_TILE = 256
COMBINE_ROWS = 256
PROJ_WIDTH = 3968

_NT = (((1,), (1,)), ((), ()))


def _cparams(sem, vmem_mb=48):
    return pltpu.CompilerParams(dimension_semantics=sem, vmem_limit_bytes=vmem_mb * 1024 * 1024)


def _resident(block_shape, index_map):
    return pl.BlockSpec(block_shape, index_map, pipeline_mode=pl.Buffered(1))


def _layer_norm(r, g, b):
    mu = jnp.mean(r, axis=-1, keepdims=True)
    d = r - mu
    var = jnp.mean(d * d, axis=-1, keepdims=True)
    return d * lax.rsqrt(var + LN_EPS) * g + b


def _gelu_tanh(x):
    return x * (0.5 * (1.0 + jnp.tanh(math.sqrt(2.0 / math.pi) * (x + 0.044715 * (x * x * x)))))


def _sigmoid(x):
    return 1.0 / (1.0 + jnp.exp(-x))


def _split3(a):
    a1 = a.astype(BF16)
    r1 = a - a1.astype(F32)
    a2 = r1.astype(BF16)
    a3 = (r1 - a2.astype(F32)).astype(BF16)
    return a1, a2, a3


def _topk_mask(score, ids, k, n):
    def body(_, carry):
        sc, sel = carry
        m = jnp.max(sc, axis=1, keepdims=True)
        idx = jnp.min(jnp.where(sc == m, ids, float(n)), axis=1, keepdims=True)
        hit = ids == idx
        sel = jnp.where(jnp.logical_and(hit, m > 0.5 * NEG_INF), 1.0, sel)
        sc = jnp.where(hit, NEG_INF, sc)
        return sc, sel

    _, sel = lax.fori_loop(0, k, body, (score, jnp.zeros(score.shape, F32)))
    return sel


def _ln_in_kernel(x_ref, g_ref, b_ref, o_ref, ob_ref):
    y = _layer_norm(x_ref[...], g_ref[...], b_ref[...])
    o_ref[...] = y
    ob_ref[...] = y.astype(BF16)


def _ln_in(x, g, b):
    L = x.shape[0]
    row = pl.BlockSpec((ROW_TILE, D_MODEL), lambda i: (i, 0))
    vec = pl.BlockSpec((1, D_MODEL), lambda i: (0, 0))
    return pl.pallas_call(
        _ln_in_kernel,
        grid=(L // ROW_TILE,),
        in_specs=[row, vec, vec],
        out_specs=[row, row],
        out_shape=[jax.ShapeDtypeStruct((L, D_MODEL), F32), jax.ShapeDtypeStruct((L, D_MODEL), BF16)],
        compiler_params=_cparams(("parallel",)),
        name="ln_in",
    )(x, g.reshape(1, -1), b.reshape(1, -1))


def _proj_kernel(x_ref, w_ref, cos_ref, sa_ref, sb_ref,
                 u_ref, mq_ref, mk_ref, mv_ref, nqc_ref, nqr_ref, cmp_ref, nkv_ref, xq_ref, g_ref):
    x = x_ref[...]
    cos, sa, sb = cos_ref[...], sa_ref[...], sb_ref[...]

    def dot(c0, n):
        return jnp.dot(x, w_ref[:, c0:c0 + n], preferred_element_type=F32)

    def rope(t):
        return t * cos + pltpu.roll(t, HEAD_DIM - 16, 1) * sa + pltpu.roll(t, 16, 1) * sb

    heads = [slice(h * HEAD_DIM, (h + 1) * HEAD_DIM) for h in range(N_HEADS)]
    u_ref[...] = dot(0, 512)
    t = dot(512, 512)
    for cs in heads:
        mq_ref[:, cs] = (rope(t[:, cs]) * SCALE).astype(BF16)
    t = dot(1024, 512)
    for cs in heads:
        mk_ref[:, cs] = rope(t[:, cs]).astype(BF16)
    mv_ref[...] = dot(1536, 512).astype(BF16)
    t = dot(2048, 512)
    nqc_ref[...] = (t * SCALE).astype(BF16)
    for cs in heads:
        nqr_ref[:, cs] = (rope(t[:, cs]) * SCALE).astype(BF16)
    t = dot(2560, 512)
    cmp_ref[0] = t[:, heads[0]].astype(BF16)
    cmp_ref[1] = t[:, heads[1]].astype(BF16)
    nkv_ref[:, heads[0]] = rope(t[:, heads[2]]).astype(BF16)
    nkv_ref[:, heads[1]] = t[:, heads[3]].astype(BF16)
    t = dot(3072, 256)
    nkv_ref[:, heads[2]] = rope(t[:, heads[0]]).astype(BF16)
    nkv_ref[:, heads[3]] = t[:, heads[1]].astype(BF16)
    t = dot(3328, 640)
    xq_ref[...] = (t[:, :512] * SCALE).astype(BF16)
    g_ref[...] = _sigmoid(t[:, 512:])


def _proj(xb, w, cosf, sa, sb):
    L = xb.shape[0]
    tm = ROW_TILE
    row = lambda n: pl.BlockSpec((tm, n), lambda i: (i, 0))
    outs = [
        (jax.ShapeDtypeStruct((L, 512), F32), row(512)),
        (jax.ShapeDtypeStruct((L, 512), BF16), row(512)),
        (jax.ShapeDtypeStruct((L, 512), BF16), row(512)),
        (jax.ShapeDtypeStruct((L, 512), BF16), row(512)),
        (jax.ShapeDtypeStruct((L, 512), BF16), row(512)),
        (jax.ShapeDtypeStruct((L, 512), BF16), row(512)),
        (jax.ShapeDtypeStruct((2, L, 128), BF16), pl.BlockSpec((2, tm, 128), lambda i: (0, i, 0))),
        (jax.ShapeDtypeStruct((L, 512), BF16), row(512)),
        (jax.ShapeDtypeStruct((L, 512), BF16), row(512)),
        (jax.ShapeDtypeStruct((L, 128), F32), row(128)),
    ]
    return pl.pallas_call(
        _proj_kernel,
        grid=(L // tm,),
        in_specs=[row(D_MODEL), _resident((D_MODEL, PROJ_WIDTH), lambda i: (0, 0)),
                  row(128), row(128), row(128)],
        out_specs=[o[1] for o in outs],
        out_shape=[o[0] for o in outs],
        compiler_params=_cparams(("parallel",), 56),
        name="in_proj",
    )(xb, w, cosf, sa, sb)


def _s5_kernel(u_ref, bmat_ref, cmat_ref, pw_ref, d_ref, wglu_ref, o_ref, hr_ref, hi_ref, car_ref, cai_ref):
    T = u_ref.shape[0]
    W = SCAN_LANES

    @pl.when(pl.program_id(0) == 0)
    def _():
        car_ref[...] = jnp.zeros(car_ref.shape, F32)
        cai_ref[...] = jnp.zeros(cai_ref.shape, F32)

    u = u_ref[...]
    ub = u.astype(BF16)
    for b in range(4):
        bu = jnp.dot(ub[:, b * 128:(b + 1) * 128], bmat_ref[b], preferred_element_type=F32)
        hr_ref[:, b * 512:(b + 1) * 512] = bu[:, :512]
        hi_ref[:, b * 512:(b + 1) * 512] = bu[:, 512:]

    for cb in range(SSM_NSTATE // W):
        sl = slice(cb * W, (cb + 1) * W)
        steps = [(pw_ref[2 * j, :, sl], pw_ref[2 * j + 1, :, sl], 1 << j) for j in range(3)]
        p_r, p_i = pw_ref[6, :, sl], pw_ref[7, :, sl]

        def tile(t, carry, sl=sl, steps=steps, p_r=p_r, p_i=p_i):
            c_r, c_i = carry
            rows = pl.ds(pl.multiple_of(t * 8, 8), 8)
            x_r, x_i = hr_ref[rows, sl], hi_ref[rows, sl]
            for l_r, l_i, d in steps:
                s_r, s_i = pltpu.roll(x_r, d, 0), pltpu.roll(x_i, d, 0)
                x_r, x_i = x_r + l_r * s_r - l_i * s_i, x_i + l_r * s_i + l_i * s_r
            x_r, x_i = x_r + p_r * c_r - p_i * c_i, x_i + p_r * c_i + p_i * c_r
            hr_ref[rows, sl] = x_r
            hi_ref[rows, sl] = x_i
            return (jnp.broadcast_to(x_r[7:8, :], (8, W)), jnp.broadcast_to(x_i[7:8, :], (8, W)))

        c_r, c_i = lax.fori_loop(0, T // 8, tile, (car_ref[:, sl], cai_ref[:, sl]))
        car_ref[:, sl] = c_r
        cai_ref[:, sl] = c_i

    ys = []
    for b in range(4):
        h_r = hr_ref[:, b * 512:(b + 1) * 512].astype(BF16)
        h_i = hi_ref[:, b * 512:(b + 1) * 512].astype(BF16)
        ys.append(jnp.dot(h_r, cmat_ref[b, :512, :], preferred_element_type=F32)
                  + jnp.dot(h_i, cmat_ref[b, 512:, :], preferred_element_type=F32))
    y = jnp.concatenate(ys, axis=1) + d_ref[...] * u
    z = _gelu_tanh(y)
    gate = _sigmoid(jnp.dot(z.astype(BF16), wglu_ref[...], preferred_element_type=F32))
    o_ref[...] = (z * gate).astype(BF16)


def _s5_params(a_re, a_im, log_dt, b_re, b_im, c_re, c_im):
    dt = jnp.exp(log_dt)[:, None]
    mag = jnp.exp(a_re * dt)
    lam_re, lam_im = mag * jnp.cos(a_im * dt), mag * jnp.sin(a_im * dt)
    den = a_re * a_re + a_im * a_im
    nr, ni = lam_re - 1.0, lam_im
    coef_re = (nr * a_re + ni * a_im) / den
    coef_im = (ni * a_re - nr * a_im) / den
    bbar_re = coef_re[..., None] * b_re - coef_im[..., None] * b_im
    bbar_im = coef_re[..., None] * b_im + coef_im[..., None] * b_re
    eye = jnp.eye(8, dtype=F32)
    to_b = lambda m: jnp.einsum('bgnc,gh->bgchn', m.reshape(4, 8, SSM_STATE, SSM_GROUP), eye).reshape(4, 128, 512)
    bmat = jnp.concatenate([to_b(bbar_re), to_b(bbar_im)], axis=-1).astype(BF16)
    to_c = lambda m: jnp.einsum('bgcn,gh->bgnhc', m.reshape(4, 8, SSM_GROUP, SSM_STATE), eye).reshape(4, 512, 128)
    cmat = jnp.concatenate([to_c(c_re), -to_c(c_im)], axis=1).astype(BF16)
    lr, li = lam_re.reshape(-1), lam_im.reshape(-1)
    pr, pi = [lr], [li]
    for _ in range(7):
        pr, pi = pr + [pr[-1] * lr - pi[-1] * li], pi + [pr[-1] * li + pi[-1] * lr]
    row = jnp.arange(8)[:, None]
    tabs = []
    for d in (1, 2, 4):
        tabs += [jnp.where(row >= d, pr[d - 1][None, :], 0.0), jnp.where(row >= d, pi[d - 1][None, :], 0.0)]
    tabs += [jnp.stack(pr, axis=0), jnp.stack(pi, axis=0)]
    return bmat, cmat, jnp.stack(tabs, axis=0)


def _s5(u, bmat, cmat, pw, d_skip, w_glu):
    L = u.shape[0]
    T = SCAN_ROWS
    row = pl.BlockSpec((T, 512), lambda i: (i, 0))
    full = lambda shape: pl.BlockSpec(shape, lambda i: (0,) * len(shape))
    return pl.pallas_call(
        _s5_kernel,
        grid=(L // T,),
        in_specs=[row, full((4, 128, 1024)), full((4, 1024, 128)), full((8, 8, SSM_NSTATE)),
                  full((1, 512)), full((512, 512))],
        out_specs=row,
        out_shape=jax.ShapeDtypeStruct((L, 512), BF16),
        scratch_shapes=[pltpu.VMEM((T, SSM_NSTATE), F32), pltpu.VMEM((T, SSM_NSTATE), F32),
                        pltpu.VMEM((8, SSM_NSTATE), F32), pltpu.VMEM((8, SSM_NSTATE), F32)],
        compiler_params=_cparams(("arbitrary",)),
        name="s5",
    )(u, bmat, cmat, pw, d_skip.reshape(1, -1), w_glu.astype(BF16))


def _kmean_kernel(k_ref, o_ref):
    rows = k_ref.shape[0]
    k = k_ref[...].astype(F32).reshape(rows // MOBA_BLOCK, MOBA_BLOCK, GROUP_WIDTH)
    o_ref[...] = jnp.mean(k, axis=1).astype(BF16)


def _kmean(k):
    L = k.shape[0]
    rows = 8 * MOBA_BLOCK if L % (8 * MOBA_BLOCK) == 0 else L
    return pl.pallas_call(
        _kmean_kernel,
        grid=(L // rows,),
        in_specs=[pl.BlockSpec((rows, GROUP_WIDTH), lambda i: (i, 0))],
        out_specs=pl.BlockSpec((rows // MOBA_BLOCK, GROUP_WIDTH), lambda i: (i, 0)),
        out_shape=jax.ShapeDtypeStruct((L // MOBA_BLOCK, GROUP_WIDTH), BF16),
        compiler_params=_cparams(("parallel",)),
        name="moba_kmean",
    )(k)


def _with_ones(v):
    return jnp.concatenate([v, jnp.ones(v.shape, v.dtype)], axis=1)


def _flash_step(s, v_ones, m_ref, acc_ref, idx, first):
    row_max = jnp.max(s, axis=1, keepdims=True)
    if first:
        m_new = jnp.broadcast_to(row_max, (s.shape[0], HEAD_DIM))
    else:
        m_old = m_ref[idx]
        m_new = jnp.maximum(m_old, row_max)
    p = jnp.concatenate([jnp.exp2(s[:, c:c + HEAD_DIM] - m_new) for c in range(0, s.shape[1], HEAD_DIM)], axis=1)
    pv = jnp.dot(p.astype(BF16), v_ones, preferred_element_type=F32)
    if first:
        acc_ref[idx] = pv
    else:
        a = jnp.exp2(m_old - m_new)
        acc_ref[idx] = jnp.concatenate([a, a], axis=1) * acc_ref[idx] + pv
    m_ref[idx] = m_new


def _flash_result(acc_ref, idx):
    acc = acc_ref[idx]
    return acc[:, :HEAD_DIM] / acc[:, HEAD_DIM:]


def _moba_kernel(q_ref, k_ref, v_ref, km_ref, o_ref, qa_ref, m_ref, acc_ref):
    TQ = q_ref.shape[0]
    own = pl.program_id(0)
    heads = [slice(h * HEAD_DIM, (h + 1) * HEAD_DIM) for h in range(N_HEADS)]

    gs = jnp.concatenate([lax.dot_general(q_ref[:, cs], km_ref[:, cs], _NT, preferred_element_type=F32)
                          for cs in heads], axis=0)
    blk = lax.broadcasted_iota(jnp.int32, gs.shape, 1)
    sel = _topk_mask(jnp.where(blk < own, gs, NEG_INF), blk.astype(F32), MOBA_TOPK, HEAD_DIM)
    bias = jnp.where(sel > 0.5, 0.0, NEG_INF).astype(BF16)
    for h, cs in enumerate(heads):
        qa_ref[h] = jnp.concatenate([q_ref[:, cs], bias[h * TQ:(h + 1) * TQ]], axis=1)

    own_rows = pl.ds(pl.multiple_of(own * MOBA_BLOCK, MOBA_BLOCK), MOBA_BLOCK)
    causal = (lax.broadcasted_iota(jnp.int32, (TQ, MOBA_BLOCK), 1)
              <= lax.broadcasted_iota(jnp.int32, (TQ, MOBA_BLOCK), 0))
    for h, cs in enumerate(heads):
        s = lax.dot_general(q_ref[:, cs], k_ref[own_rows, cs], _NT, preferred_element_type=F32)
        _flash_step(jnp.where(causal, s, NEG_INF), _with_ones(v_ref[own_rows, cs]), m_ref, acc_ref, h, True)

    def past(first_block, n_blocks):
        n = n_blocks * MOBA_BLOCK
        rows = pl.ds(pl.multiple_of(first_block * MOBA_BLOCK, MOBA_BLOCK), n)
        blk = first_block + lax.broadcasted_iota(jnp.int32, (n, HEAD_DIM), 0) // MOBA_BLOCK
        tag = jnp.where(lax.broadcasted_iota(jnp.int32, (n, HEAD_DIM), 1) == blk, 1.0, 0.0).astype(BF16)
        for h, cs in enumerate(heads):
            k_aug = jnp.concatenate([k_ref[rows, cs], tag], axis=1)
            s = lax.dot_general(qa_ref[h], k_aug, _NT, preferred_element_type=F32)
            _flash_step(s, _with_ones(v_ref[rows, cs]), m_ref, acc_ref, h, False)

    def group(t, carry):
        past(MOBA_STEP_BLOCKS * t, MOBA_STEP_BLOCKS)
        return carry

    lax.fori_loop(0, own // MOBA_STEP_BLOCKS, group, 0)
    size = MOBA_STEP_BLOCKS // 2
    while size >= 1:
        @pl.when((own & size) != 0)
        def _(size=size):
            past(own & ~(2 * size - 1), size)
        size //= 2

    for h, cs in enumerate(heads):
        o_ref[:, cs] = _flash_result(acc_ref, h).astype(BF16)


def _moba(q, k, v, kmean):
    L = q.shape[0]
    TQ = MOBA_BLOCK
    nb = L // MOBA_BLOCK
    assert nb <= HEAD_DIM
    km = jnp.pad(kmean, ((0, HEAD_DIM - nb), (0, 0)))
    tile = pl.BlockSpec((TQ, GROUP_WIDTH), lambda i: (i, 0))
    return pl.pallas_call(
        _moba_kernel,
        grid=(L // TQ,),
        in_specs=[tile, _resident((L, GROUP_WIDTH), lambda i: (0, 0)), _resident((L, GROUP_WIDTH), lambda i: (0, 0)),
                  _resident((HEAD_DIM, GROUP_WIDTH), lambda i: (0, 0))],
        out_specs=tile,
        out_shape=jax.ShapeDtypeStruct((L, GROUP_WIDTH), BF16),
        scratch_shapes=[pltpu.VMEM((N_HEADS, TQ, 2 * HEAD_DIM), BF16), pltpu.VMEM((N_HEADS, TQ, HEAD_DIM), F32),
                        pltpu.VMEM((N_HEADS, TQ, 2 * HEAD_DIM), F32)],
        compiler_params=_cparams(("parallel",)),
        name="moba",
    )(q, k, v, km)


def _compress_kernel(c_ref, w1_ref, w2_ref, pos_ref, o_ref):
    n = c_ref.shape[1]
    half = w1_ref.shape[1] // 2
    c = c_ref[0]
    a = jnp.dot(c, w1_ref[0, :half, :], preferred_element_type=F32)
    b = jnp.dot(c, w1_ref[0, half:, :], preferred_element_type=F32)
    bias = jnp.dot(pos_ref[0], w1_ref[0], preferred_element_type=F32)[0:1, :]
    hid = _gelu_tanh(a + pltpu.roll(b, n - 1, 0) + bias)
    out = jnp.dot(hid.astype(BF16), w2_ref[0], preferred_element_type=F32)
    last = lax.broadcasted_iota(jnp.int32, out.shape, 0) == n - 1
    o_ref[0] = jnp.where(last, 0.0, out).astype(BF16)


def _compress(cmp_kv, w1, w2, pos):
    L = cmp_kv.shape[1]
    n = L // NSA_CMP_STRIDE
    wide = NSA_CMP_STRIDE * HEAD_DIM
    chunks = cmp_kv.reshape(2, n, wide)
    hidden = w1.shape[-1]
    return pl.pallas_call(
        _compress_kernel,
        grid=(2,),
        in_specs=[pl.BlockSpec((1, n, wide), lambda j: (j, 0, 0)),
                  pl.BlockSpec((1, 2 * wide, hidden), lambda j: (j, 0, 0)),
                  pl.BlockSpec((1, hidden, HEAD_DIM), lambda j: (j, 0, 0)),
                  pl.BlockSpec((1, 8, 2 * wide), lambda j: (j, 0, 0))],
        out_specs=pl.BlockSpec((1, n, HEAD_DIM), lambda j: (j, 0, 0)),
        out_shape=jax.ShapeDtypeStruct((2, n, HEAD_DIM), BF16),
        compiler_params=_cparams(("parallel",)),
        name="nsa_compress",
    )(chunks, w1, w2, pos)


def _stack_heads(ref):
    return jnp.concatenate([ref[:, h * HEAD_DIM:(h + 1) * HEAD_DIM] for h in range(N_HEADS)], axis=0)


def _nsa_cmp_kernel(q_ref, kc_ref, vc_ref, agg_ref, o_ref, sel_ref):
    TQ = q_ref.shape[0]
    nc = kc_ref.shape[1]
    ns = nc * NSA_CMP_STRIDE // NSA_SEL_BLOCK
    start = pl.program_id(0) * TQ
    qs = _stack_heads(q_ref)
    s = lax.dot_general(qs, kc_ref[0], _NT, preferred_element_type=F32)
    cend = lax.broadcasted_iota(jnp.int32, s.shape, 1) * NSA_CMP_STRIDE + (NSA_CMP_LEN - 1)
    ok = cend <= start + (lax.broadcasted_iota(jnp.int32, s.shape, 0) & (TQ - 1))
    s = jnp.where(ok, s, NEG_INF)
    m = jnp.max(s, axis=1, keepdims=True)
    e = jnp.where(ok, jnp.exp2(s - m), 0.0)
    p = e / jnp.maximum(jnp.sum(e, axis=1, keepdims=True), 1e-30)
    oc = jnp.dot(p.astype(BF16), vc_ref[0], preferred_element_type=F32)
    for h in range(N_HEADS):
        o_ref[:, h * HEAD_DIM:(h + 1) * HEAD_DIM] = oc[h * TQ:(h + 1) * TQ, :]

    imp = p[0:TQ] + p[TQ:2 * TQ] + p[2 * TQ:3 * TQ] + p[3 * TQ:4 * TQ]
    agg = agg_ref[...]
    imp_sel = sum(jnp.dot(t, agg, preferred_element_type=F32) for t in _split3(imp))
    blk = lax.broadcasted_iota(jnp.int32, imp_sel.shape, 1)
    own = (start + lax.broadcasted_iota(jnp.int32, imp_sel.shape, 0)) // NSA_SEL_BLOCK
    score = jnp.where(blk == own, FORCE_SCORE, jnp.where(blk < own, imp_sel, NEG_INF))
    sel = _topk_mask(score, blk.astype(F32), min(NSA_TOPK, ns), imp_sel.shape[1])
    sel_ref[...] = jnp.where(sel > 0.5, 0.0, NEG_INF).astype(BF16)


def _nsa_cmp(nq, kvc, agg):
    L = nq.shape[0]
    TQ = 2 * Q_BLOCK
    nc = kvc.shape[1]
    lanes = agg.shape[1]
    return pl.pallas_call(
        _nsa_cmp_kernel,
        grid=(L // TQ,),
        in_specs=[pl.BlockSpec((TQ, GROUP_WIDTH), lambda i: (i, 0)),
                  pl.BlockSpec((1, nc, HEAD_DIM), lambda i: (0, 0, 0)),
                  pl.BlockSpec((1, nc, HEAD_DIM), lambda i: (1, 0, 0)),
                  pl.BlockSpec((nc, lanes), lambda i: (0, 0))],
        out_specs=[pl.BlockSpec((TQ, GROUP_WIDTH), lambda i: (i, 0)),
                   pl.BlockSpec((TQ, lanes), lambda i: (i, 0))],
        out_shape=[jax.ShapeDtypeStruct((L, GROUP_WIDTH), F32), jax.ShapeDtypeStruct((L, lanes), BF16)],
        compiler_params=_cparams(("parallel",)),
        name="nsa_cmp",
    )(nq, kvc, kvc, agg)


def _nsa_sel_kernel(q_ref, ks_ref, vs_ref, kw_ref, vw_ref, sb_ref, tag_ref, oc_ref, g_ref, o_ref,
                    qa_ref, m_ref, acc_ref):
    TQ = q_ref.shape[0]
    CK = tag_ref.shape[1]
    cpg = HEAD_DIM // (CK // NSA_SEL_BLOCK)
    R = N_HEADS * TQ
    halves = (slice(0, R // 2), slice(R // 2, R))
    start = pl.program_id(0) * TQ
    qs = _stack_heads(q_ref)
    qpos = start + (lax.broadcasted_iota(jnp.int32, (R, 1), 0) & (TQ - 1))

    bias = sb_ref[...]
    for g in range(sb_ref.shape[1] // HEAD_DIM):
        qa_ref[g] = jnp.concatenate(
            [qs, jnp.concatenate([bias[:, g * HEAD_DIM:(g + 1) * HEAD_DIM]] * N_HEADS, axis=0)], axis=1)

    def scores(c, n):
        rows = pl.ds(pl.multiple_of(c * CK, CK), n * CK)
        tags = tag_ref[pl.ds(c % cpg, n)].reshape(n * CK, HEAD_DIM)
        k_aug = jnp.concatenate([ks_ref[rows, :], tags], axis=1)
        return rows, [lax.dot_general(qa_ref[c // cpg, hs, :], k_aug, _NT, preferred_element_type=F32)
                      for hs in halves]

    c_last = start // CK
    rows, ss = scores(c_last, 1)
    kpos = c_last * CK + lax.broadcasted_iota(jnp.int32, ss[0].shape, 1)
    v_ones = _with_ones(vs_ref[rows, :])
    for idx, hs in enumerate(halves):
        _flash_step(jnp.where(kpos <= qpos[hs], ss[idx], NEG_INF), v_ones, m_ref, acc_ref, idx, True)

    def past(c, n):
        rows, ss = scores(c, n)
        v_ones = _with_ones(vs_ref[rows, :])
        for idx in range(2):
            _flash_step(ss[idx], v_ones, m_ref, acc_ref, idx, False)

    def group(t, carry):
        past(SEL_STEP_CHUNKS * t, SEL_STEP_CHUNKS)
        return carry

    lax.fori_loop(0, c_last // SEL_STEP_CHUNKS, group, 0)
    size = SEL_STEP_CHUNKS // 2
    while size >= 1:
        @pl.when((c_last & size) != 0)
        def _(size=size):
            past(c_last & ~(2 * size - 1), size)
        size //= 2
    o_sel = jnp.concatenate([_flash_result(acc_ref, 0), _flash_result(acc_ref, 1)], axis=0)

    span = TQ + NSA_WINDOW
    w0 = jnp.maximum(start - NSA_WINDOW, 0)
    rows = pl.ds(pl.multiple_of(w0, TQ), span)
    s = lax.dot_general(qs, kw_ref[rows, :], _NT, preferred_element_type=F32)
    wpos = w0 + lax.broadcasted_iota(jnp.int32, s.shape, 1)
    s = jnp.where(jnp.logical_and(wpos <= qpos, wpos > qpos - NSA_WINDOW), s, NEG_INF)
    p = jnp.exp2(s - jnp.max(s, axis=1, keepdims=True))
    o_win = jnp.dot(p.astype(BF16), vw_ref[rows, :], preferred_element_type=F32) / jnp.sum(p, axis=1, keepdims=True)

    g = g_ref[...]
    for h in range(N_HEADS):
        hs = slice(h * TQ, (h + 1) * TQ)
        cs = slice(h * HEAD_DIM, (h + 1) * HEAD_DIM)
        o_ref[:, cs] = (g[:, 3 * h:3 * h + 1] * oc_ref[:, cs] + g[:, 3 * h + 1:3 * h + 2] * o_sel[hs]
                        + g[:, 3 * h + 2:3 * h + 3] * o_win[hs]).astype(BF16)


def _nsa_sel(nq_rope, nkv, sel_bias, tags, o_cmp, gates):
    L = nq_rope.shape[0]
    TQ = Q_BLOCK
    lanes = sel_bias.shape[1]
    R = N_HEADS * TQ
    col = lambda j: _resident((L, HEAD_DIM), lambda i, j=j: (0, j))
    return pl.pallas_call(
        _nsa_sel_kernel,
        grid=(L // TQ,),
        in_specs=[pl.BlockSpec((TQ, GROUP_WIDTH), lambda i: (i, 0)),
                  col(0), col(1), col(2), col(3),
                  pl.BlockSpec((TQ, lanes), lambda i: (i, 0)),
                  _resident(tags.shape, lambda i: (0, 0, 0)),
                  pl.BlockSpec((TQ, GROUP_WIDTH), lambda i: (i, 0)),
                  pl.BlockSpec((TQ, HEAD_DIM), lambda i: (i, 0))],
        out_specs=pl.BlockSpec((TQ, GROUP_WIDTH), lambda i: (i, 0)),
        out_shape=jax.ShapeDtypeStruct((L, GROUP_WIDTH), BF16),
        scratch_shapes=[pltpu.VMEM((lanes // HEAD_DIM, R, 2 * HEAD_DIM), BF16), pltpu.VMEM((2, R // 2, HEAD_DIM), F32),
                        pltpu.VMEM((2, R // 2, 2 * HEAD_DIM), F32)],
        compiler_params=_cparams(("parallel",)),
        name="nsa_sel_win",
    )(nq_rope, nkv, nkv, nkv, nkv, sel_bias, tags, o_cmp, gates)


def _matmul_kernel(a_ref, b_ref, o_ref):
    o_ref[...] = jnp.dot(a_ref[...], b_ref[...], preferred_element_type=F32).astype(o_ref.dtype)


def _mem_proj(mem_b, w):
    n, width = mem_b.shape[0], w.shape[1]
    return pl.pallas_call(
        _matmul_kernel,
        grid=(width // 512,),
        in_specs=[pl.BlockSpec((n, D_MODEL), lambda j: (0, 0)), pl.BlockSpec((D_MODEL, 512), lambda j: (0, j))],
        out_specs=pl.BlockSpec((n, 512), lambda j: (0, j)),
        out_shape=jax.ShapeDtypeStruct((n, width), BF16),
        compiler_params=_cparams(("parallel",)),
        name="mem_proj",
    )(mem_b, w)


def _xattn_kernel(q_ref, k_ref, v_ref, o_ref):
    for h in range(N_HEADS):
        cs = slice(h * HEAD_DIM, (h + 1) * HEAD_DIM)
        s = lax.dot_general(q_ref[:, cs], k_ref[:, cs], _NT, preferred_element_type=F32)
        p = jnp.exp2(s - jnp.max(s, axis=1, keepdims=True))
        o = jnp.dot(p.astype(BF16), v_ref[:, cs], preferred_element_type=F32)
        o_ref[:, cs] = (o / jnp.sum(p, axis=1, keepdims=True)).astype(BF16)


def _xattn(xq, mem_kv):
    L = xq.shape[0]
    n = mem_kv.shape[0]
    row = pl.BlockSpec((ROW_TILE, GROUP_WIDTH), lambda i: (i, 0))
    return pl.pallas_call(
        _xattn_kernel,
        grid=(L // ROW_TILE,),
        in_specs=[row, pl.BlockSpec((n, GROUP_WIDTH), lambda i: (0, 0)),
                  pl.BlockSpec((n, GROUP_WIDTH), lambda i: (0, 1))],
        out_specs=row,
        out_shape=jax.ShapeDtypeStruct((L, GROUP_WIDTH), BF16),
        compiler_params=_cparams(("parallel",)),
        name="xattn",
    )(xq, mem_kv, mem_kv)


def _out_proj_kernel(y0_ref, y1_ref, y2_ref, y3_ref, w_ref, x_ref, g_ref, b_ref, o_ref, ob_ref):
    acc = ALPHA * x_ref[...]
    for j, y_ref in enumerate((y0_ref, y1_ref, y2_ref, y3_ref)):
        acc = acc + jnp.dot(y_ref[...], w_ref[j * GROUP_WIDTH:(j + 1) * GROUP_WIDTH, :], preferred_element_type=F32)
    y = _layer_norm(acc, g_ref[...], b_ref[...])
    o_ref[...] = y
    ob_ref[...] = y.astype(BF16)


def _out_proj(ys, w_o, x, g, b):
    L = x.shape[0]
    tm = ROW_TILE
    part = pl.BlockSpec((tm, GROUP_WIDTH), lambda i: (i, 0))
    row = pl.BlockSpec((tm, D_MODEL), lambda i: (i, 0))
    vec = pl.BlockSpec((1, D_MODEL), lambda i: (0, 0))
    return pl.pallas_call(
        _out_proj_kernel,
        grid=(L // tm,),
        in_specs=[part, part, part, part, _resident((D_MODEL, D_MODEL), lambda i: (0, 0)), row, vec, vec],
        out_specs=[row, row],
        out_shape=[jax.ShapeDtypeStruct((L, D_MODEL), F32), jax.ShapeDtypeStruct((L, D_MODEL), BF16)],
        compiler_params=_cparams(("parallel",)),
        name="out_proj_ln",
    )(*ys, w_o, x, g.reshape(1, -1), b.reshape(1, -1))


def _swiglu_step(x_ref, wg_ref, wu_ref, wd_ref, acc_ref):
    f = pl.program_id(1)
    x = x_ref[...]
    gate = jnp.dot(x, wg_ref[0].astype(BF16), preferred_element_type=F32)
    up = jnp.dot(x, wu_ref[0].astype(BF16), preferred_element_type=F32)
    hid = (gate * _sigmoid(gate) * up).astype(BF16)
    part = jnp.dot(hid, wd_ref[0].astype(BF16), preferred_element_type=F32)

    @pl.when(f == 0)
    def _():
        acc_ref[...] = part

    @pl.when(f > 0)
    def _():
        acc_ref[...] += part


def _ffn_dense_kernel(x_ref, wg_ref, wu_ref, wd_ref, xr_ref, g_ref, b_ref, o_ref, ob_ref, acc_ref):
    _swiglu_step(x_ref, wg_ref, wu_ref, wd_ref, acc_ref)

    @pl.when(pl.program_id(1) == pl.num_programs(1) - 1)
    def _():
        y = _layer_norm(ALPHA * xr_ref[...] + acc_ref[...], g_ref[...], b_ref[...])
        o_ref[...] = y
        ob_ref[...] = y.astype(BF16)


def _ffn_dense(xb, x, wg, wu, wd, g, b):
    L = x.shape[0]
    tm, tf = ROW_TILE, FF_TILE
    row = pl.BlockSpec((tm, D_MODEL), lambda i, f: (i, 0))
    vec = pl.BlockSpec((1, D_MODEL), lambda i, f: (0, 0))
    return pl.pallas_call(
        _ffn_dense_kernel,
        grid=(L // tm, D_FF // tf),
        in_specs=[row,
                  pl.BlockSpec((1, D_MODEL, tf), lambda i, f: (0, 0, f)),
                  pl.BlockSpec((1, D_MODEL, tf), lambda i, f: (0, 0, f)),
                  pl.BlockSpec((1, tf, D_MODEL), lambda i, f: (0, f, 0)),
                  row, vec, vec],
        out_specs=[row, row],
        out_shape=[jax.ShapeDtypeStruct((L, D_MODEL), F32), jax.ShapeDtypeStruct((L, D_MODEL), BF16)],
        scratch_shapes=[pltpu.VMEM((tm, D_MODEL), F32)],
        compiler_params=_cparams(("parallel", "arbitrary")),
        name="ffn_dense_ln",
    )(xb, wg[None], wu[None], wd[None], x, g.reshape(1, -1), b.reshape(1, -1))


def _start_row_gather(idx_ref, base, n, src_hbm, dst_ref, sem):
    def body(r, carry):
        pltpu.make_async_copy(src_hbm.at[pl.ds(idx_ref[base + r], 1)], dst_ref.at[pl.ds(r, 1)], sem).start()
        return carry

    lax.fori_loop(0, n, body, 0, unroll=8)


def _wait_row_gather(n, src_hbm, dst_ref, sem):
    pltpu.make_async_copy(src_hbm.at[pl.ds(0, n)], dst_ref.at[pl.ds(0, n)], sem).wait()


def _ffn_moe_kernel(exp_ref, used_ref, tok_ref, x_hbm, wg_ref, wu_ref, wd_ref, o_ref, stage_ref, xb_ref, sem):
    b, f = pl.program_id(0), pl.program_id(1)
    tm = stage_ref.shape[0]
    live = b < used_ref[0]

    @pl.when(jnp.logical_and(live, f == 0))
    def _():
        @pl.when(b == 0)
        def _():
            _start_row_gather(tok_ref, 0, tm, x_hbm, stage_ref, sem)

        _wait_row_gather(tm, x_hbm, stage_ref, sem)
        xb_ref[...] = stage_ref[...].astype(BF16)

        @pl.when(b + 1 < used_ref[0])
        def _():
            _start_row_gather(tok_ref, (b + 1) * tm, tm, x_hbm, stage_ref, sem)

    @pl.when(live)
    def _():
        _swiglu_step(xb_ref, wg_ref, wu_ref, wd_ref, o_ref)

    @pl.when(jnp.logical_and(jnp.logical_not(live), f == 0))
    def _():
        o_ref[...] = jnp.zeros(o_ref.shape, F32)


def _ffn_moe(x, buf_tok, blk_exp, n_used, wg, wu, wd, layer):
    P = buf_tok.shape[0]
    tm, tf = MOE_ROWS, MOE_FF_TILE
    nf = D_FF // tf
    ff = lambda b, f, used: jnp.where(b < used[0], f, nf - 1)
    grid_spec = pltpu.PrefetchScalarGridSpec(
        num_scalar_prefetch=3,
        grid=(P // tm, nf),
        in_specs=[pl.BlockSpec(memory_space=pl.ANY),
                  pl.BlockSpec((None, 1, D_MODEL, tf), lambda b, f, e, u, t: (layer, e[b], 0, ff(b, f, u))),
                  pl.BlockSpec((None, 1, D_MODEL, tf), lambda b, f, e, u, t: (layer, e[b], 0, ff(b, f, u))),
                  pl.BlockSpec((None, 1, tf, D_MODEL), lambda b, f, e, u, t: (layer, e[b], ff(b, f, u), 0))],
        out_specs=pl.BlockSpec((tm, D_MODEL), lambda b, f, e, u, t: (b, 0)),
        scratch_shapes=[pltpu.VMEM((tm, D_MODEL), F32), pltpu.VMEM((tm, D_MODEL), BF16),
                        pltpu.SemaphoreType.DMA(())],
    )
    return pl.pallas_call(
        _ffn_moe_kernel,
        grid_spec=grid_spec,
        out_shape=jax.ShapeDtypeStruct((P, D_MODEL), F32),
        compiler_params=_cparams(("arbitrary", "arbitrary"), 56),
        name="ffn_moe",
    )(blk_exp, n_used, buf_tok, x, wg, wu, wd)


def _router_kernel(x_ref, r_ref, o_ref):
    xs = _split3(x_ref[...])
    rs = (r_ref[0], r_ref[1], r_ref[2])
    logits = None
    for i, j in ((0, 0), (0, 1), (1, 0), (1, 1), (0, 2), (2, 0)):
        t = jnp.dot(xs[i], rs[j], preferred_element_type=F32)
        logits = t if logits is None else logits + t
    lane = lax.broadcasted_iota(jnp.int32, logits.shape, 1)
    lane_f = lane.astype(F32)
    logits = jnp.where(lane < N_EXPERTS, logits, NEG_INF)
    m1 = jnp.max(logits, axis=1, keepdims=True)
    i1 = jnp.min(jnp.where(logits == m1, lane_f, 128.0), axis=1, keepdims=True)
    rest = jnp.where(lane_f == i1, NEG_INF, logits)
    m2 = jnp.max(rest, axis=1, keepdims=True)
    i2 = jnp.min(jnp.where(rest == m2, lane_f, 128.0), axis=1, keepdims=True)
    e2 = jnp.exp(m2 - m1)
    g1 = 1.0 / (1.0 + e2)
    g2 = e2 / (1.0 + e2)
    o_ref[...] = jnp.where(lane == 0, i1, jnp.where(lane == 1, i2, jnp.where(lane == 2, g1, jnp.where(lane == 3, g2, 0.0))))


def _router(x, router3):
    L = x.shape[0]
    return pl.pallas_call(
        _router_kernel,
        grid=(L // ROW_TILE,),
        in_specs=[pl.BlockSpec((ROW_TILE, D_MODEL), lambda i: (i, 0)),
                  pl.BlockSpec((3, D_MODEL, 128), lambda i: (0, 0, 0))],
        out_specs=pl.BlockSpec((ROW_TILE, 128), lambda i: (i, 0)),
        out_shape=jax.ShapeDtypeStruct((L, 128), F32),
        compiler_params=_cparams(("parallel",)),
        name="moe_router",
    )(x, router3)


def _combine_kernel(da_ref, db_ref, x_ref, gw_ref, g_ref, b_ref, y_hbm, o_ref, ob_ref, ya_ref, yb_ref, sem):
    i, n = pl.program_id(0), pl.num_programs(0)
    tm = x_ref.shape[0]
    slot = i % 2

    def start(step, slot):
        _start_row_gather(da_ref, step * tm, tm, y_hbm, ya_ref.at[slot], sem.at[slot])
        _start_row_gather(db_ref, step * tm, tm, y_hbm, yb_ref.at[slot], sem.at[slot])

    @pl.when(i == 0)
    def _():
        start(0, 0)

    @pl.when(i + 1 < n)
    def _():
        start(i + 1, 1 - slot)

    _wait_row_gather(tm, y_hbm, ya_ref.at[slot], sem.at[slot])
    _wait_row_gather(tm, y_hbm, yb_ref.at[slot], sem.at[slot])
    gw = gw_ref[...]
    f = gw[:, 2:3] * ya_ref[slot] + gw[:, 3:4] * yb_ref[slot]
    y = _layer_norm(ALPHA * x_ref[...] + f, g_ref[...], b_ref[...])
    o_ref[...] = y
    ob_ref[...] = y.astype(BF16)


def _combine(x, y_rows, dest_a, dest_b, route, g, b):
    L = x.shape[0]
    tm = COMBINE_ROWS
    row = pl.BlockSpec((tm, D_MODEL), lambda i, a, b: (i, 0))
    vec = pl.BlockSpec((1, D_MODEL), lambda i, a, b: (0, 0))
    grid_spec = pltpu.PrefetchScalarGridSpec(
        num_scalar_prefetch=2,
        grid=(L // tm,),
        in_specs=[row, pl.BlockSpec((tm, 128), lambda i, a, b: (i, 0)), vec, vec, pl.BlockSpec(memory_space=pl.ANY)],
        out_specs=[row, row],
        scratch_shapes=[pltpu.VMEM((2, tm, D_MODEL), F32), pltpu.VMEM((2, tm, D_MODEL), F32),
                        pltpu.SemaphoreType.DMA((2,))],
    )
    return pl.pallas_call(
        _combine_kernel,
        grid_spec=grid_spec,
        out_shape=[jax.ShapeDtypeStruct((L, D_MODEL), F32), jax.ShapeDtypeStruct((L, D_MODEL), BF16)],
        compiler_params=_cparams(("arbitrary",)),
        name="moe_combine_ln",
    )(dest_a, dest_b, x, route, g.reshape(1, -1), b.reshape(1, -1), y_rows)


def _moe(x, router, wg, wu, wd, layer, g, b):
    T = x.shape[0]
    r32 = jnp.pad(router, ((0, 0), (0, 128 - N_EXPERTS)))
    r1 = r32.astype(BF16)
    r2 = (r32 - r1.astype(F32)).astype(BF16)
    r3 = (r32 - r1.astype(F32) - r2.astype(F32)).astype(BF16)
    route = _router(x, jnp.stack([r1, r2, r3]))
    e_flat = route[:, :TOP_K].astype(jnp.int32).reshape(-1)
    onehot = (e_flat[:, None] == jnp.arange(N_EXPERTS, dtype=jnp.int32)[None, :]).astype(jnp.int32)
    csum = jnp.cumsum(onehot, axis=0)
    rank = jnp.sum(onehot * csum, axis=1) - 1
    counts = csum[-1]
    padded = (counts + MOE_ROWS - 1) // MOE_ROWS * MOE_ROWS
    pend = jnp.cumsum(padded)
    dest = (pend - padded)[e_flat] + rank
    n_blocks = -(-(T * TOP_K + N_EXPERTS * (MOE_ROWS - 1)) // MOE_ROWS)
    P = n_blocks * MOE_ROWS
    tok = jnp.repeat(jnp.arange(T, dtype=jnp.int32), TOP_K)
    buf_tok = jnp.zeros((P,), jnp.int32).at[dest].set(tok)
    n_used = (pend[-1] // MOE_ROWS).astype(jnp.int32).reshape(1)
    blk_start = jnp.arange(n_blocks, dtype=jnp.int32) * MOE_ROWS
    blk_exp = jnp.minimum(jnp.searchsorted(pend, jnp.minimum(blk_start, pend[-1] - 1), side='right'),
                          N_EXPERTS - 1).astype(jnp.int32)
    y_rows = _ffn_moe(x, buf_tok, blk_exp, n_used, wg, wu, wd, layer)
    d2 = dest.reshape(T, TOP_K)
    return _combine(x, y_rows, d2[:, 0], d2[:, 1], route, g, b)


def _rope_tables(L):
    inv_freq = 1.0 / (ROPE_THETA ** (jnp.arange(0, ROPE_DIM, 2, dtype=F32) / ROPE_DIM))
    ang = jnp.arange(L, dtype=F32)[:, None] * inv_freq[None, :]
    cos, sin = jnp.cos(ang), jnp.sin(ang)
    z = lambda n: jnp.zeros((L, n), F32)
    cosf = jnp.concatenate([cos, cos, jnp.ones((L, HEAD_DIM - ROPE_DIM), F32)], axis=1)
    sa = jnp.concatenate([-sin, z(HEAD_DIM - 16)], axis=1)
    sb = jnp.concatenate([z(16), sin, z(HEAD_DIM - ROPE_DIM)], axis=1)
    return cosf, sa, sb


def _w_in_pack_kernel(a_ref, b_ref, c_ref, o_ref):
    o_ref[:, :3328] = a_ref[...].astype(BF16)
    o_ref[:, 3328:3840] = b_ref[...].astype(BF16)
    o_ref[:, 3840:] = c_ref[...].astype(BF16)


def _reorder_w_in(w):
    gates = jnp.pad(w[:, 3328:3340], ((0, 0), (0, 128 - 12)))
    rows = 256
    blk = lambda n: pl.BlockSpec((rows, n), lambda i: (i, 0))
    return pl.pallas_call(
        _w_in_pack_kernel,
        grid=(D_MODEL // rows,),
        in_specs=[blk(3328), blk(512), blk(128)],
        out_specs=blk(PROJ_WIDTH),
        out_shape=jax.ShapeDtypeStruct((D_MODEL, PROJ_WIDTH), BF16),
        compiler_params=_cparams(("parallel",)),
        name="w_in_pack",
    )(w[:, :3328], w[:, 3340:3852], gates)


def _trunk(x, mem, p):
    L = x.shape[0]
    ns, nc = L // NSA_SEL_BLOCK, L // NSA_CMP_STRIDE
    cosf, sa, sb = _rope_tables(L)
    lanes = -(-ns // HEAD_DIM) * HEAD_DIM
    cj = jnp.arange(nc)[:, None] - 4 * jnp.arange(lanes)[None, :]
    agg = jnp.logical_and(cj >= -1, cj <= 3).astype(BF16)
    ck = min(SEL_CHUNK, L)
    per = ck // NSA_SEL_BLOCK
    n_tag = min(HEAD_DIM // per, L // ck)
    blk_id = jnp.arange(n_tag)[:, None, None] * per + jnp.arange(ck)[None, :, None] // NSA_SEL_BLOCK
    tags = (blk_id == jnp.arange(HEAD_DIM)[None, None, :]).astype(BF16)
    mem_b = mem.astype(BF16)

    x, xb = _ln_in(x, p['ln_in_g'], p['ln_in_b'])
    for i in range(DEPTH):
        u, mq, mk, mv, nqc, nqr, cmp_kv, nkv, xq, gates = _proj(xb, _reorder_w_in(p['w_in'][i]), cosf, sa, sb)
        bmat, cmat, pw = _s5_params(p['ssm_a_re'][i], p['ssm_a_im'][i], p['ssm_log_dt'][i], p['ssm_b_re'][i],
                                    p['ssm_b_im'][i], p['ssm_c_re'][i], p['ssm_c_im'][i])
        y_ssm = _s5(u, bmat, cmat, pw, p['ssm_d'][i], p['ssm_w_glu'][i])
        y_moba = _moba(mq, mk, mv, _kmean(mk))
        pos = jnp.stack([p['nsa_pos_k'][i].reshape(1, -1), p['nsa_pos_v'][i].reshape(1, -1)])
        pos = jnp.broadcast_to(pos, (2, 8, pos.shape[-1])).astype(BF16)
        kvc = _compress(cmp_kv, jnp.stack([p['nsa_ck1'][i], p['nsa_cv1'][i]]).astype(BF16),
                        jnp.stack([p['nsa_ck2'][i], p['nsa_cv2'][i]]).astype(BF16), pos)
        o_cmp, sel_bias = _nsa_cmp(nqc, kvc, agg)
        y_nsa = _nsa_sel(nqr, nkv, sel_bias, tags, o_cmp, gates)
        mem_kv = _mem_proj(mem_b, jnp.concatenate([p['mem_wk'][i], p['mem_wv'][i]], axis=1).astype(BF16))
        y_x = _xattn(xq, mem_kv)
        x, xb = _out_proj((y_ssm, y_moba, y_nsa, y_x), p['w_o'][i].astype(BF16), x, p['ln1_g'][i], p['ln1_b'][i])
        j = i // 2
        if i % 2 == 0:
            x, xb = _ffn_dense(xb, x, p['ffn_w_gate'][j].astype(BF16), p['ffn_w_up'][j].astype(BF16),
                               p['ffn_w_down'][j].astype(BF16), p['ln2_g'][i], p['ln2_b'][i])
        else:
            x, xb = _moe(x, p['moe_router'][j], p['moe_w_gate'], p['moe_w_up'], p['moe_w_down'], j,
                         p['ln2_g'][i], p['ln2_b'][i])
    return x


def kernel(x, mem, ln_in_g, ln_in_b, w_in, ssm_a_re, ssm_a_im, ssm_log_dt, ssm_b_re, ssm_b_im, ssm_c_re, ssm_c_im,
           ssm_d, ssm_w_glu, nsa_pos_k, nsa_pos_v, nsa_ck1, nsa_ck2, nsa_cv1, nsa_cv2, mem_wk, mem_wv, w_o, ln1_g,
           ln1_b, ln2_g, ln2_b, ffn_w_gate, ffn_w_up, ffn_w_down, moe_router, moe_w_gate, moe_w_up, moe_w_down):
    p = dict(ln_in_g=ln_in_g, ln_in_b=ln_in_b, w_in=w_in, ssm_a_re=ssm_a_re, ssm_a_im=ssm_a_im,
             ssm_log_dt=ssm_log_dt, ssm_b_re=ssm_b_re, ssm_b_im=ssm_b_im, ssm_c_re=ssm_c_re, ssm_c_im=ssm_c_im,
             ssm_d=ssm_d, ssm_w_glu=ssm_w_glu, nsa_pos_k=nsa_pos_k, nsa_pos_v=nsa_pos_v, nsa_ck1=nsa_ck1,
             nsa_ck2=nsa_ck2, nsa_cv1=nsa_cv1, nsa_cv2=nsa_cv2, mem_wk=mem_wk, mem_wv=mem_wv, w_o=w_o,
             ln1_g=ln1_g, ln1_b=ln1_b, ln2_g=ln2_g, ln2_b=ln2_b, ffn_w_gate=ffn_w_gate, ffn_w_up=ffn_w_up,
             ffn_w_down=ffn_w_down, moe_router=moe_router, moe_w_gate=moe_w_gate, moe_w_up=moe_w_up,
             moe_w_down=moe_w_down)
    assert x.shape[0] == 1 and mem.shape[0] == 1
    return _trunk(x[0], mem[0], p)[None]
```

```python
import functools
import math

import jax
import jax.numpy as jnp
from jax import lax
from jax.experimental import pallas as pl
from jax.experimental.pallas import tpu as pltpu

F32 = jnp.float32
BF16 = jnp.bfloat16

D_MODEL = 2048
DEPTH = 4
HEAD_DIM = 128
GROUP_WIDTH = 512
N_HEADS = 4
SSM_CH = 512
SSM_GROUP = 16
SSM_NG = 32
SSM_STATE = 64
SSM_NSTATE = SSM_NG * SSM_STATE
MOBA_BLOCK = 256
MOBA_TOPK = 3
NSA_CMP_LEN = 32
NSA_CMP_STRIDE = 16
NSA_SEL_BLOCK = 64
NSA_TOPK = 16
NSA_WINDOW = 512
ROPE_THETA = 500000.0
ROPE_DIM = 32
D_FF = 5632
N_EXPERTS = 8
TOP_K = 2
Q_BLOCK = 128
LN_EPS = 1e-5
NEG_INF = -1e30
FORCE_SCORE = 1e9
ALPHA = (2.0 * DEPTH) ** 0.25
SCALE = HEAD_DIM ** -0.5 * math.log2(math.e)

ROW_TILE = 512
SCAN_ROWS = 512
SCAN_LANES = 512
SEL_CHUNK = 512
SEL_STEP_CHUNKS = 4
MOBA_STEP_BLOCKS = 8
FF_TILE = 512
FF_ROW_CHUNK = 256
MOE_ROWS = 1024
MOE_FF_TILE = 256
COMBINE_ROWS = 256
PROJ_WIDTH = 3968

_NT = (((1,), (1,)), ((), ()))


def _cparams(sem, vmem_mb=48):
    return pltpu.CompilerParams(dimension_semantics=sem, vmem_limit_bytes=vmem_mb * 1024 * 1024)


def _resident(block_shape, index_map):
    return pl.BlockSpec(block_shape, index_map, pipeline_mode=pl.Buffered(1))


def _layer_norm(r, g, b):
    mu = jnp.mean(r, axis=-1, keepdims=True)
    d = r - mu
    var = jnp.mean(d * d, axis=-1, keepdims=True)
    return d * lax.rsqrt(var + LN_EPS) * g + b


def _gelu_tanh(x):
    return x * (0.5 * (1.0 + jnp.tanh(math.sqrt(2.0 / math.pi) * (x + 0.044715 * (x * x * x)))))


def _sigmoid(x):
    return 1.0 / (1.0 + jnp.exp(-x))


def _split3(a):
    a1 = a.astype(BF16)
    r1 = a - a1.astype(F32)
    a2 = r1.astype(BF16)
    a3 = (r1 - a2.astype(F32)).astype(BF16)
    return a1, a2, a3


def _topk_mask(score, k):
    ids = lax.broadcasted_iota(jnp.int32, score.shape, 1).astype(F32)

    def body(_, sc):
        m = jnp.max(sc, axis=1, keepdims=True)
        idx = jnp.min(jnp.where(sc == m, ids, float(score.shape[1])), axis=1, keepdims=True)
        return jnp.where(ids == idx, NEG_INF, sc)

    left = lax.fori_loop(0, k, body, score)
    return jnp.where(jnp.logical_and(score > 0.5 * NEG_INF, left <= 0.5 * NEG_INF), 1.0, 0.0)


def _ln_in_kernel(x_ref, g_ref, b_ref, o_ref, ob_ref):
    y = _layer_norm(x_ref[...], g_ref[...], b_ref[...])
    o_ref[...] = y
    ob_ref[...] = y.astype(BF16)


def _ln_in(x, g, b):
    L = x.shape[0]
    row = pl.BlockSpec((ROW_TILE, D_MODEL), lambda i: (i, 0))
    vec = pl.BlockSpec((1, D_MODEL), lambda i: (0, 0))
    return pl.pallas_call(
        _ln_in_kernel,
        grid=(L // ROW_TILE,),
        in_specs=[row, vec, vec],
        out_specs=[row, row],
        out_shape=[jax.ShapeDtypeStruct((L, D_MODEL), F32), jax.ShapeDtypeStruct((L, D_MODEL), BF16)],
        compiler_params=_cparams(("parallel",)),
        name="ln_in",
    )(x, g.reshape(1, -1), b.reshape(1, -1))


def _proj_kernel(x_ref, w_ref, cos_ref, sa_ref, sb_ref,
                 u_ref, mq_ref, mk_ref, mv_ref, nqc_ref, nqr_ref, cmp_ref, nkv_ref, xq_ref, g_ref):
    x = x_ref[...]
    cos, sa, sb = cos_ref[...], sa_ref[...], sb_ref[...]

    def dot(c0, n):
        return jnp.dot(x, w_ref[:, c0:c0 + n], preferred_element_type=F32)

    def rope(t):
        return t * cos + pltpu.roll(t, HEAD_DIM - 16, 1) * sa + pltpu.roll(t, 16, 1) * sb

    heads = [slice(h * HEAD_DIM, (h + 1) * HEAD_DIM) for h in range(N_HEADS)]
    u_ref[...] = dot(0, 512)
    t = dot(512, 512)
    for cs in heads:
        mq_ref[:, cs] = (rope(t[:, cs]) * SCALE).astype(BF16)
    t = dot(1024, 512)
    for cs in heads:
        mk_ref[:, cs] = rope(t[:, cs]).astype(BF16)
    mv_ref[...] = dot(1536, 512).astype(BF16)
    t = dot(2048, 512)
    nqc_ref[...] = (t * SCALE).astype(BF16)
    for cs in heads:
        nqr_ref[:, cs] = (rope(t[:, cs]) * SCALE).astype(BF16)
    t = dot(2560, 512)
    cmp_ref[0] = t[:, heads[0]].astype(BF16)
    cmp_ref[1] = t[:, heads[1]].astype(BF16)
    nkv_ref[:, heads[0]] = rope(t[:, heads[2]]).astype(BF16)
    nkv_ref[:, heads[1]] = t[:, heads[3]].astype(BF16)
    t = dot(3072, 256)
    nkv_ref[:, heads[2]] = rope(t[:, heads[0]]).astype(BF16)
    nkv_ref[:, heads[3]] = t[:, heads[1]].astype(BF16)
    t = dot(3328, 640)
    xq_ref[...] = (t[:, :512] * SCALE).astype(BF16)
    g_ref[...] = _sigmoid(t[:, 512:])


def _proj(xb, w, cosf, sa, sb):
    L = xb.shape[0]
    tm = ROW_TILE
    row = lambda n: pl.BlockSpec((tm, n), lambda i: (i, 0))
    outs = [
        (jax.ShapeDtypeStruct((L, 512), F32), row(512)),
        (jax.ShapeDtypeStruct((L, 512), BF16), row(512)),
        (jax.ShapeDtypeStruct((L, 512), BF16), row(512)),
        (jax.ShapeDtypeStruct((L, 512), BF16), row(512)),
        (jax.ShapeDtypeStruct((L, 512), BF16), row(512)),
        (jax.ShapeDtypeStruct((L, 512), BF16), row(512)),
        (jax.ShapeDtypeStruct((2, L, 128), BF16), pl.BlockSpec((2, tm, 128), lambda i: (0, i, 0))),
        (jax.ShapeDtypeStruct((L, 512), BF16), row(512)),
        (jax.ShapeDtypeStruct((L, 512), BF16), row(512)),
        (jax.ShapeDtypeStruct((L, 128), F32), row(128)),
    ]
    return pl.pallas_call(
        _proj_kernel,
        grid=(L // tm,),
        in_specs=[row(D_MODEL), _resident((D_MODEL, PROJ_WIDTH), lambda i: (0, 0)),
                  row(128), row(128), row(128)],
        out_specs=[o[1] for o in outs],
        out_shape=[o[0] for o in outs],
        compiler_params=_cparams(("parallel",), 56),
        name="in_proj",
    )(xb, w, cosf, sa, sb)


def _s5_kernel(u_ref, bmat_ref, cmat_ref, pw_ref, d_ref, wglu_ref, o_ref, hr_ref, hi_ref, car_ref, cai_ref):
    T = u_ref.shape[0]
    W = SCAN_LANES

    @pl.when(pl.program_id(0) == 0)
    def _():
        car_ref[...] = jnp.zeros(car_ref.shape, F32)
        cai_ref[...] = jnp.zeros(cai_ref.shape, F32)

    u = u_ref[...]
    ub = u.astype(BF16)
    for b in range(4):
        bu = jnp.dot(ub[:, b * 128:(b + 1) * 128], bmat_ref[b], preferred_element_type=F32)
        hr_ref[:, b * 512:(b + 1) * 512] = bu[:, :512]
        hi_ref[:, b * 512:(b + 1) * 512] = bu[:, 512:]

    for cb in range(SSM_NSTATE // W):
        sl = slice(cb * W, (cb + 1) * W)
        steps = [(pw_ref[2 * j, :, sl], pw_ref[2 * j + 1, :, sl], 1 << j) for j in range(3)]
        p_r, p_i = pw_ref[6, :, sl], pw_ref[7, :, sl]

        def tile(t, carry, sl=sl, steps=steps, p_r=p_r, p_i=p_i):
            c_r, c_i = carry
            rows = pl.ds(pl.multiple_of(t * 8, 8), 8)
            x_r, x_i = hr_ref[rows, sl], hi_ref[rows, sl]
            for l_r, l_i, d in steps:
                s_r, s_i = pltpu.roll(x_r, d, 0), pltpu.roll(x_i, d, 0)
                x_r, x_i = x_r + l_r * s_r - l_i * s_i, x_i + l_r * s_i + l_i * s_r
            x_r, x_i = x_r + p_r * c_r - p_i * c_i, x_i + p_r * c_i + p_i * c_r
            hr_ref[rows, sl] = x_r
            hi_ref[rows, sl] = x_i
            return (jnp.broadcast_to(x_r[7:8, :], (8, W)), jnp.broadcast_to(x_i[7:8, :], (8, W)))

        c_r, c_i = lax.fori_loop(0, T // 8, tile, (car_ref[:, sl], cai_ref[:, sl]))
        car_ref[:, sl] = c_r
        cai_ref[:, sl] = c_i

    ys = []
    for b in range(4):
        h_r = hr_ref[:, b * 512:(b + 1) * 512].astype(BF16)
        h_i = hi_ref[:, b * 512:(b + 1) * 512].astype(BF16)
        ys.append(jnp.dot(h_r, cmat_ref[b, :512, :], preferred_element_type=F32)
                  + jnp.dot(h_i, cmat_ref[b, 512:, :], preferred_element_type=F32))
    y = jnp.concatenate(ys, axis=1) + d_ref[...] * u
    z = _gelu_tanh(y)
    gate = _sigmoid(jnp.dot(z.astype(BF16), wglu_ref[...], preferred_element_type=F32))
    o_ref[...] = (z * gate).astype(BF16)


def _s5_params(a_re, a_im, log_dt, b_re, b_im, c_re, c_im):
    dt = jnp.exp(log_dt)[:, None]
    mag = jnp.exp(a_re * dt)
    lam_re, lam_im = mag * jnp.cos(a_im * dt), mag * jnp.sin(a_im * dt)
    den = a_re * a_re + a_im * a_im
    nr, ni = lam_re - 1.0, lam_im
    coef_re = (nr * a_re + ni * a_im) / den
    coef_im = (ni * a_re - nr * a_im) / den
    bbar_re = coef_re[..., None] * b_re - coef_im[..., None] * b_im
    bbar_im = coef_re[..., None] * b_im + coef_im[..., None] * b_re
    eye = jnp.eye(8, dtype=F32)
    to_b = lambda m: jnp.einsum('bgnc,gh->bgchn', m.reshape(4, 8, SSM_STATE, SSM_GROUP), eye).reshape(4, 128, 512)
    bmat = jnp.concatenate([to_b(bbar_re), to_b(bbar_im)], axis=-1).astype(BF16)
    to_c = lambda m: jnp.einsum('bgcn,gh->bgnhc', m.reshape(4, 8, SSM_GROUP, SSM_STATE), eye).reshape(4, 512, 128)
    cmat = jnp.concatenate([to_c(c_re), -to_c(c_im)], axis=1).astype(BF16)
    lr, li = lam_re.reshape(-1), lam_im.reshape(-1)
    pr, pi = [lr], [li]
    for _ in range(7):
        pr, pi = pr + [pr[-1] * lr - pi[-1] * li], pi + [pr[-1] * li + pi[-1] * lr]
    row = jnp.arange(8)[:, None]
    tabs = []
    for d in (1, 2, 4):
        tabs += [jnp.where(row >= d, pr[d - 1][None, :], 0.0), jnp.where(row >= d, pi[d - 1][None, :], 0.0)]
    tabs += [jnp.stack(pr, axis=0), jnp.stack(pi, axis=0)]
    return bmat, cmat, jnp.stack(tabs, axis=0)


def _s5(u, bmat, cmat, pw, d_skip, w_glu):
    L = u.shape[0]
    T = SCAN_ROWS
    row = pl.BlockSpec((T, 512), lambda i: (i, 0))
    full = lambda shape: pl.BlockSpec(shape, lambda i: (0,) * len(shape))
    return pl.pallas_call(
        _s5_kernel,
        grid=(L // T,),
        in_specs=[row, full((4, 128, 1024)), full((4, 1024, 128)), full((8, 8, SSM_NSTATE)),
                  full((1, 512)), full((512, 512))],
        out_specs=row,
        out_shape=jax.ShapeDtypeStruct((L, 512), BF16),
        scratch_shapes=[pltpu.VMEM((T, SSM_NSTATE), F32), pltpu.VMEM((T, SSM_NSTATE), F32),
                        pltpu.VMEM((8, SSM_NSTATE), F32), pltpu.VMEM((8, SSM_NSTATE), F32)],
        compiler_params=_cparams(("arbitrary",)),
        name="s5",
    )(u, bmat, cmat, pw, d_skip.reshape(1, -1), w_glu.astype(BF16))


def _kmean_kernel(k_ref, o_ref):
    rows = k_ref.shape[0]
    k = k_ref[...].astype(F32).reshape(rows // MOBA_BLOCK, MOBA_BLOCK, GROUP_WIDTH)
    o_ref[...] = jnp.mean(k, axis=1).astype(BF16)


def _kmean(k):
    L = k.shape[0]
    rows = 8 * MOBA_BLOCK if L % (8 * MOBA_BLOCK) == 0 else L
    return pl.pallas_call(
        _kmean_kernel,
        grid=(L // rows,),
        in_specs=[pl.BlockSpec((rows, GROUP_WIDTH), lambda i: (i, 0))],
        out_specs=pl.BlockSpec((rows // MOBA_BLOCK, GROUP_WIDTH), lambda i: (i, 0)),
        out_shape=jax.ShapeDtypeStruct((L // MOBA_BLOCK, GROUP_WIDTH), BF16),
        compiler_params=_cparams(("parallel",)),
        name="moba_kmean",
    )(k)


def _with_ones(v):
    return jnp.concatenate([v, jnp.ones(v.shape, v.dtype)], axis=1)


def _flash_step(s, v_ones, m_ref, acc_ref, idx, first):
    row_max = jnp.max(s, axis=1, keepdims=True)
    if first:
        m_new = jnp.broadcast_to(row_max, (s.shape[0], HEAD_DIM))
    else:
        m_old = m_ref[idx]
        m_new = jnp.maximum(m_old, row_max)
    p = jnp.concatenate([jnp.exp2(s[:, c:c + HEAD_DIM] - m_new) for c in range(0, s.shape[1], HEAD_DIM)], axis=1)
    pv = jnp.dot(p.astype(BF16), v_ones, preferred_element_type=F32)
    if first:
        acc_ref[idx] = pv
    else:
        a = jnp.exp2(m_old - m_new)
        acc_ref[idx] = jnp.concatenate([a, a], axis=1) * acc_ref[idx] + pv
    m_ref[idx] = m_new


def _flash_result(acc_ref, idx):
    acc = acc_ref[idx]
    return acc[:, :HEAD_DIM] / acc[:, HEAD_DIM:]


def _moba_kernel(q_ref, k_ref, v_ref, km_ref, o_ref, qa_ref, m_ref, acc_ref):
    TQ = q_ref.shape[0]
    own = pl.program_id(0)
    heads = [slice(h * HEAD_DIM, (h + 1) * HEAD_DIM) for h in range(N_HEADS)]

    gs = jnp.concatenate([lax.dot_general(q_ref[:, cs], km_ref[:, cs], _NT, preferred_element_type=F32)
                          for cs in heads], axis=0)
    blk = lax.broadcasted_iota(jnp.int32, gs.shape, 1)
    sel = _topk_mask(jnp.where(blk < own, gs, NEG_INF), MOBA_TOPK)
    bias = jnp.where(sel > 0.5, 0.0, NEG_INF).astype(BF16)
    for h, cs in enumerate(heads):
        qa_ref[h] = jnp.concatenate([q_ref[:, cs], bias[h * TQ:(h + 1) * TQ]], axis=1)

    own_rows = pl.ds(pl.multiple_of(own * MOBA_BLOCK, MOBA_BLOCK), MOBA_BLOCK)
    causal = (lax.broadcasted_iota(jnp.int32, (TQ, MOBA_BLOCK), 1)
              <= lax.broadcasted_iota(jnp.int32, (TQ, MOBA_BLOCK), 0))
    for h, cs in enumerate(heads):
        s = lax.dot_general(q_ref[:, cs], k_ref[own_rows, cs], _NT, preferred_element_type=F32)
        _flash_step(jnp.where(causal, s, NEG_INF), _with_ones(v_ref[own_rows, cs]), m_ref, acc_ref, h, True)

    def past(first_block, n_blocks):
        n = n_blocks * MOBA_BLOCK
        rows = pl.ds(pl.multiple_of(first_block * MOBA_BLOCK, MOBA_BLOCK), n)
        blk = first_block + lax.broadcasted_iota(jnp.int32, (n, HEAD_DIM), 0) // MOBA_BLOCK
        tag = jnp.where(lax.broadcasted_iota(jnp.int32, (n, HEAD_DIM), 1) == blk, 1.0, 0.0).astype(BF16)
        for h, cs in enumerate(heads):
            k_aug = jnp.concatenate([k_ref[rows, cs], tag], axis=1)
            s = lax.dot_general(qa_ref[h], k_aug, _NT, preferred_element_type=F32)
            _flash_step(s, _with_ones(v_ref[rows, cs]), m_ref, acc_ref, h, False)

    def group(t, carry):
        past(MOBA_STEP_BLOCKS * t, MOBA_STEP_BLOCKS)
        return carry

    lax.fori_loop(0, own // MOBA_STEP_BLOCKS, group, 0)
    size = MOBA_STEP_BLOCKS // 2
    while size >= 1:
        @pl.when((own & size) != 0)
        def _(size=size):
            past(own & ~(2 * size - 1), size)
        size //= 2

    for h, cs in enumerate(heads):
        o_ref[:, cs] = _flash_result(acc_ref, h).astype(BF16)


def _moba(q, k, v, kmean):
    L = q.shape[0]
    TQ = MOBA_BLOCK
    nb = L // MOBA_BLOCK
    assert nb <= HEAD_DIM
    km = jnp.pad(kmean, ((0, HEAD_DIM - nb), (0, 0)))
    tile = pl.BlockSpec((TQ, GROUP_WIDTH), lambda i: (i, 0))
    return pl.pallas_call(
        _moba_kernel,
        grid=(L // TQ,),
        in_specs=[tile, _resident((L, GROUP_WIDTH), lambda i: (0, 0)), _resident((L, GROUP_WIDTH), lambda i: (0, 0)),
                  _resident((HEAD_DIM, GROUP_WIDTH), lambda i: (0, 0))],
        out_specs=tile,
        out_shape=jax.ShapeDtypeStruct((L, GROUP_WIDTH), BF16),
        scratch_shapes=[pltpu.VMEM((N_HEADS, TQ, 2 * HEAD_DIM), BF16), pltpu.VMEM((N_HEADS, TQ, HEAD_DIM), F32),
                        pltpu.VMEM((N_HEADS, TQ, 2 * HEAD_DIM), F32)],
        compiler_params=_cparams(("parallel",)),
        name="moba",
    )(q, k, v, km)


def _compress_kernel(c_ref, w1_ref, w2_ref, pos_ref, o_ref):
    n = c_ref.shape[1]
    half = w1_ref.shape[1] // 2
    c = c_ref[0]
    a = jnp.dot(c, w1_ref[0, :half, :], preferred_element_type=F32)
    b = jnp.dot(c, w1_ref[0, half:, :], preferred_element_type=F32)
    bias = jnp.dot(pos_ref[0], w1_ref[0], preferred_element_type=F32)[0:1, :]
    hid = _gelu_tanh(a + pltpu.roll(b, n - 1, 0) + bias)
    out = jnp.dot(hid.astype(BF16), w2_ref[0], preferred_element_type=F32)
    last = lax.broadcasted_iota(jnp.int32, out.shape, 0) == n - 1
    o_ref[0] = jnp.where(last, 0.0, out).astype(BF16)


def _compress(cmp_kv, w1, w2, pos):
    L = cmp_kv.shape[1]
    n = L // NSA_CMP_STRIDE
    wide = NSA_CMP_STRIDE * HEAD_DIM
    chunks = cmp_kv.reshape(2, n, wide)
    hidden = w1.shape[-1]
    return pl.pallas_call(
        _compress_kernel,
        grid=(2,),
        in_specs=[pl.BlockSpec((1, n, wide), lambda j: (j, 0, 0)),
                  pl.BlockSpec((1, 2 * wide, hidden), lambda j: (j, 0, 0)),
                  pl.BlockSpec((1, hidden, HEAD_DIM), lambda j: (j, 0, 0)),
                  pl.BlockSpec((1, 8, 2 * wide), lambda j: (j, 0, 0))],
        out_specs=pl.BlockSpec((1, n, HEAD_DIM), lambda j: (j, 0, 0)),
        out_shape=jax.ShapeDtypeStruct((2, n, HEAD_DIM), BF16),
        compiler_params=_cparams(("parallel",)),
        name="nsa_compress",
    )(chunks, w1, w2, pos)


def _stack_heads(ref):
    return jnp.concatenate([ref[:, h * HEAD_DIM:(h + 1) * HEAD_DIM] for h in range(N_HEADS)], axis=0)


def _nsa_cmp_kernel(q_ref, kc_ref, vc_ref, agg_ref, o_ref, sel_ref):
    TQ = q_ref.shape[0]
    nc = kc_ref.shape[1]
    ns = nc * NSA_CMP_STRIDE // NSA_SEL_BLOCK
    lanes = agg_ref.shape[1]
    start = pl.program_id(0) * TQ
    qs = _stack_heads(q_ref)

    def run(cols, width):
        s = lax.dot_general(qs, kc_ref[0, :cols, :], _NT, preferred_element_type=F32)
        cend = lax.broadcasted_iota(jnp.int32, s.shape, 1) * NSA_CMP_STRIDE + (NSA_CMP_LEN - 1)
        ok = cend <= start + (lax.broadcasted_iota(jnp.int32, s.shape, 0) & (TQ - 1))
        s = jnp.where(ok, s, NEG_INF)
        m = jnp.max(s, axis=1, keepdims=True)
        e = jnp.where(ok, jnp.exp2(s - m), 0.0)
        p = e / jnp.maximum(jnp.sum(e, axis=1, keepdims=True), 1e-30)
        oc = jnp.dot(p.astype(BF16), vc_ref[0, :cols, :], preferred_element_type=F32)
        for h in range(N_HEADS):
            o_ref[:, h * HEAD_DIM:(h + 1) * HEAD_DIM] = oc[h * TQ:(h + 1) * TQ, :]

        imp = p[0:TQ] + p[TQ:2 * TQ] + p[2 * TQ:3 * TQ] + p[3 * TQ:4 * TQ]
        agg = agg_ref[:cols, :width]
        imp_sel = sum(jnp.dot(t, agg, preferred_element_type=F32) for t in _split3(imp))
        blk = lax.broadcasted_iota(jnp.int32, imp_sel.shape, 1)
        own = (start + lax.broadcasted_iota(jnp.int32, imp_sel.shape, 0)) // NSA_SEL_BLOCK
        score = jnp.where(blk == own, FORCE_SCORE, jnp.where(blk < own, imp_sel, NEG_INF))
        sel = _topk_mask(score, min(NSA_TOPK, ns))
        sel_ref[:, :width] = jnp.where(sel > 0.5, 0.0, NEG_INF).astype(BF16)
        if width < lanes:
            sel_ref[:, width:] = jnp.full((TQ, lanes - width), NEG_INF, BF16)

    parts = 4 if nc % (4 * HEAD_DIM) == 0 else 1
    quarter = (start + TQ - 1) // (nc * NSA_CMP_STRIDE // parts)
    for k in range(parts):
        cols = (k + 1) * nc // parts
        width = min(lanes, -(-((k + 1) * ns // parts) // HEAD_DIM) * HEAD_DIM)
        pl.when(quarter == k)(functools.partial(run, cols, width))


def _nsa_cmp(nq, kvc, agg):
    L = nq.shape[0]
    TQ = 2 * Q_BLOCK
    nc = kvc.shape[1]
    lanes = agg.shape[1]
    return pl.pallas_call(
        _nsa_cmp_kernel,
        grid=(L // TQ,),
        in_specs=[pl.BlockSpec((TQ, GROUP_WIDTH), lambda i: (i, 0)),
                  pl.BlockSpec((1, nc, HEAD_DIM), lambda i: (0, 0, 0)),
                  pl.BlockSpec((1, nc, HEAD_DIM), lambda i: (1, 0, 0)),
                  pl.BlockSpec((nc, lanes), lambda i: (0, 0))],
        out_specs=[pl.BlockSpec((TQ, GROUP_WIDTH), lambda i: (i, 0)),
                   pl.BlockSpec((TQ, lanes), lambda i: (i, 0))],
        out_shape=[jax.ShapeDtypeStruct((L, GROUP_WIDTH), F32), jax.ShapeDtypeStruct((L, lanes), BF16)],
        compiler_params=_cparams(("parallel",)),
        name="nsa_cmp",
    )(nq, kvc, kvc, agg)


def _nsa_sel_kernel(q_ref, ks_ref, vs_ref, kw_ref, vw_ref, sb_ref, tag_ref, oc_ref, g_ref, o_ref,
                    qa_ref, m_ref, acc_ref):
    TQ = q_ref.shape[0]
    CK = tag_ref.shape[1]
    cpg = HEAD_DIM // (CK // NSA_SEL_BLOCK)
    R = N_HEADS * TQ
    halves = (slice(0, R // 2), slice(R // 2, R))
    start = pl.program_id(0) * TQ
    qs = _stack_heads(q_ref)
    qpos = start + (lax.broadcasted_iota(jnp.int32, (R, 1), 0) & (TQ - 1))

    bias = sb_ref[...]
    for g in range(sb_ref.shape[1] // HEAD_DIM):
        qa_ref[g] = jnp.concatenate(
            [qs, jnp.concatenate([bias[:, g * HEAD_DIM:(g + 1) * HEAD_DIM]] * N_HEADS, axis=0)], axis=1)

    def scores(c, n):
        rows = pl.ds(pl.multiple_of(c * CK, CK), n * CK)
        tags = tag_ref[pl.ds(c % cpg, n)].reshape(n * CK, HEAD_DIM)
        k_aug = jnp.concatenate([ks_ref[rows, :], tags], axis=1)
        return rows, [lax.dot_general(qa_ref[c // cpg, hs, :], k_aug, _NT, preferred_element_type=F32)
                      for hs in halves]

    c_last = start // CK
    rows, ss = scores(c_last, 1)
    kpos = c_last * CK + lax.broadcasted_iota(jnp.int32, ss[0].shape, 1)
    v_ones = _with_ones(vs_ref[rows, :])
    for idx, hs in enumerate(halves):
        _flash_step(jnp.where(kpos <= qpos[hs], ss[idx], NEG_INF), v_ones, m_ref, acc_ref, idx, True)

    def past(c, n):
        rows, ss = scores(c, n)
        v_ones = _with_ones(vs_ref[rows, :])
        for idx in range(2):
            _flash_step(ss[idx], v_ones, m_ref, acc_ref, idx, False)

    def group(t, carry):
        past(SEL_STEP_CHUNKS * t, SEL_STEP_CHUNKS)
        return carry

    lax.fori_loop(0, c_last // SEL_STEP_CHUNKS, group, 0)
    size = SEL_STEP_CHUNKS // 2
    while size >= 1:
        @pl.when((c_last & size) != 0)
        def _(size=size):
            past(c_last & ~(2 * size - 1), size)
        size //= 2
    o_sel = jnp.concatenate([_flash_result(acc_ref, 0), _flash_result(acc_ref, 1)], axis=0)

    span = TQ + NSA_WINDOW
    w0 = jnp.maximum(start - NSA_WINDOW, 0)
    rows = pl.ds(pl.multiple_of(w0, TQ), span)
    s = lax.dot_general(qs, kw_ref[rows, :], _NT, preferred_element_type=F32)
    wpos = w0 + lax.broadcasted_iota(jnp.int32, s.shape, 1)
    s = jnp.where(jnp.logical_and(wpos <= qpos, wpos > qpos - NSA_WINDOW), s, NEG_INF)
    p = jnp.exp2(s - jnp.max(s, axis=1, keepdims=True))
    o_win = jnp.dot(p.astype(BF16), vw_ref[rows, :], preferred_element_type=F32) / jnp.sum(p, axis=1, keepdims=True)

    g = g_ref[...]
    for h in range(N_HEADS):
        hs = slice(h * TQ, (h + 1) * TQ)
        cs = slice(h * HEAD_DIM, (h + 1) * HEAD_DIM)
        o_ref[:, cs] = (g[:, 3 * h:3 * h + 1] * oc_ref[:, cs] + g[:, 3 * h + 1:3 * h + 2] * o_sel[hs]
                        + g[:, 3 * h + 2:3 * h + 3] * o_win[hs]).astype(BF16)


def _nsa_sel(nq_rope, nkv, sel_bias, tags, o_cmp, gates):
    L = nq_rope.shape[0]
    TQ = Q_BLOCK
    lanes = sel_bias.shape[1]
    R = N_HEADS * TQ
    col = lambda j: _resident((L, HEAD_DIM), lambda i, j=j: (0, j))
    return pl.pallas_call(
        _nsa_sel_kernel,
        grid=(L // TQ,),
        in_specs=[pl.BlockSpec((TQ, GROUP_WIDTH), lambda i: (i, 0)),
                  col(0), col(1), col(2), col(3),
                  pl.BlockSpec((TQ, lanes), lambda i: (i, 0)),
                  _resident(tags.shape, lambda i: (0, 0, 0)),
                  pl.BlockSpec((TQ, GROUP_WIDTH), lambda i: (i, 0)),
                  pl.BlockSpec((TQ, HEAD_DIM), lambda i: (i, 0))],
        out_specs=pl.BlockSpec((TQ, GROUP_WIDTH), lambda i: (i, 0)),
        out_shape=jax.ShapeDtypeStruct((L, GROUP_WIDTH), BF16),
        scratch_shapes=[pltpu.VMEM((lanes // HEAD_DIM, R, 2 * HEAD_DIM), BF16), pltpu.VMEM((2, R // 2, HEAD_DIM), F32),
                        pltpu.VMEM((2, R // 2, 2 * HEAD_DIM), F32)],
        compiler_params=_cparams(("parallel",)),
        name="nsa_sel_win",
    )(nq_rope, nkv, nkv, nkv, nkv, sel_bias, tags, o_cmp, gates)


def _matmul_kernel(a_ref, b_ref, o_ref):
    o_ref[...] = jnp.dot(a_ref[...], b_ref[...], preferred_element_type=F32).astype(o_ref.dtype)


def _mem_proj(mem_b, w):
    n, width = mem_b.shape[0], w.shape[1]
    return pl.pallas_call(
        _matmul_kernel,
        grid=(width // 512,),
        in_specs=[pl.BlockSpec((n, D_MODEL), lambda j: (0, 0)), pl.BlockSpec((D_MODEL, 512), lambda j: (0, j))],
        out_specs=pl.BlockSpec((n, 512), lambda j: (0, j)),
        out_shape=jax.ShapeDtypeStruct((n, width), BF16),
        compiler_params=_cparams(("parallel",)),
        name="mem_proj",
    )(mem_b, w)


def _xattn_kernel(q_ref, k_ref, v_ref, o_ref):
    for h in range(N_HEADS):
        cs = slice(h * HEAD_DIM, (h + 1) * HEAD_DIM)
        s = lax.dot_general(q_ref[:, cs], k_ref[:, cs], _NT, preferred_element_type=F32)
        p = jnp.exp2(s - jnp.max(s, axis=1, keepdims=True))
        o = jnp.dot(p.astype(BF16), v_ref[:, cs], preferred_element_type=F32)
        o_ref[:, cs] = (o / jnp.sum(p, axis=1, keepdims=True)).astype(BF16)


def _xattn(xq, mem_kv):
    L = xq.shape[0]
    n = mem_kv.shape[0]
    row = pl.BlockSpec((ROW_TILE, GROUP_WIDTH), lambda i: (i, 0))
    return pl.pallas_call(
        _xattn_kernel,
        grid=(L // ROW_TILE,),
        in_specs=[row, pl.BlockSpec((n, GROUP_WIDTH), lambda i: (0, 0)),
                  pl.BlockSpec((n, GROUP_WIDTH), lambda i: (0, 1))],
        out_specs=row,
        out_shape=jax.ShapeDtypeStruct((L, GROUP_WIDTH), BF16),
        compiler_params=_cparams(("parallel",)),
        name="xattn",
    )(xq, mem_kv, mem_kv)


def _out_proj_kernel(y0_ref, y1_ref, y2_ref, y3_ref, w_ref, x_ref, g_ref, b_ref, o_ref, ob_ref):
    acc = ALPHA * x_ref[...]
    for j, y_ref in enumerate((y0_ref, y1_ref, y2_ref, y3_ref)):
        acc = acc + jnp.dot(y_ref[...], w_ref[j * GROUP_WIDTH:(j + 1) * GROUP_WIDTH, :], preferred_element_type=F32)
    y = _layer_norm(acc, g_ref[...], b_ref[...])
    o_ref[...] = y
    ob_ref[...] = y.astype(BF16)


def _out_proj(ys, w_o, x, g, b):
    L = x.shape[0]
    tm = ROW_TILE
    part = pl.BlockSpec((tm, GROUP_WIDTH), lambda i: (i, 0))
    row = pl.BlockSpec((tm, D_MODEL), lambda i: (i, 0))
    vec = pl.BlockSpec((1, D_MODEL), lambda i: (0, 0))
    return pl.pallas_call(
        _out_proj_kernel,
        grid=(L // tm,),
        in_specs=[part, part, part, part, _resident((D_MODEL, D_MODEL), lambda i: (0, 0)), row, vec, vec],
        out_specs=[row, row],
        out_shape=[jax.ShapeDtypeStruct((L, D_MODEL), F32), jax.ShapeDtypeStruct((L, D_MODEL), BF16)],
        compiler_params=_cparams(("parallel",)),
        name="out_proj_ln",
    )(*ys, w_o, x, g.reshape(1, -1), b.reshape(1, -1))


def _swiglu_step(x_ref, wg_ref, wu_ref, wd_ref, acc_ref):
    @pl.when(pl.program_id(1) == 0)
    def _():
        acc_ref[...] = jnp.zeros(acc_ref.shape, F32)

    wg, wu, wd = wg_ref[0].astype(BF16), wu_ref[0].astype(BF16), wd_ref[0].astype(BF16)
    for r in range(0, x_ref.shape[0], FF_ROW_CHUNK):
        rows = slice(r, r + FF_ROW_CHUNK)
        x = x_ref[rows, :]
        gate = jnp.dot(x, wg, preferred_element_type=F32)
        up = jnp.dot(x, wu, preferred_element_type=F32)
        hid = (gate * _sigmoid(gate) * up).astype(BF16)
        acc_ref[rows, :] += jnp.dot(hid, wd, preferred_element_type=F32)


def _ffn_dense_kernel(x_ref, wg_ref, wu_ref, wd_ref, xr_ref, g_ref, b_ref, o_ref, ob_ref, acc_ref):
    _swiglu_step(x_ref, wg_ref, wu_ref, wd_ref, acc_ref)

    @pl.when(pl.program_id(1) == pl.num_programs(1) - 1)
    def _():
        y = _layer_norm(ALPHA * xr_ref[...] + acc_ref[...], g_ref[...], b_ref[...])
        o_ref[...] = y
        ob_ref[...] = y.astype(BF16)


def _ffn_dense(xb, x, wg, wu, wd, g, b):
    L = x.shape[0]
    tm, tf = ROW_TILE, FF_TILE
    row = pl.BlockSpec((tm, D_MODEL), lambda i, f: (i, 0))
    vec = pl.BlockSpec((1, D_MODEL), lambda i, f: (0, 0))
    return pl.pallas_call(
        _ffn_dense_kernel,
        grid=(L // tm, D_FF // tf),
        in_specs=[row,
                  pl.BlockSpec((1, D_MODEL, tf), lambda i, f: (0, 0, f)),
                  pl.BlockSpec((1, D_MODEL, tf), lambda i, f: (0, 0, f)),
                  pl.BlockSpec((1, tf, D_MODEL), lambda i, f: (0, f, 0)),
                  row, vec, vec],
        out_specs=[row, row],
        out_shape=[jax.ShapeDtypeStruct((L, D_MODEL), F32), jax.ShapeDtypeStruct((L, D_MODEL), BF16)],
        scratch_shapes=[pltpu.VMEM((tm, D_MODEL), F32)],
        compiler_params=_cparams(("parallel", "arbitrary")),
        name="ffn_dense_ln",
    )(xb, wg[None], wu[None], wd[None], x, g.reshape(1, -1), b.reshape(1, -1))


def _start_row_gather(idx_ref, base, n, src_hbm, dst_ref, sem):
    def body(r, carry):
        pltpu.make_async_copy(src_hbm.at[pl.ds(idx_ref[base + r], 1)], dst_ref.at[pl.ds(r, 1)], sem).start()
        return carry

    lax.fori_loop(0, n, body, 0, unroll=8)


def _wait_row_gather(n, src_hbm, dst_ref, sem):
    pltpu.make_async_copy(src_hbm.at[pl.ds(0, n)], dst_ref.at[pl.ds(0, n)], sem).wait()


def _ffn_moe_kernel(exp_ref, used_ref, tok_ref, x_hbm, wg_ref, wu_ref, wd_ref, o_ref, stage_ref, xb_ref, sem):
    b, f = pl.program_id(0), pl.program_id(1)
    tm = stage_ref.shape[0]
    live = b < used_ref[0]

    @pl.when(jnp.logical_and(live, f == 0))
    def _():
        @pl.when(b == 0)
        def _():
            _start_row_gather(tok_ref, 0, tm, x_hbm, stage_ref, sem)

        _wait_row_gather(tm, x_hbm, stage_ref, sem)
        xb_ref[...] = stage_ref[...].astype(BF16)

        @pl.when(b + 1 < used_ref[0])
        def _():
            _start_row_gather(tok_ref, (b + 1) * tm, tm, x_hbm, stage_ref, sem)

    @pl.when(live)
    def _():
        _swiglu_step(xb_ref, wg_ref, wu_ref, wd_ref, o_ref)

    @pl.when(jnp.logical_and(jnp.logical_not(live), f == 0))
    def _():
        o_ref[...] = jnp.zeros(o_ref.shape, F32)


def _ffn_moe(x, buf_tok, blk_exp, n_used, wg, wu, wd, layer):
    P = buf_tok.shape[0]
    tm, tf = MOE_ROWS, MOE_FF_TILE
    nf = D_FF // tf
    ff = lambda b, f, used: jnp.where(b < used[0], f, nf - 1)
    grid_spec = pltpu.PrefetchScalarGridSpec(
        num_scalar_prefetch=3,
        grid=(P // tm, nf),
        in_specs=[pl.BlockSpec(memory_space=pl.ANY),
                  pl.BlockSpec((None, 1, D_MODEL, tf), lambda b, f, e, u, t: (layer, e[b], 0, ff(b, f, u))),
                  pl.BlockSpec((None, 1, D_MODEL, tf), lambda b, f, e, u, t: (layer, e[b], 0, ff(b, f, u))),
                  pl.BlockSpec((None, 1, tf, D_MODEL), lambda b, f, e, u, t: (layer, e[b], ff(b, f, u), 0))],
        out_specs=pl.BlockSpec((tm, D_MODEL), lambda b, f, e, u, t: (b, 0)),
        scratch_shapes=[pltpu.VMEM((tm, D_MODEL), F32), pltpu.VMEM((tm, D_MODEL), BF16),
                        pltpu.SemaphoreType.DMA(())],
    )
    return pl.pallas_call(
        _ffn_moe_kernel,
        grid_spec=grid_spec,
        out_shape=jax.ShapeDtypeStruct((P, D_MODEL), F32),
        compiler_params=_cparams(("arbitrary", "arbitrary"), 56),
        name="ffn_moe",
    )(blk_exp, n_used, buf_tok, x, wg, wu, wd)


def _router_kernel(x_ref, r_ref, o_ref):
    xs = _split3(x_ref[...])
    rs = (r_ref[0], r_ref[1], r_ref[2])
    logits = None
    for i, j in ((0, 0), (0, 1), (1, 0), (1, 1), (0, 2), (2, 0)):
        t = jnp.dot(xs[i], rs[j], preferred_element_type=F32)
        logits = t if logits is None else logits + t
    lane = lax.broadcasted_iota(jnp.int32, logits.shape, 1)
    lane_f = lane.astype(F32)
    logits = jnp.where(lane < N_EXPERTS, logits, NEG_INF)
    m1 = jnp.max(logits, axis=1, keepdims=True)
    i1 = jnp.min(jnp.where(logits == m1, lane_f, 128.0), axis=1, keepdims=True)
    rest = jnp.where(lane_f == i1, NEG_INF, logits)
    m2 = jnp.max(rest, axis=1, keepdims=True)
    i2 = jnp.min(jnp.where(rest == m2, lane_f, 128.0), axis=1, keepdims=True)
    e2 = jnp.exp(m2 - m1)
    g1 = 1.0 / (1.0 + e2)
    g2 = e2 / (1.0 + e2)
    o_ref[...] = jnp.where(lane == 0, i1, jnp.where(lane == 1, i2, jnp.where(lane == 2, g1, jnp.where(lane == 3, g2, 0.0))))


def _router(x, router3):
    L = x.shape[0]
    return pl.pallas_call(
        _router_kernel,
        grid=(L // ROW_TILE,),
        in_specs=[pl.BlockSpec((ROW_TILE, D_MODEL), lambda i: (i, 0)),
                  pl.BlockSpec((3, D_MODEL, 128), lambda i: (0, 0, 0))],
        out_specs=pl.BlockSpec((ROW_TILE, 128), lambda i: (i, 0)),
        out_shape=jax.ShapeDtypeStruct((L, 128), F32),
        compiler_params=_cparams(("parallel",)),
        name="moe_router",
    )(x, router3)


def _combine_kernel(da_ref, db_ref, x_ref, gw_ref, g_ref, b_ref, y_hbm, o_ref, ob_ref, ya_ref, yb_ref, sem):
    i, n = pl.program_id(0), pl.num_programs(0)
    tm = x_ref.shape[0]
    slot = i % 2

    def start(step, slot):
        _start_row_gather(da_ref, step * tm, tm, y_hbm, ya_ref.at[slot], sem.at[slot])
        _start_row_gather(db_ref, step * tm, tm, y_hbm, yb_ref.at[slot], sem.at[slot])

    @pl.when(i == 0)
    def _():
        start(0, 0)

    @pl.when(i + 1 < n)
    def _():
        start(i + 1, 1 - slot)

    _wait_row_gather(tm, y_hbm, ya_ref.at[slot], sem.at[slot])
    _wait_row_gather(tm, y_hbm, yb_ref.at[slot], sem.at[slot])
    gw = gw_ref[...]
    f = gw[:, 2:3] * ya_ref[slot] + gw[:, 3:4] * yb_ref[slot]
    y = _layer_norm(ALPHA * x_ref[...] + f, g_ref[...], b_ref[...])
    o_ref[...] = y
    ob_ref[...] = y.astype(BF16)


def _combine(x, y_rows, dest_a, dest_b, route, g, b):
    L = x.shape[0]
    tm = COMBINE_ROWS
    row = pl.BlockSpec((tm, D_MODEL), lambda i, a, b: (i, 0))
    vec = pl.BlockSpec((1, D_MODEL), lambda i, a, b: (0, 0))
    grid_spec = pltpu.PrefetchScalarGridSpec(
        num_scalar_prefetch=2,
        grid=(L // tm,),
        in_specs=[row, pl.BlockSpec((tm, 128), lambda i, a, b: (i, 0)), vec, vec, pl.BlockSpec(memory_space=pl.ANY)],
        out_specs=[row, row],
        scratch_shapes=[pltpu.VMEM((2, tm, D_MODEL), F32), pltpu.VMEM((2, tm, D_MODEL), F32),
                        pltpu.SemaphoreType.DMA((2,))],
    )
    return pl.pallas_call(
        _combine_kernel,
        grid_spec=grid_spec,
        out_shape=[jax.ShapeDtypeStruct((L, D_MODEL), F32), jax.ShapeDtypeStruct((L, D_MODEL), BF16)],
        compiler_params=_cparams(("arbitrary",)),
        name="moe_combine_ln",
    )(dest_a, dest_b, x, route, g.reshape(1, -1), b.reshape(1, -1), y_rows)


def _moe(x, router, wg, wu, wd, layer, g, b):
    T = x.shape[0]
    r32 = jnp.pad(router, ((0, 0), (0, 128 - N_EXPERTS)))
    r1 = r32.astype(BF16)
    r2 = (r32 - r1.astype(F32)).astype(BF16)
    r3 = (r32 - r1.astype(F32) - r2.astype(F32)).astype(BF16)
    route = _router(x, jnp.stack([r1, r2, r3]))
    e_flat = route[:, :TOP_K].astype(jnp.int32).reshape(-1)
    onehot = (e_flat[:, None] == jnp.arange(N_EXPERTS, dtype=jnp.int32)[None, :]).astype(jnp.int32)
    csum = jnp.cumsum(onehot, axis=0)
    rank = jnp.sum(onehot * csum, axis=1) - 1
    counts = csum[-1]
    padded = (counts + MOE_ROWS - 1) // MOE_ROWS * MOE_ROWS
    pend = jnp.cumsum(padded)
    dest = (pend - padded)[e_flat] + rank
    n_blocks = -(-(T * TOP_K + N_EXPERTS * (MOE_ROWS - 1)) // MOE_ROWS)
    P = n_blocks * MOE_ROWS
    tok = jnp.repeat(jnp.arange(T, dtype=jnp.int32), TOP_K)
    buf_tok = jnp.zeros((P,), jnp.int32).at[dest].set(tok)
    n_used = (pend[-1] // MOE_ROWS).astype(jnp.int32).reshape(1)
    blk_start = jnp.arange(n_blocks, dtype=jnp.int32) * MOE_ROWS
    blk_exp = jnp.minimum(jnp.searchsorted(pend, jnp.minimum(blk_start, pend[-1] - 1), side='right'),
                          N_EXPERTS - 1).astype(jnp.int32)
    y_rows = _ffn_moe(x, buf_tok, blk_exp, n_used, wg, wu, wd, layer)
    d2 = dest.reshape(T, TOP_K)
    return _combine(x, y_rows, d2[:, 0], d2[:, 1], route, g, b)


def _rope_tables(L):
    inv_freq = 1.0 / (ROPE_THETA ** (jnp.arange(0, ROPE_DIM, 2, dtype=F32) / ROPE_DIM))
    ang = jnp.arange(L, dtype=F32)[:, None] * inv_freq[None, :]
    cos, sin = jnp.cos(ang), jnp.sin(ang)
    z = lambda n: jnp.zeros((L, n), F32)
    cosf = jnp.concatenate([cos, cos, jnp.ones((L, HEAD_DIM - ROPE_DIM), F32)], axis=1)
    sa = jnp.concatenate([-sin, z(HEAD_DIM - 16)], axis=1)
    sb = jnp.concatenate([z(16), sin, z(HEAD_DIM - ROPE_DIM)], axis=1)
    return cosf, sa, sb


def _w_in_pack_kernel(a_ref, b_ref, c_ref, o_ref):
    o_ref[:, :3328] = a_ref[...].astype(BF16)
    o_ref[:, 3328:3840] = b_ref[...].astype(BF16)
    o_ref[:, 3840:] = c_ref[...].astype(BF16)


def _reorder_w_in(w):
    gates = jnp.pad(w[:, 3328:3340], ((0, 0), (0, 128 - 12)))
    rows = 256
    blk = lambda n: pl.BlockSpec((rows, n), lambda i: (i, 0))
    return pl.pallas_call(
        _w_in_pack_kernel,
        grid=(D_MODEL // rows,),
        in_specs=[blk(3328), blk(512), blk(128)],
        out_specs=blk(PROJ_WIDTH),
        out_shape=jax.ShapeDtypeStruct((D_MODEL, PROJ_WIDTH), BF16),
        compiler_params=_cparams(("parallel",)),
        name="w_in_pack",
    )(w[:, :3328], w[:, 3340:3852], gates)


def _trunk(x, mem, p):
    L = x.shape[0]
    ns, nc = L // NSA_SEL_BLOCK, L // NSA_CMP_STRIDE
    cosf, sa, sb = _rope_tables(L)
    lanes = -(-ns // HEAD_DIM) * HEAD_DIM
    cj = jnp.arange(nc)[:, None] - 4 * jnp.arange(lanes)[None, :]
    agg = jnp.logical_and(cj >= -1, cj <= 3).astype(BF16)
    ck = min(SEL_CHUNK, L)
    per = ck // NSA_SEL_BLOCK
    n_tag = min(HEAD_DIM // per, L // ck)
    blk_id = jnp.arange(n_tag)[:, None, None] * per + jnp.arange(ck)[None, :, None] // NSA_SEL_BLOCK
    tags = (blk_id == jnp.arange(HEAD_DIM)[None, None, :]).astype(BF16)
    mem_b = mem.astype(BF16)

    x, xb = _ln_in(x, p['ln_in_g'], p['ln_in_b'])
    for i in range(DEPTH):
        u, mq, mk, mv, nqc, nqr, cmp_kv, nkv, xq, gates = _proj(xb, _reorder_w_in(p['w_in'][i]), cosf, sa, sb)
        bmat, cmat, pw = _s5_params(p['ssm_a_re'][i], p['ssm_a_im'][i], p['ssm_log_dt'][i], p['ssm_b_re'][i],
                                    p['ssm_b_im'][i], p['ssm_c_re'][i], p['ssm_c_im'][i])
        y_ssm = _s5(u, bmat, cmat, pw, p['ssm_d'][i], p['ssm_w_glu'][i])
        y_moba = _moba(mq, mk, mv, _kmean(mk))
        pos = jnp.stack([p['nsa_pos_k'][i].reshape(1, -1), p['nsa_pos_v'][i].reshape(1, -1)])
        pos = jnp.broadcast_to(pos, (2, 8, pos.shape[-1])).astype(BF16)
        kvc = _compress(cmp_kv, jnp.stack([p['nsa_ck1'][i], p['nsa_cv1'][i]]).astype(BF16),
                        jnp.stack([p['nsa_ck2'][i], p['nsa_cv2'][i]]).astype(BF16), pos)
        o_cmp, sel_bias = _nsa_cmp(nqc, kvc, agg)
        y_nsa = _nsa_sel(nqr, nkv, sel_bias, tags, o_cmp, gates)
        mem_kv = _mem_proj(mem_b, jnp.concatenate([p['mem_wk'][i], p['mem_wv'][i]], axis=1).astype(BF16))
        y_x = _xattn(xq, mem_kv)
        x, xb = _out_proj((y_ssm, y_moba, y_nsa, y_x), p['w_o'][i].astype(BF16), x, p['ln1_g'][i], p['ln1_b'][i])
        j = i // 2
        if i % 2 == 0:
            x, xb = _ffn_dense(xb, x, p['ffn_w_gate'][j].astype(BF16), p['ffn_w_up'][j].astype(BF16),
                               p['ffn_w_down'][j].astype(BF16), p['ln2_g'][i], p['ln2_b'][i])
        else:
            x, xb = _moe(x, p['moe_router'][j], p['moe_w_gate'], p['moe_w_up'], p['moe_w_down'], j,
                         p['ln2_g'][i], p['ln2_b'][i])
    return x


def kernel(x, mem, ln_in_g, ln_in_b, w_in, ssm_a_re, ssm_a_im, ssm_log_dt, ssm_b_re, ssm_b_im, ssm_c_re, ssm_c_im,
           ssm_d, ssm_w_glu, nsa_pos_k, nsa_pos_v, nsa_ck1, nsa_ck2, nsa_cv1, nsa_cv2, mem_wk, mem_wv, w_o, ln1_g,
           ln1_b, ln2_g, ln2_b, ffn_w_gate, ffn_w_up, ffn_w_down, moe_router, moe_w_gate, moe_w_up, moe_w_down):
    p = dict(ln_in_g=ln_in_g, ln_in_b=ln_in_b, w_in=w_in, ssm_a_re=ssm_a_re, ssm_a_im=ssm_a_im,
             ssm_log_dt=ssm_log_dt, ssm_b_re=ssm_b_re, ssm_b_im=ssm_b_im, ssm_c_re=ssm_c_re, ssm_c_im=ssm_c_im,
             ssm_d=ssm_d, ssm_w_glu=ssm_w_glu, nsa_pos_k=nsa_pos_k, nsa_pos_v=nsa_pos_v, nsa_ck1=nsa_ck1,
             nsa_ck2=nsa_ck2, nsa_cv1=nsa_cv1, nsa_cv2=nsa_cv2, mem_wk=mem_wk, mem_wv=mem_wv, w_o=w_o,
             ln1_g=ln1_g, ln1_b=ln1_b, ln2_g=ln2_g, ln2_b=ln2_b, ffn_w_gate=ffn_w_gate, ffn_w_up=ffn_w_up,
             ffn_w_down=ffn_w_down, moe_router=moe_router, moe_w_gate=moe_w_gate, moe_w_up=moe_w_up,
             moe_w_down=moe_w_down)
    assert x.shape[0] == 1 and mem.shape[0] == 1
    return _trunk(x[0], mem[0], p)[None]
```

```python
import functools
import math

import jax
import jax.numpy as jnp
from jax import lax
from jax.experimental import pallas as pl
from jax.experimental.pallas import tpu as pltpu

F32 = jnp.float32
BF16 = jnp.bfloat16

D_MODEL = 2048
DEPTH = 4
HEAD_DIM = 128
GROUP_WIDTH = 512
N_HEADS = 4
SSM_CH = 512
SSM_GROUP = 16
SSM_NG = 32
SSM_STATE = 64
SSM_NSTATE = SSM_NG * SSM_STATE
MOBA_BLOCK = 256
MOBA_TOPK = 3
NSA_CMP_LEN = 32
NSA_CMP_STRIDE = 16
NSA_SEL_BLOCK = 64
NSA_TOPK = 16
NSA_WINDOW = 512
ROPE_THETA = 500000.0
ROPE_DIM = 32
D_FF = 5632
N_EXPERTS = 8
TOP_K = 2
Q_BLOCK = 128
LN_EPS = 1e-5
NEG_INF = -1e30
FORCE_SCORE = 1e9
ALPHA = (2.0 * DEPTH) ** 0.25
SCALE = HEAD_DIM ** -0.5 * math.log2(math.e)

ROW_TILE = 512
SCAN_ROWS = 512
SCAN_LANES = 512
SEL_CHUNK = 512
SEL_STEP_CHUNKS = 4
MOBA_STEP_BLOCKS = 8
FF_TILE = 512
FF_ROW_CHUNK = 256
MOE_ROWS = 1056
MOE_ROW_CHUNK = 352
MOE_FF_TILE = 256
COMBINE_ROWS = 256
PROJ_WIDTH = 3968

_NT = (((1,), (1,)), ((), ()))


def _cparams(sem, vmem_mb=48):
    return pltpu.CompilerParams(dimension_semantics=sem, vmem_limit_bytes=vmem_mb * 1024 * 1024)


def _resident(block_shape, index_map):
    return pl.BlockSpec(block_shape, index_map, pipeline_mode=pl.Buffered(1))


def _layer_norm(r, g, b):
    mu = jnp.mean(r, axis=-1, keepdims=True)
    d = r - mu
    var = jnp.mean(d * d, axis=-1, keepdims=True)
    return d * lax.rsqrt(var + LN_EPS) * g + b


def _gelu_tanh(x):
    return x * (0.5 * (1.0 + jnp.tanh(math.sqrt(2.0 / math.pi) * (x + 0.044715 * (x * x * x)))))


def _sigmoid(x):
    return 1.0 / (1.0 + jnp.exp(-x))


def _split3(a):
    a1 = a.astype(BF16)
    r1 = a - a1.astype(F32)
    a2 = r1.astype(BF16)
    a3 = (r1 - a2.astype(F32)).astype(BF16)
    return a1, a2, a3


def _topk_mask(score, k):
    ids = lax.broadcasted_iota(jnp.int32, score.shape, 1).astype(F32)

    def body(_, sc):
        m = jnp.max(sc, axis=1, keepdims=True)
        idx = jnp.min(jnp.where(sc == m, ids, float(score.shape[1])), axis=1, keepdims=True)
        return jnp.where(ids == idx, NEG_INF, sc)

    left = lax.fori_loop(0, k, body, score)
    return jnp.where(jnp.logical_and(score > 0.5 * NEG_INF, left <= 0.5 * NEG_INF), 1.0, 0.0)


def _ln_in_kernel(x_ref, g_ref, b_ref, o_ref, ob_ref):
    y = _layer_norm(x_ref[...], g_ref[...], b_ref[...])
    o_ref[...] = y
    ob_ref[...] = y.astype(BF16)


def _ln_in(x, g, b):
    L = x.shape[0]
    row = pl.BlockSpec((ROW_TILE, D_MODEL), lambda i: (i, 0))
    vec = pl.BlockSpec((1, D_MODEL), lambda i: (0, 0))
    return pl.pallas_call(
        _ln_in_kernel,
        grid=(L // ROW_TILE,),
        in_specs=[row, vec, vec],
        out_specs=[row, row],
        out_shape=[jax.ShapeDtypeStruct((L, D_MODEL), F32), jax.ShapeDtypeStruct((L, D_MODEL), BF16)],
        compiler_params=_cparams(("parallel",)),
        name="ln_in",
    )(x, g.reshape(1, -1), b.reshape(1, -1))


def _proj_kernel(x_ref, w_ref, cos_ref, sa_ref, sb_ref,
                 u_ref, mq_ref, mk_ref, mv_ref, nqc_ref, nqr_ref, cmp_ref, nkv_ref, xq_ref, g_ref):
    x = x_ref[...]
    cos, sa, sb = cos_ref[...], sa_ref[...], sb_ref[...]

    def dot(c0, n):
        return jnp.dot(x, w_ref[:, c0:c0 + n], preferred_element_type=F32)

    def rope(t):
        return t * cos + pltpu.roll(t, HEAD_DIM - 16, 1) * sa + pltpu.roll(t, 16, 1) * sb

    heads = [slice(h * HEAD_DIM, (h + 1) * HEAD_DIM) for h in range(N_HEADS)]
    u_ref[...] = dot(0, 512)
    t = dot(512, 512)
    for cs in heads:
        mq_ref[:, cs] = (rope(t[:, cs]) * SCALE).astype(BF16)
    t = dot(1024, 512)
    for cs in heads:
        mk_ref[:, cs] = rope(t[:, cs]).astype(BF16)
    mv_ref[...] = dot(1536, 512).astype(BF16)
    t = dot(2048, 512)
    nqc_ref[...] = (t * SCALE).astype(BF16)
    for cs in heads:
        nqr_ref[:, cs] = (rope(t[:, cs]) * SCALE).astype(BF16)
    t = dot(2560, 512)
    cmp_ref[0] = t[:, heads[0]].astype(BF16)
    cmp_ref[1] = t[:, heads[1]].astype(BF16)
    nkv_ref[:, heads[0]] = rope(t[:, heads[2]]).astype(BF16)
    nkv_ref[:, heads[1]] = t[:, heads[3]].astype(BF16)
    t = dot(3072, 256)
    nkv_ref[:, heads[2]] = rope(t[:, heads[0]]).astype(BF16)
    nkv_ref[:, heads[3]] = t[:, heads[1]].astype(BF16)
    t = dot(3328, 640)
    xq_ref[...] = (t[:, :512] * SCALE).astype(BF16)
    g_ref[...] = _sigmoid(t[:, 512:])


def _proj(xb, w, cosf, sa, sb):
    L = xb.shape[0]
    tm = ROW_TILE
    row = lambda n: pl.BlockSpec((tm, n), lambda i: (i, 0))
    outs = [
        (jax.ShapeDtypeStruct((L, 512), F32), row(512)),
        (jax.ShapeDtypeStruct((L, 512), BF16), row(512)),
        (jax.ShapeDtypeStruct((L, 512), BF16), row(512)),
        (jax.ShapeDtypeStruct((L, 512), BF16), row(512)),
        (jax.ShapeDtypeStruct((L, 512), BF16), row(512)),
        (jax.ShapeDtypeStruct((L, 512), BF16), row(512)),
        (jax.ShapeDtypeStruct((2, L, 128), BF16), pl.BlockSpec((2, tm, 128), lambda i: (0, i, 0))),
        (jax.ShapeDtypeStruct((L, 512), BF16), row(512)),
        (jax.ShapeDtypeStruct((L, 512), BF16), row(512)),
        (jax.ShapeDtypeStruct((L, 128), F32), row(128)),
    ]
    return pl.pallas_call(
        _proj_kernel,
        grid=(L // tm,),
        in_specs=[row(D_MODEL), _resident((D_MODEL, PROJ_WIDTH), lambda i: (0, 0)),
                  row(128), row(128), row(128)],
        out_specs=[o[1] for o in outs],
        out_shape=[o[0] for o in outs],
        compiler_params=_cparams(("parallel",), 56),
        name="in_proj",
    )(xb, w, cosf, sa, sb)


def _s5_kernel(u_ref, bmat_ref, cmat_ref, pw_ref, d_ref, wglu_ref, o_ref, hr_ref, hi_ref, car_ref, cai_ref):
    T = u_ref.shape[0]
    W = SCAN_LANES

    @pl.when(pl.program_id(0) == 0)
    def _():
        car_ref[...] = jnp.zeros(car_ref.shape, F32)
        cai_ref[...] = jnp.zeros(cai_ref.shape, F32)

    u = u_ref[...]
    ub = u.astype(BF16)
    for b in range(4):
        bu = jnp.dot(ub[:, b * 128:(b + 1) * 128], bmat_ref[b], preferred_element_type=F32)
        hr_ref[:, b * 512:(b + 1) * 512] = bu[:, :512]
        hi_ref[:, b * 512:(b + 1) * 512] = bu[:, 512:]

    for cb in range(SSM_NSTATE // W):
        sl = slice(cb * W, (cb + 1) * W)
        steps = [(pw_ref[2 * j, :, sl], pw_ref[2 * j + 1, :, sl], 1 << j) for j in range(3)]
        p_r, p_i = pw_ref[6, :, sl], pw_ref[7, :, sl]

        def tile(t, carry, sl=sl, steps=steps, p_r=p_r, p_i=p_i):
            c_r, c_i = carry
            rows = pl.ds(pl.multiple_of(t * 8, 8), 8)
            x_r, x_i = hr_ref[rows, sl], hi_ref[rows, sl]
            for l_r, l_i, d in steps:
                s_r, s_i = pltpu.roll(x_r, d, 0), pltpu.roll(x_i, d, 0)
                x_r, x_i = x_r + l_r * s_r - l_i * s_i, x_i + l_r * s_i + l_i * s_r
            x_r, x_i = x_r + p_r * c_r - p_i * c_i, x_i + p_r * c_i + p_i * c_r
            hr_ref[rows, sl] = x_r
            hi_ref[rows, sl] = x_i
            return (jnp.broadcast_to(x_r[7:8, :], (8, W)), jnp.broadcast_to(x_i[7:8, :], (8, W)))

        c_r, c_i = lax.fori_loop(0, T // 8, tile, (car_ref[:, sl], cai_ref[:, sl]))
        car_ref[:, sl] = c_r
        cai_ref[:, sl] = c_i

    ys = []
    for b in range(4):
        h_r = hr_ref[:, b * 512:(b + 1) * 512].astype(BF16)
        h_i = hi_ref[:, b * 512:(b + 1) * 512].astype(BF16)
        ys.append(jnp.dot(h_r, cmat_ref[b, :512, :], preferred_element_type=F32)
                  + jnp.dot(h_i, cmat_ref[b, 512:, :], preferred_element_type=F32))
    y = jnp.concatenate(ys, axis=1) + d_ref[...] * u
    z = _gelu_tanh(y)
    gate = _sigmoid(jnp.dot(z.astype(BF16), wglu_ref[...], preferred_element_type=F32))
    o_ref[...] = (z * gate).astype(BF16)


def _s5_params(a_re, a_im, log_dt, b_re, b_im, c_re, c_im):
    dt = jnp.exp(log_dt)[:, None]
    mag = jnp.exp(a_re * dt)
    lam_re, lam_im = mag * jnp.cos(a_im * dt), mag * jnp.sin(a_im * dt)
    den = a_re * a_re + a_im * a_im
    nr, ni = lam_re - 1.0, lam_im
    coef_re = (nr * a_re + ni * a_im) / den
    coef_im = (ni * a_re - nr * a_im) / den
    bbar_re = coef_re[..., None] * b_re - coef_im[..., None] * b_im
    bbar_im = coef_re[..., None] * b_im + coef_im[..., None] * b_re
    eye = jnp.eye(8, dtype=F32)
    to_b = lambda m: jnp.einsum('bgnc,gh->bgchn', m.reshape(4, 8, SSM_STATE, SSM_GROUP), eye).reshape(4, 128, 512)
    bmat = jnp.concatenate([to_b(bbar_re), to_b(bbar_im)], axis=-1).astype(BF16)
    to_c = lambda m: jnp.einsum('bgcn,gh->bgnhc', m.reshape(4, 8, SSM_GROUP, SSM_STATE), eye).reshape(4, 512, 128)
    cmat = jnp.concatenate([to_c(c_re), -to_c(c_im)], axis=1).astype(BF16)
    lr, li = lam_re.reshape(-1), lam_im.reshape(-1)
    pr, pi = [lr], [li]
    for _ in range(7):
        pr, pi = pr + [pr[-1] * lr - pi[-1] * li], pi + [pr[-1] * li + pi[-1] * lr]
    row = jnp.arange(8)[:, None]
    tabs = []
    for d in (1, 2, 4):
        tabs += [jnp.where(row >= d, pr[d - 1][None, :], 0.0), jnp.where(row >= d, pi[d - 1][None, :], 0.0)]
    tabs += [jnp.stack(pr, axis=0), jnp.stack(pi, axis=0)]
    return bmat, cmat, jnp.stack(tabs, axis=0)


def _s5(u, bmat, cmat, pw, d_skip, w_glu):
    L = u.shape[0]
    T = SCAN_ROWS
    row = pl.BlockSpec((T, 512), lambda i: (i, 0))
    full = lambda shape: pl.BlockSpec(shape, lambda i: (0,) * len(shape))
    return pl.pallas_call(
        _s5_kernel,
        grid=(L // T,),
        in_specs=[row, full((4, 128, 1024)), full((4, 1024, 128)), full((8, 8, SSM_NSTATE)),
                  full((1, 512)), full((512, 512))],
        out_specs=row,
        out_shape=jax.ShapeDtypeStruct((L, 512), BF16),
        scratch_shapes=[pltpu.VMEM((T, SSM_NSTATE), F32), pltpu.VMEM((T, SSM_NSTATE), F32),
                        pltpu.VMEM((8, SSM_NSTATE), F32), pltpu.VMEM((8, SSM_NSTATE), F32)],
        compiler_params=_cparams(("arbitrary",)),
        name="s5",
    )(u, bmat, cmat, pw, d_skip.reshape(1, -1), w_glu.astype(BF16))


def _kmean_kernel(k_ref, o_ref):
    rows = k_ref.shape[0]
    k = k_ref[...].astype(F32).reshape(rows // MOBA_BLOCK, MOBA_BLOCK, GROUP_WIDTH)
    o_ref[...] = jnp.mean(k, axis=1).astype(BF16)


def _kmean(k):
    L = k.shape[0]
    rows = 8 * MOBA_BLOCK if L % (8 * MOBA_BLOCK) == 0 else L
    return pl.pallas_call(
        _kmean_kernel,
        grid=(L // rows,),
        in_specs=[pl.BlockSpec((rows, GROUP_WIDTH), lambda i: (i, 0))],
        out_specs=pl.BlockSpec((rows // MOBA_BLOCK, GROUP_WIDTH), lambda i: (i, 0)),
        out_shape=jax.ShapeDtypeStruct((L // MOBA_BLOCK, GROUP_WIDTH), BF16),
        compiler_params=_cparams(("parallel",)),
        name="moba_kmean",
    )(k)


def _with_ones(v):
    return jnp.concatenate([v, jnp.ones(v.shape, v.dtype)], axis=1)


def _flash_step(s, v_ones, m_ref, acc_ref, idx, first):
    row_max = jnp.max(s, axis=1, keepdims=True)
    if first:
        m_new = jnp.broadcast_to(row_max, (s.shape[0], HEAD_DIM))
    else:
        m_old = m_ref[idx]
        m_new = jnp.maximum(m_old, row_max)
    p = jnp.concatenate([jnp.exp2(s[:, c:c + HEAD_DIM] - m_new) for c in range(0, s.shape[1], HEAD_DIM)], axis=1)
    pv = jnp.dot(p.astype(BF16), v_ones, preferred_element_type=F32)
    if first:
        acc_ref[idx] = pv
    else:
        a = jnp.exp2(m_old - m_new)
        acc_ref[idx] = jnp.concatenate([a, a], axis=1) * acc_ref[idx] + pv
    m_ref[idx] = m_new


def _flash_result(acc_ref, idx):
    acc = acc_ref[idx]
    return acc[:, :HEAD_DIM] / acc[:, HEAD_DIM:]


def _moba_kernel(q_ref, k_ref, v_ref, km_ref, o_ref, qa_ref, m_ref, acc_ref):
    TQ = q_ref.shape[0]
    own = pl.program_id(0)
    heads = [slice(h * HEAD_DIM, (h + 1) * HEAD_DIM) for h in range(N_HEADS)]

    gs = jnp.concatenate([lax.dot_general(q_ref[:, cs], km_ref[:, cs], _NT, preferred_element_type=F32)
                          for cs in heads], axis=0)
    blk = lax.broadcasted_iota(jnp.int32, gs.shape, 1)
    sel = _topk_mask(jnp.where(blk < own, gs, NEG_INF), MOBA_TOPK)
    bias = jnp.where(sel > 0.5, 0.0, NEG_INF).astype(BF16)
    for h, cs in enumerate(heads):
        qa_ref[h] = jnp.concatenate([q_ref[:, cs], bias[h * TQ:(h + 1) * TQ]], axis=1)

    own_rows = pl.ds(pl.multiple_of(own * MOBA_BLOCK, MOBA_BLOCK), MOBA_BLOCK)
    causal = (lax.broadcasted_iota(jnp.int32, (TQ, MOBA_BLOCK), 1)
              <= lax.broadcasted_iota(jnp.int32, (TQ, MOBA_BLOCK), 0))
    for h, cs in enumerate(heads):
        s = lax.dot_general(q_ref[:, cs], k_ref[own_rows, cs], _NT, preferred_element_type=F32)
        _flash_step(jnp.where(causal, s, NEG_INF), _with_ones(v_ref[own_rows, cs]), m_ref, acc_ref, h, True)

    def past(first_block, n_blocks):
        n = n_blocks * MOBA_BLOCK
        rows = pl.ds(pl.multiple_of(first_block * MOBA_BLOCK, MOBA_BLOCK), n)
        blk = first_block + lax.broadcasted_iota(jnp.int32, (n, HEAD_DIM), 0) // MOBA_BLOCK
        tag = jnp.where(lax.broadcasted_iota(jnp.int32, (n, HEAD_DIM), 1) == blk, 1.0, 0.0).astype(BF16)
        for h, cs in enumerate(heads):
            k_aug = jnp.concatenate([k_ref[rows, cs], tag], axis=1)
            s = lax.dot_general(qa_ref[h], k_aug, _NT, preferred_element_type=F32)
            _flash_step(s, _with_ones(v_ref[rows, cs]), m_ref, acc_ref, h, False)

    def group(t, carry):
        past(MOBA_STEP_BLOCKS * t, MOBA_STEP_BLOCKS)
        return carry

    lax.fori_loop(0, own // MOBA_STEP_BLOCKS, group, 0)
    size = MOBA_STEP_BLOCKS // 2
    while size >= 1:
        @pl.when((own & size) != 0)
        def _(size=size):
            past(own & ~(2 * size - 1), size)
        size //= 2

    for h, cs in enumerate(heads):
        o_ref[:, cs] = _flash_result(acc_ref, h).astype(BF16)


def _moba(q, k, v, kmean):
    L = q.shape[0]
    TQ = MOBA_BLOCK
    nb = L // MOBA_BLOCK
    assert nb <= HEAD_DIM
    km = jnp.pad(kmean, ((0, HEAD_DIM - nb), (0, 0)))
    tile = pl.BlockSpec((TQ, GROUP_WIDTH), lambda i: (i, 0))
    return pl.pallas_call(
        _moba_kernel,
        grid=(L // TQ,),
        in_specs=[tile, _resident((L, GROUP_WIDTH), lambda i: (0, 0)), _resident((L, GROUP_WIDTH), lambda i: (0, 0)),
                  _resident((HEAD_DIM, GROUP_WIDTH), lambda i: (0, 0))],
        out_specs=tile,
        out_shape=jax.ShapeDtypeStruct((L, GROUP_WIDTH), BF16),
        scratch_shapes=[pltpu.VMEM((N_HEADS, TQ, 2 * HEAD_DIM), BF16), pltpu.VMEM((N_HEADS, TQ, HEAD_DIM), F32),
                        pltpu.VMEM((N_HEADS, TQ, 2 * HEAD_DIM), F32)],
        compiler_params=_cparams(("parallel",)),
        name="moba",
    )(q, k, v, km)


def _compress_kernel(c_ref, w1_ref, w2_ref, pos_ref, o_ref):
    n = c_ref.shape[1]
    half = w1_ref.shape[1] // 2
    c = c_ref[0]
    a = jnp.dot(c, w1_ref[0, :half, :], preferred_element_type=F32)
    b = jnp.dot(c, w1_ref[0, half:, :], preferred_element_type=F32)
    bias = jnp.dot(pos_ref[0], w1_ref[0], preferred_element_type=F32)[0:1, :]
    hid = _gelu_tanh(a + pltpu.roll(b, n - 1, 0) + bias)
    out = jnp.dot(hid.astype(BF16), w2_ref[0], preferred_element_type=F32)
    last = lax.broadcasted_iota(jnp.int32, out.shape, 0) == n - 1
    o_ref[0] = jnp.where(last, 0.0, out).astype(BF16)


def _compress(cmp_kv, w1, w2, pos):
    L = cmp_kv.shape[1]
    n = L // NSA_CMP_STRIDE
    wide = NSA_CMP_STRIDE * HEAD_DIM
    chunks = cmp_kv.reshape(2, n, wide)
    hidden = w1.shape[-1]
    return pl.pallas_call(
        _compress_kernel,
        grid=(2,),
        in_specs=[pl.BlockSpec((1, n, wide), lambda j: (j, 0, 0)),
                  pl.BlockSpec((1, 2 * wide, hidden), lambda j: (j, 0, 0)),
                  pl.BlockSpec((1, hidden, HEAD_DIM), lambda j: (j, 0, 0)),
                  pl.BlockSpec((1, 8, 2 * wide), lambda j: (j, 0, 0))],
        out_specs=pl.BlockSpec((1, n, HEAD_DIM), lambda j: (j, 0, 0)),
        out_shape=jax.ShapeDtypeStruct((2, n, HEAD_DIM), BF16),
        compiler_params=_cparams(("parallel",)),
        name="nsa_compress",
    )(chunks, w1, w2, pos)


def _stack_heads(ref):
    return jnp.concatenate([ref[:, h * HEAD_DIM:(h + 1) * HEAD_DIM] for h in range(N_HEADS)], axis=0)


def _nsa_cmp_kernel(q_ref, kc_ref, vc_ref, agg_ref, o_ref, sel_ref):
    TQ = q_ref.shape[0]
    nc = kc_ref.shape[1]
    ns = nc * NSA_CMP_STRIDE // NSA_SEL_BLOCK
    lanes = agg_ref.shape[1]
    start = pl.program_id(0) * TQ
    qs = _stack_heads(q_ref)

    def run(cols, width):
        s = lax.dot_general(qs, kc_ref[0, :cols, :], _NT, preferred_element_type=F32)
        cend = lax.broadcasted_iota(jnp.int32, s.shape, 1) * NSA_CMP_STRIDE + (NSA_CMP_LEN - 1)
        ok = cend <= start + (lax.broadcasted_iota(jnp.int32, s.shape, 0) & (TQ - 1))
        s = jnp.where(ok, s, NEG_INF)
        m = jnp.max(s, axis=1, keepdims=True)
        e = jnp.where(ok, jnp.exp2(s - m), 0.0)
        p = e / jnp.maximum(jnp.sum(e, axis=1, keepdims=True), 1e-30)
        oc = jnp.dot(p.astype(BF16), vc_ref[0, :cols, :], preferred_element_type=F32)
        for h in range(N_HEADS):
            o_ref[:, h * HEAD_DIM:(h + 1) * HEAD_DIM] = oc[h * TQ:(h + 1) * TQ, :]

        imp = p[0:TQ] + p[TQ:2 * TQ] + p[2 * TQ:3 * TQ] + p[3 * TQ:4 * TQ]
        agg = agg_ref[:cols, :width]
        imp_sel = sum(jnp.dot(t, agg, preferred_element_type=F32) for t in _split3(imp))
        blk = lax.broadcasted_iota(jnp.int32, imp_sel.shape, 1)
        own = (start + lax.broadcasted_iota(jnp.int32, imp_sel.shape, 0)) // NSA_SEL_BLOCK
        score = jnp.where(blk == own, FORCE_SCORE, jnp.where(blk < own, imp_sel, NEG_INF))
        sel = _topk_mask(score, min(NSA_TOPK, ns))
        sel_ref[:, :width] = jnp.where(sel > 0.5, 0.0, NEG_INF).astype(BF16)
        if width < lanes:
            sel_ref[:, width:] = jnp.full((TQ, lanes - width), NEG_INF, BF16)

    parts = 4 if nc % (4 * HEAD_DIM) == 0 else 1
    quarter = (start + TQ - 1) // (nc * NSA_CMP_STRIDE // parts)
    for k in range(parts):
        cols = (k + 1) * nc // parts
        width = min(lanes, -(-((k + 1) * ns // parts) // HEAD_DIM) * HEAD_DIM)
        pl.when(quarter == k)(functools.partial(run, cols, width))


def _nsa_cmp(nq, kvc, agg):
    L = nq.shape[0]
    TQ = 2 * Q_BLOCK
    nc = kvc.shape[1]
    lanes = agg.shape[1]
    return pl.pallas_call(
        _nsa_cmp_kernel,
        grid=(L // TQ,),
        in_specs=[pl.BlockSpec((TQ, GROUP_WIDTH), lambda i: (i, 0)),
                  pl.BlockSpec((1, nc, HEAD_DIM), lambda i: (0, 0, 0)),
                  pl.BlockSpec((1, nc, HEAD_DIM), lambda i: (1, 0, 0)),
                  pl.BlockSpec((nc, lanes), lambda i: (0, 0))],
        out_specs=[pl.BlockSpec((TQ, GROUP_WIDTH), lambda i: (i, 0)),
                   pl.BlockSpec((TQ, lanes), lambda i: (i, 0))],
        out_shape=[jax.ShapeDtypeStruct((L, GROUP_WIDTH), F32), jax.ShapeDtypeStruct((L, lanes), BF16)],
        compiler_params=_cparams(("parallel",)),
        name="nsa_cmp",
    )(nq, kvc, kvc, agg)


def _nsa_sel_kernel(q_ref, ks_ref, vs_ref, kw_ref, vw_ref, sb_ref, tag_ref, oc_ref, g_ref, o_ref,
                    qa_ref, m_ref, acc_ref):
    TQ = q_ref.shape[0]
    CK = tag_ref.shape[1]
    cpg = HEAD_DIM // (CK // NSA_SEL_BLOCK)
    R = N_HEADS * TQ
    halves = (slice(0, R // 2), slice(R // 2, R))
    start = pl.program_id(0) * TQ
    qs = _stack_heads(q_ref)
    qpos = start + (lax.broadcasted_iota(jnp.int32, (R, 1), 0) & (TQ - 1))

    bias = sb_ref[...]
    for g in range(sb_ref.shape[1] // HEAD_DIM):
        qa_ref[g] = jnp.concatenate(
            [qs, jnp.concatenate([bias[:, g * HEAD_DIM:(g + 1) * HEAD_DIM]] * N_HEADS, axis=0)], axis=1)

    def scores(c, n):
        rows = pl.ds(pl.multiple_of(c * CK, CK), n * CK)
        tags = tag_ref[pl.ds(c % cpg, n)].reshape(n * CK, HEAD_DIM)
        k_aug = jnp.concatenate([ks_ref[rows, :], tags], axis=1)
        return rows, [lax.dot_general(qa_ref[c // cpg, hs, :], k_aug, _NT, preferred_element_type=F32)
                      for hs in halves]

    c_last = start // CK
    rows, ss = scores(c_last, 1)
    kpos = c_last * CK + lax.broadcasted_iota(jnp.int32, ss[0].shape, 1)
    v_ones = _with_ones(vs_ref[rows, :])
    for idx, hs in enumerate(halves):
        _flash_step(jnp.where(kpos <= qpos[hs], ss[idx], NEG_INF), v_ones, m_ref, acc_ref, idx, True)

    def past(c, n):
        rows, ss = scores(c, n)
        v_ones = _with_ones(vs_ref[rows, :])
        for idx in range(2):
            _flash_step(ss[idx], v_ones, m_ref, acc_ref, idx, False)

    def group(t, carry):
        past(SEL_STEP_CHUNKS * t, SEL_STEP_CHUNKS)
        return carry

    lax.fori_loop(0, c_last // SEL_STEP_CHUNKS, group, 0)
    size = SEL_STEP_CHUNKS // 2
    while size >= 1:
        @pl.when((c_last & size) != 0)
        def _(size=size):
            past(c_last & ~(2 * size - 1), size)
        size //= 2
    o_sel = jnp.concatenate([_flash_result(acc_ref, 0), _flash_result(acc_ref, 1)], axis=0)

    span = TQ + NSA_WINDOW
    w0 = jnp.maximum(start - NSA_WINDOW, 0)
    rows = pl.ds(pl.multiple_of(w0, TQ), span)
    s = lax.dot_general(qs, kw_ref[rows, :], _NT, preferred_element_type=F32)
    wpos = w0 + lax.broadcasted_iota(jnp.int32, s.shape, 1)
    s = jnp.where(jnp.logical_and(wpos <= qpos, wpos > qpos - NSA_WINDOW), s, NEG_INF)
    p = jnp.exp2(s - jnp.max(s, axis=1, keepdims=True))
    o_win = jnp.dot(p.astype(BF16), vw_ref[rows, :], preferred_element_type=F32) / jnp.sum(p, axis=1, keepdims=True)

    g = g_ref[...]
    for h in range(N_HEADS):
        hs = slice(h * TQ, (h + 1) * TQ)
        cs = slice(h * HEAD_DIM, (h + 1) * HEAD_DIM)
        o_ref[:, cs] = (g[:, 3 * h:3 * h + 1] * oc_ref[:, cs] + g[:, 3 * h + 1:3 * h + 2] * o_sel[hs]
                        + g[:, 3 * h + 2:3 * h + 3] * o_win[hs]).astype(BF16)


def _nsa_sel(nq_rope, nkv, sel_bias, tags, o_cmp, gates):
    L = nq_rope.shape[0]
    TQ = Q_BLOCK
    lanes = sel_bias.shape[1]
    R = N_HEADS * TQ
    col = lambda j: _resident((L, HEAD_DIM), lambda i, j=j: (0, j))
    return pl.pallas_call(
        _nsa_sel_kernel,
        grid=(L // TQ,),
        in_specs=[pl.BlockSpec((TQ, GROUP_WIDTH), lambda i: (i, 0)),
                  col(0), col(1), col(2), col(3),
                  pl.BlockSpec((TQ, lanes), lambda i: (i, 0)),
                  _resident(tags.shape, lambda i: (0, 0, 0)),
                  pl.BlockSpec((TQ, GROUP_WIDTH), lambda i: (i, 0)),
                  pl.BlockSpec((TQ, HEAD_DIM), lambda i: (i, 0))],
        out_specs=pl.BlockSpec((TQ, GROUP_WIDTH), lambda i: (i, 0)),
        out_shape=jax.ShapeDtypeStruct((L, GROUP_WIDTH), BF16),
        scratch_shapes=[pltpu.VMEM((lanes // HEAD_DIM, R, 2 * HEAD_DIM), BF16), pltpu.VMEM((2, R // 2, HEAD_DIM), F32),
                        pltpu.VMEM((2, R // 2, 2 * HEAD_DIM), F32)],
        compiler_params=_cparams(("parallel",)),
        name="nsa_sel_win",
    )(nq_rope, nkv, nkv, nkv, nkv, sel_bias, tags, o_cmp, gates)


def _matmul_kernel(a_ref, b_ref, o_ref):
    o_ref[...] = jnp.dot(a_ref[...], b_ref[...], preferred_element_type=F32).astype(o_ref.dtype)


def _mem_proj(mem_b, w):
    n, width = mem_b.shape[0], w.shape[1]
    return pl.pallas_call(
        _matmul_kernel,
        grid=(width // 512,),
        in_specs=[pl.BlockSpec((n, D_MODEL), lambda j: (0, 0)), pl.BlockSpec((D_MODEL, 512), lambda j: (0, j))],
        out_specs=pl.BlockSpec((n, 512), lambda j: (0, j)),
        out_shape=jax.ShapeDtypeStruct((n, width), BF16),
        compiler_params=_cparams(("parallel",)),
        name="mem_proj",
    )(mem_b, w)


def _xattn_kernel(q_ref, k_ref, v_ref, o_ref):
    for h in range(N_HEADS):
        cs = slice(h * HEAD_DIM, (h + 1) * HEAD_DIM)
        s = lax.dot_general(q_ref[:, cs], k_ref[:, cs], _NT, preferred_element_type=F32)
        p = jnp.exp2(s - jnp.max(s, axis=1, keepdims=True))
        o = jnp.dot(p.astype(BF16), v_ref[:, cs], preferred_element_type=F32)
        o_ref[:, cs] = (o / jnp.sum(p, axis=1, keepdims=True)).astype(BF16)


def _xattn(xq, mem_kv):
    L = xq.shape[0]
    n = mem_kv.shape[0]
    row = pl.BlockSpec((ROW_TILE, GROUP_WIDTH), lambda i: (i, 0))
    return pl.pallas_call(
        _xattn_kernel,
        grid=(L // ROW_TILE,),
        in_specs=[row, pl.BlockSpec((n, GROUP_WIDTH), lambda i: (0, 0)),
                  pl.BlockSpec((n, GROUP_WIDTH), lambda i: (0, 1))],
        out_specs=row,
        out_shape=jax.ShapeDtypeStruct((L, GROUP_WIDTH), BF16),
        compiler_params=_cparams(("parallel",)),
        name="xattn",
    )(xq, mem_kv, mem_kv)


def _out_proj_kernel(y0_ref, y1_ref, y2_ref, y3_ref, w_ref, x_ref, g_ref, b_ref, o_ref, ob_ref):
    acc = ALPHA * x_ref[...]
    for j, y_ref in enumerate((y0_ref, y1_ref, y2_ref, y3_ref)):
        acc = acc + jnp.dot(y_ref[...], w_ref[j * GROUP_WIDTH:(j + 1) * GROUP_WIDTH, :], preferred_element_type=F32)
    y = _layer_norm(acc, g_ref[...], b_ref[...])
    o_ref[...] = y
    ob_ref[...] = y.astype(BF16)


def _out_proj(ys, w_o, x, g, b):
    L = x.shape[0]
    tm = ROW_TILE
    part = pl.BlockSpec((tm, GROUP_WIDTH), lambda i: (i, 0))
    row = pl.BlockSpec((tm, D_MODEL), lambda i: (i, 0))
    vec = pl.BlockSpec((1, D_MODEL), lambda i: (0, 0))
    return pl.pallas_call(
        _out_proj_kernel,
        grid=(L // tm,),
        in_specs=[part, part, part, part, _resident((D_MODEL, D_MODEL), lambda i: (0, 0)), row, vec, vec],
        out_specs=[row, row],
        out_shape=[jax.ShapeDtypeStruct((L, D_MODEL), F32), jax.ShapeDtypeStruct((L, D_MODEL), BF16)],
        compiler_params=_cparams(("parallel",)),
        name="out_proj_ln",
    )(*ys, w_o, x, g.reshape(1, -1), b.reshape(1, -1))


def _swiglu_step(x_ref, wg_ref, wu_ref, wd_ref, acc_ref, chunk):
    @pl.when(pl.program_id(1) == 0)
    def _():
        acc_ref[...] = jnp.zeros(acc_ref.shape, F32)

    wg, wu, wd = wg_ref[0].astype(BF16), wu_ref[0].astype(BF16), wd_ref[0].astype(BF16)
    for r in range(0, x_ref.shape[0], chunk):
        rows = slice(r, r + chunk)
        x = x_ref[rows, :]
        gate = jnp.dot(x, wg, preferred_element_type=F32)
        up = jnp.dot(x, wu, preferred_element_type=F32)
        hid = (gate * _sigmoid(gate) * up).astype(BF16)
        acc_ref[rows, :] += jnp.dot(hid, wd, preferred_element_type=F32)


def _ffn_dense_kernel(x_ref, wg_ref, wu_ref, wd_ref, xr_ref, g_ref, b_ref, o_ref, ob_ref, acc_ref):
    _swiglu_step(x_ref, wg_ref, wu_ref, wd_ref, acc_ref, FF_ROW_CHUNK)

    @pl.when(pl.program_id(1) == pl.num_programs(1) - 1)
    def _():
        y = _layer_norm(ALPHA * xr_ref[...] + acc_ref[...], g_ref[...], b_ref[...])
        o_ref[...] = y
        ob_ref[...] = y.astype(BF16)


def _ffn_dense(xb, x, wg, wu, wd, g, b):
    L = x.shape[0]
    tm, tf = ROW_TILE, FF_TILE
    row = pl.BlockSpec((tm, D_MODEL), lambda i, f: (i, 0))
    vec = pl.BlockSpec((1, D_MODEL), lambda i, f: (0, 0))
    return pl.pallas_call(
        _ffn_dense_kernel,
        grid=(L // tm, D_FF // tf),
        in_specs=[row,
                  pl.BlockSpec((1, D_MODEL, tf), lambda i, f: (0, 0, f)),
                  pl.BlockSpec((1, D_MODEL, tf), lambda i, f: (0, 0, f)),
                  pl.BlockSpec((1, tf, D_MODEL), lambda i, f: (0, f, 0)),
                  row, vec, vec],
        out_specs=[row, row],
        out_shape=[jax.ShapeDtypeStruct((L, D_MODEL), F32), jax.ShapeDtypeStruct((L, D_MODEL), BF16)],
        scratch_shapes=[pltpu.VMEM((tm, D_MODEL), F32)],
        compiler_params=_cparams(("parallel", "arbitrary")),
        name="ffn_dense_ln",
    )(xb, wg[None], wu[None], wd[None], x, g.reshape(1, -1), b.reshape(1, -1))


def _start_row_gather(idx_ref, base, n, src_hbm, dst_ref, sem):
    def body(r, carry):
        pltpu.make_async_copy(src_hbm.at[pl.ds(idx_ref[base + r], 1)], dst_ref.at[pl.ds(r, 1)], sem).start()
        return carry

    lax.fori_loop(0, n, body, 0, unroll=8)


def _wait_row_gather(n, src_hbm, dst_ref, sem):
    pltpu.make_async_copy(src_hbm.at[pl.ds(0, n)], dst_ref.at[pl.ds(0, n)], sem).wait()


def _ffn_moe_kernel(exp_ref, used_ref, tok_ref, x_hbm, wg_ref, wu_ref, wd_ref, o_ref, stage_ref, xb_ref, sem,
                    *, per, chunk):
    b, f = pl.program_id(0), pl.program_id(1)
    n_b, n_f = pl.num_programs(0), pl.num_programs(1)
    tm = stage_ref.shape[0]
    used = used_ref[0]
    live = b < used

    @pl.when(jnp.logical_and(live, f == 0))
    def _():
        @pl.when(b == 0)
        def _():
            _start_row_gather(tok_ref, 0, tm, x_hbm, stage_ref, sem)

        _wait_row_gather(tm, x_hbm, stage_ref, sem)
        xb_ref[...] = stage_ref[...].astype(BF16)

    @pl.when(live)
    def _():
        base = jnp.minimum(b + 1, n_b - 1) * tm + f * per
        for j in range(per):
            pltpu.make_async_copy(x_hbm.at[pl.ds(tok_ref[base + j], 1)],
                                  stage_ref.at[pl.ds(f * per + j, 1)], sem).start()
        _swiglu_step(xb_ref, wg_ref, wu_ref, wd_ref, o_ref, chunk)

    @pl.when(jnp.logical_and(b == used - 1, f == n_f - 1))
    def _():
        _wait_row_gather(tm, x_hbm, stage_ref, sem)

    @pl.when(jnp.logical_and(jnp.logical_not(live), f == 0))
    def _():
        o_ref[...] = jnp.zeros(o_ref.shape, F32)


def _ffn_moe(x, buf_tok, blk_exp, n_used, wg, wu, wd, layer):
    P = buf_tok.shape[0]
    tm, tf = MOE_ROWS, MOE_FF_TILE
    nf = D_FF // tf
    assert tm % nf == 0 and tm % MOE_ROW_CHUNK == 0
    ff = lambda b, f, used: jnp.where(b < used[0], f, nf - 1)
    grid_spec = pltpu.PrefetchScalarGridSpec(
        num_scalar_prefetch=3,
        grid=(P // tm, nf),
        in_specs=[pl.BlockSpec(memory_space=pl.ANY),
                  pl.BlockSpec((None, 1, D_MODEL, tf), lambda b, f, e, u, t: (layer, e[b], 0, ff(b, f, u))),
                  pl.BlockSpec((None, 1, D_MODEL, tf), lambda b, f, e, u, t: (layer, e[b], 0, ff(b, f, u))),
                  pl.BlockSpec((None, 1, tf, D_MODEL), lambda b, f, e, u, t: (layer, e[b], ff(b, f, u), 0))],
        out_specs=pl.BlockSpec((tm, D_MODEL), lambda b, f, e, u, t: (b, 0)),
        scratch_shapes=[pltpu.VMEM((tm, D_MODEL), F32), pltpu.VMEM((tm, D_MODEL), BF16),
                        pltpu.SemaphoreType.DMA(())],
    )
    return pl.pallas_call(
        functools.partial(_ffn_moe_kernel, per=tm // nf, chunk=MOE_ROW_CHUNK),
        grid_spec=grid_spec,
        out_shape=jax.ShapeDtypeStruct((P, D_MODEL), F32),
        compiler_params=_cparams(("arbitrary", "arbitrary"), 56),
        name="ffn_moe",
    )(blk_exp, n_used, buf_tok, x, wg, wu, wd)


def _router_kernel(x_ref, r_ref, o_ref):
    xs = _split3(x_ref[...])
    rs = (r_ref[0], r_ref[1], r_ref[2])
    logits = None
    for i, j in ((0, 0), (0, 1), (1, 0), (1, 1), (0, 2), (2, 0)):
        t = jnp.dot(xs[i], rs[j], preferred_element_type=F32)
        logits = t if logits is None else logits + t
    lane = lax.broadcasted_iota(jnp.int32, logits.shape, 1)
    lane_f = lane.astype(F32)
    logits = jnp.where(lane < N_EXPERTS, logits, NEG_INF)
    m1 = jnp.max(logits, axis=1, keepdims=True)
    i1 = jnp.min(jnp.where(logits == m1, lane_f, 128.0), axis=1, keepdims=True)
    rest = jnp.where(lane_f == i1, NEG_INF, logits)
    m2 = jnp.max(rest, axis=1, keepdims=True)
    i2 = jnp.min(jnp.where(rest == m2, lane_f, 128.0), axis=1, keepdims=True)
    e2 = jnp.exp(m2 - m1)
    g1 = 1.0 / (1.0 + e2)
    g2 = e2 / (1.0 + e2)
    o_ref[...] = jnp.where(lane == 0, i1, jnp.where(lane == 1, i2, jnp.where(lane == 2, g1, jnp.where(lane == 3, g2, 0.0))))


def _router(x, router3):
    L = x.shape[0]
    return pl.pallas_call(
        _router_kernel,
        grid=(L // ROW_TILE,),
        in_specs=[pl.BlockSpec((ROW_TILE, D_MODEL), lambda i: (i, 0)),
                  pl.BlockSpec((3, D_MODEL, 128), lambda i: (0, 0, 0))],
        out_specs=pl.BlockSpec((ROW_TILE, 128), lambda i: (i, 0)),
        out_shape=jax.ShapeDtypeStruct((L, 128), F32),
        compiler_params=_cparams(("parallel",)),
        name="moe_router",
    )(x, router3)


def _combine_kernel(da_ref, db_ref, x_ref, gw_ref, g_ref, b_ref, y_hbm, o_ref, ob_ref, ya_ref, yb_ref, sem):
    i, n = pl.program_id(0), pl.num_programs(0)
    tm = x_ref.shape[0]
    slot = i % 2

    def start(step, slot):
        _start_row_gather(da_ref, step * tm, tm, y_hbm, ya_ref.at[slot], sem.at[slot])
        _start_row_gather(db_ref, step * tm, tm, y_hbm, yb_ref.at[slot], sem.at[slot])

    @pl.when(i == 0)
    def _():
        start(0, 0)

    @pl.when(i + 1 < n)
    def _():
        start(i + 1, 1 - slot)

    _wait_row_gather(tm, y_hbm, ya_ref.at[slot], sem.at[slot])
    _wait_row_gather(tm, y_hbm, yb_ref.at[slot], sem.at[slot])
    gw = gw_ref[...]
    f = gw[:, 2:3] * ya_ref[slot] + gw[:, 3:4] * yb_ref[slot]
    y = _layer_norm(ALPHA * x_ref[...] + f, g_ref[...], b_ref[...])
    o_ref[...] = y
    ob_ref[...] = y.astype(BF16)


def _combine(x, y_rows, dest_a, dest_b, route, g, b):
    L = x.shape[0]
    tm = COMBINE_ROWS
    row = pl.BlockSpec((tm, D_MODEL), lambda i, a, b: (i, 0))
    vec = pl.BlockSpec((1, D_MODEL), lambda i, a, b: (0, 0))
    grid_spec = pltpu.PrefetchScalarGridSpec(
        num_scalar_prefetch=2,
        grid=(L // tm,),
        in_specs=[row, pl.BlockSpec((tm, 128), lambda i, a, b: (i, 0)), vec, vec, pl.BlockSpec(memory_space=pl.ANY)],
        out_specs=[row, row],
        scratch_shapes=[pltpu.VMEM((2, tm, D_MODEL), F32), pltpu.VMEM((2, tm, D_MODEL), F32),
                        pltpu.SemaphoreType.DMA((2,))],
    )
    return pl.pallas_call(
        _combine_kernel,
        grid_spec=grid_spec,
        out_shape=[jax.ShapeDtypeStruct((L, D_MODEL), F32), jax.ShapeDtypeStruct((L, D_MODEL), BF16)],
        compiler_params=_cparams(("arbitrary",)),
        name="moe_combine_ln",
    )(dest_a, dest_b, x, route, g.reshape(1, -1), b.reshape(1, -1), y_rows)


def _moe(x, router, wg, wu, wd, layer, g, b):
    T = x.shape[0]
    r32 = jnp.pad(router, ((0, 0), (0, 128 - N_EXPERTS)))
    r1 = r32.astype(BF16)
    r2 = (r32 - r1.astype(F32)).astype(BF16)
    r3 = (r32 - r1.astype(F32) - r2.astype(F32)).astype(BF16)
    route = _router(x, jnp.stack([r1, r2, r3]))
    e_flat = route[:, :TOP_K].astype(jnp.int32).reshape(-1)
    onehot = (e_flat[:, None] == jnp.arange(N_EXPERTS, dtype=jnp.int32)[None, :]).astype(jnp.int32)
    csum = jnp.cumsum(onehot, axis=0)
    rank = jnp.sum(onehot * csum, axis=1) - 1
    counts = csum[-1]
    padded = (counts + MOE_ROWS - 1) // MOE_ROWS * MOE_ROWS
    pend = jnp.cumsum(padded)
    dest = (pend - padded)[e_flat] + rank
    n_blocks = -(-(T * TOP_K + N_EXPERTS * (MOE_ROWS - 1)) // MOE_ROWS)
    P = n_blocks * MOE_ROWS
    tok = jnp.repeat(jnp.arange(T, dtype=jnp.int32), TOP_K)
    buf_tok = jnp.zeros((P,), jnp.int32).at[dest].set(tok)
    n_used = (pend[-1] // MOE_ROWS).astype(jnp.int32).reshape(1)
    blk_start = jnp.arange(n_blocks, dtype=jnp.int32) * MOE_ROWS
    blk_exp = jnp.minimum(jnp.searchsorted(pend, jnp.minimum(blk_start, pend[-1] - 1), side='right'),
                          N_EXPERTS - 1).astype(jnp.int32)
    y_rows = _ffn_moe(x, buf_tok, blk_exp, n_used, wg, wu, wd, layer)
    d2 = dest.reshape(T, TOP_K)
    return _combine(x, y_rows, d2[:, 0], d2[:, 1], route, g, b)


def _rope_tables(L):
    inv_freq = 1.0 / (ROPE_THETA ** (jnp.arange(0, ROPE_DIM, 2, dtype=F32) / ROPE_DIM))
    ang = jnp.arange(L, dtype=F32)[:, None] * inv_freq[None, :]
    cos, sin = jnp.cos(ang), jnp.sin(ang)
    z = lambda n: jnp.zeros((L, n), F32)
    cosf = jnp.concatenate([cos, cos, jnp.ones((L, HEAD_DIM - ROPE_DIM), F32)], axis=1)
    sa = jnp.concatenate([-sin, z(HEAD_DIM - 16)], axis=1)
    sb = jnp.concatenate([z(16), sin, z(HEAD_DIM - ROPE_DIM)], axis=1)
    return cosf, sa, sb


def _w_in_pack_kernel(a_ref, b_ref, c_ref, o_ref):
    o_ref[:, :3328] = a_ref[...].astype(BF16)
    o_ref[:, 3328:3840] = b_ref[...].astype(BF16)
    o_ref[:, 3840:] = c_ref[...].astype(BF16)


def _reorder_w_in(w_all, layer):
    tail = w_all[layer, :, 3328:]
    gates = jnp.pad(tail[:, :12], ((0, 0), (0, 128 - 12)))
    rows = 256
    blk = lambda n: pl.BlockSpec((rows, n), lambda i: (i, 0))
    return pl.pallas_call(
        _w_in_pack_kernel,
        grid=(D_MODEL // rows,),
        in_specs=[pl.BlockSpec((None, rows, 3328), lambda i: (layer, i, 0)), blk(512), blk(128)],
        out_specs=blk(PROJ_WIDTH),
        out_shape=jax.ShapeDtypeStruct((D_MODEL, PROJ_WIDTH), BF16),
        compiler_params=_cparams(("parallel",)),
        name="w_in_pack",
    )(w_all, tail[:, 12:], gates)


def _trunk(x, mem, p):
    L = x.shape[0]
    ns, nc = L // NSA_SEL_BLOCK, L // NSA_CMP_STRIDE
    cosf, sa, sb = _rope_tables(L)
    lanes = -(-ns // HEAD_DIM) * HEAD_DIM
    cj = jnp.arange(nc)[:, None] - 4 * jnp.arange(lanes)[None, :]
    agg = jnp.logical_and(cj >= -1, cj <= 3).astype(BF16)
    ck = min(SEL_CHUNK, L)
    per = ck // NSA_SEL_BLOCK
    n_tag = min(HEAD_DIM // per, L // ck)
    blk_id = jnp.arange(n_tag)[:, None, None] * per + jnp.arange(ck)[None, :, None] // NSA_SEL_BLOCK
    tags = (blk_id == jnp.arange(HEAD_DIM)[None, None, :]).astype(BF16)
    mem_b = mem.astype(BF16)

    x, xb = _ln_in(x, p['ln_in_g'], p['ln_in_b'])
    for i in range(DEPTH):
        u, mq, mk, mv, nqc, nqr, cmp_kv, nkv, xq, gates = _proj(xb, _reorder_w_in(p['w_in'], i), cosf, sa, sb)
        bmat, cmat, pw = _s5_params(p['ssm_a_re'][i], p['ssm_a_im'][i], p['ssm_log_dt'][i], p['ssm_b_re'][i],
                                    p['ssm_b_im'][i], p['ssm_c_re'][i], p['ssm_c_im'][i])
        y_ssm = _s5(u, bmat, cmat, pw, p['ssm_d'][i], p['ssm_w_glu'][i])
        y_moba = _moba(mq, mk, mv, _kmean(mk))
        pos = jnp.stack([p['nsa_pos_k'][i].reshape(1, -1), p['nsa_pos_v'][i].reshape(1, -1)])
        pos = jnp.broadcast_to(pos, (2, 8, pos.shape[-1])).astype(BF16)
        kvc = _compress(cmp_kv, jnp.stack([p['nsa_ck1'][i], p['nsa_cv1'][i]]).astype(BF16),
                        jnp.stack([p['nsa_ck2'][i], p['nsa_cv2'][i]]).astype(BF16), pos)
        o_cmp, sel_bias = _nsa_cmp(nqc, kvc, agg)
        y_nsa = _nsa_sel(nqr, nkv, sel_bias, tags, o_cmp, gates)
        mem_kv = _mem_proj(mem_b, jnp.concatenate([p['mem_wk'][i], p['mem_wv'][i]], axis=1).astype(BF16))
        y_x = _xattn(xq, mem_kv)
        x, xb = _out_proj((y_ssm, y_moba, y_nsa, y_x), p['w_o'][i].astype(BF16), x, p['ln1_g'][i], p['ln1_b'][i])
        j = i // 2
        if i % 2 == 0:
            x, xb = _ffn_dense(xb, x, p['ffn_w_gate'][j].astype(BF16), p['ffn_w_up'][j].astype(BF16),
                               p['ffn_w_down'][j].astype(BF16), p['ln2_g'][i], p['ln2_b'][i])
        else:
            x, xb = _moe(x, p['moe_router'][j], p['moe_w_gate'], p['moe_w_up'], p['moe_w_down'], j,
                         p['ln2_g'][i], p['ln2_b'][i])
    return x


def kernel(x, mem, ln_in_g, ln_in_b, w_in, ssm_a_re, ssm_a_im, ssm_log_dt, ssm_b_re, ssm_b_im, ssm_c_re, ssm_c_im,
           ssm_d, ssm_w_glu, nsa_pos_k, nsa_pos_v, nsa_ck1, nsa_ck2, nsa_cv1, nsa_cv2, mem_wk, mem_wv, w_o, ln1_g,
           ln1_b, ln2_g, ln2_b, ffn_w_gate, ffn_w_up, ffn_w_down, moe_router, moe_w_gate, moe_w_up, moe_w_down):
    p = dict(ln_in_g=ln_in_g, ln_in_b=ln_in_b, w_in=w_in, ssm_a_re=ssm_a_re, ssm_a_im=ssm_a_im,
             ssm_log_dt=ssm_log_dt, ssm_b_re=ssm_b_re, ssm_b_im=ssm_b_im, ssm_c_re=ssm_c_re, ssm_c_im=ssm_c_im,
             ssm_d=ssm_d, ssm_w_glu=ssm_w_glu, nsa_pos_k=nsa_pos_k, nsa_pos_v=nsa_pos_v, nsa_ck1=nsa_ck1,
             nsa_ck2=nsa_ck2, nsa_cv1=nsa_cv1, nsa_cv2=nsa_cv2, mem_wk=mem_wk, mem_wv=mem_wv, w_o=w_o,
             ln1_g=ln1_g, ln1_b=ln1_b, ln2_g=ln2_g, ln2_b=ln2_b, ffn_w_gate=ffn_w_gate, ffn_w_up=ffn_w_up,
             ffn_w_down=ffn_w_down, moe_router=moe_router, moe_w_gate=moe_w_gate, moe_w_up=moe_w_up,
             moe_w_down=moe_w_down)
    assert x.shape[0] == 1 and mem.shape[0] == 1
    return _trunk(x[0], mem[0], p)[None]
```

```python
import functools
import math

import jax
import jax.numpy as jnp
from jax import lax
from jax.experimental import pallas as pl
from jax.experimental.pallas import tpu as pltpu

F32 = jnp.float32
BF16 = jnp.bfloat16

D_MODEL = 2048
DEPTH = 4
HEAD_DIM = 128
GROUP_WIDTH = 512
N_HEADS = 4
SSM_GROUP = 16
SSM_NG = 32
SSM_STATE = 64
SSM_NSTATE = SSM_NG * SSM_STATE
MOBA_BLOCK = 256
MOBA_TOPK = 3
NSA_CMP_LEN = 32
NSA_CMP_STRIDE = 16
NSA_SEL_BLOCK = 64
NSA_TOPK = 16
NSA_WINDOW = 512
ROPE_THETA = 500000.0
ROPE_DIM = 32
D_FF = 5632
N_EXPERTS = 8
TOP_K = 2
Q_BLOCK = 128
LN_EPS = 1e-5
NEG_INF = -1e30
FORCE_SCORE = 1e9
ALPHA = (2.0 * DEPTH) ** 0.25
SCALE = HEAD_DIM ** -0.5 * math.log2(math.e)

ROW_TILE = 512
SCAN_ROWS = 512
SCAN_LANES = 512
SEL_CHUNK = 512
SEL_STEP_CHUNKS = 4
MOBA_STEP_BLOCKS = 8
FF_TILE = 512
FF_ROW_CHUNK = 256
MOE_ROWS = 1056
MOE_ROW_CHUNK = 528
MOE_FF_TILE = 256
COMBINE_ROWS = 256
PROJ_WIDTH = 3968

_NT = (((1,), (1,)), ((), ()))


def _cparams(sem, vmem_mb=48):
    return pltpu.CompilerParams(dimension_semantics=sem, vmem_limit_bytes=vmem_mb * 1024 * 1024)


def _resident(block_shape, index_map):
    return pl.BlockSpec(block_shape, index_map, pipeline_mode=pl.Buffered(1))


def _layer_norm(r, g, b):
    mu = jnp.mean(r, axis=-1, keepdims=True)
    d = r - mu
    var = jnp.mean(d * d, axis=-1, keepdims=True)
    return d * lax.rsqrt(var + LN_EPS) * g + b


def _gelu_tanh(x):
    return x * (0.5 * (1.0 + jnp.tanh(math.sqrt(2.0 / math.pi) * (x + 0.044715 * (x * x * x)))))


def _sigmoid(x):
    return 1.0 / (1.0 + jnp.exp(-x))


def _split3(a):
    a1 = a.astype(BF16)
    r1 = a - a1.astype(F32)
    a2 = r1.astype(BF16)
    a3 = (r1 - a2.astype(F32)).astype(BF16)
    return a1, a2, a3


def _topk_mask(score, k):
    ids = lax.broadcasted_iota(jnp.int32, score.shape, 1).astype(F32)

    def body(_, sc):
        m = jnp.max(sc, axis=1, keepdims=True)
        idx = jnp.min(jnp.where(sc == m, ids, float(score.shape[1])), axis=1, keepdims=True)
        return jnp.where(ids == idx, NEG_INF, sc)

    left = lax.fori_loop(0, k, body, score)
    return jnp.where(jnp.logical_and(score > 0.5 * NEG_INF, left <= 0.5 * NEG_INF), 1.0, 0.0)


def _ln_in_kernel(x_ref, g_ref, b_ref, o_ref, ob_ref):
    y = _layer_norm(x_ref[...], g_ref[...], b_ref[...])
    o_ref[...] = y
    ob_ref[...] = y.astype(BF16)


def _ln_in(x, g, b):
    L = x.shape[0]
    row = pl.BlockSpec((ROW_TILE, D_MODEL), lambda i: (i, 0))
    vec = pl.BlockSpec((1, D_MODEL), lambda i: (0, 0))
    return pl.pallas_call(
        _ln_in_kernel,
        grid=(L // ROW_TILE,),
        in_specs=[row, vec, vec],
        out_specs=[row, row],
        out_shape=[jax.ShapeDtypeStruct((L, D_MODEL), F32), jax.ShapeDtypeStruct((L, D_MODEL), BF16)],
        compiler_params=_cparams(("parallel",)),
        name="ln_in",
    )(x, g.reshape(1, -1), b.reshape(1, -1))


def _proj_kernel(x_ref, w_ref, cos_ref, sa_ref, sb_ref,
                 u_ref, mq_ref, mk_ref, mv_ref, nqc_ref, nqr_ref, cmp_ref, nkv_ref, xq_ref, g_ref):
    x = x_ref[...]
    cos, sa, sb = cos_ref[...], sa_ref[...], sb_ref[...]

    def dot(c0, n):
        return jnp.dot(x, w_ref[:, c0:c0 + n], preferred_element_type=F32)

    def rope(t):
        return t * cos + pltpu.roll(t, HEAD_DIM - 16, 1) * sa + pltpu.roll(t, 16, 1) * sb

    heads = [slice(h * HEAD_DIM, (h + 1) * HEAD_DIM) for h in range(N_HEADS)]
    u_ref[...] = dot(0, 512)
    t = dot(512, 512)
    for cs in heads:
        mq_ref[:, cs] = (rope(t[:, cs]) * SCALE).astype(BF16)
    t = dot(1024, 512)
    for cs in heads:
        mk_ref[:, cs] = rope(t[:, cs]).astype(BF16)
    mv_ref[...] = dot(1536, 512).astype(BF16)
    t = dot(2048, 512)
    nqc_ref[...] = (t * SCALE).astype(BF16)
    for cs in heads:
        nqr_ref[:, cs] = (rope(t[:, cs]) * SCALE).astype(BF16)
    t = dot(2560, 512)
    cmp_ref[0] = t[:, heads[0]].astype(BF16)
    cmp_ref[1] = t[:, heads[1]].astype(BF16)
    nkv_ref[:, heads[0]] = rope(t[:, heads[2]]).astype(BF16)
    nkv_ref[:, heads[1]] = t[:, heads[3]].astype(BF16)
    t = dot(3072, 256)
    nkv_ref[:, heads[2]] = rope(t[:, heads[0]]).astype(BF16)
    nkv_ref[:, heads[3]] = t[:, heads[1]].astype(BF16)
    t = dot(3328, 640)
    xq_ref[...] = (t[:, :512] * SCALE).astype(BF16)
    g_ref[...] = _sigmoid(t[:, 512:])


def _proj(xb, w, cosf, sa, sb):
    L = xb.shape[0]
    tm = ROW_TILE
    row = lambda n: pl.BlockSpec((tm, n), lambda i: (i, 0))
    outs = [
        (jax.ShapeDtypeStruct((L, 512), F32), row(512)),
        (jax.ShapeDtypeStruct((L, 512), BF16), row(512)),
        (jax.ShapeDtypeStruct((L, 512), BF16), row(512)),
        (jax.ShapeDtypeStruct((L, 512), BF16), row(512)),
        (jax.ShapeDtypeStruct((L, 512), BF16), row(512)),
        (jax.ShapeDtypeStruct((L, 512), BF16), row(512)),
        (jax.ShapeDtypeStruct((2, L, 128), BF16), pl.BlockSpec((2, tm, 128), lambda i: (0, i, 0))),
        (jax.ShapeDtypeStruct((L, 512), BF16), row(512)),
        (jax.ShapeDtypeStruct((L, 512), BF16), row(512)),
        (jax.ShapeDtypeStruct((L, 128), F32), row(128)),
    ]
    return pl.pallas_call(
        _proj_kernel,
        grid=(L // tm,),
        in_specs=[row(D_MODEL), _resident((D_MODEL, PROJ_WIDTH), lambda i: (0, 0)),
                  row(128), row(128), row(128)],
        out_specs=[o[1] for o in outs],
        out_shape=[o[0] for o in outs],
        compiler_params=_cparams(("parallel",), 56),
        name="in_proj",
    )(xb, w, cosf, sa, sb)


def _s5_kernel(u_ref, bmat_ref, cmat_ref, pw_ref, d_ref, wglu_ref, o_ref, hr_ref, hi_ref, car_ref, cai_ref):
    T = u_ref.shape[0]
    W = SCAN_LANES

    @pl.when(pl.program_id(0) == 0)
    def _():
        car_ref[...] = jnp.zeros(car_ref.shape, F32)
        cai_ref[...] = jnp.zeros(cai_ref.shape, F32)

    u = u_ref[...]
    ub = u.astype(BF16)
    for b in range(4):
        bu = jnp.dot(ub[:, b * 128:(b + 1) * 128], bmat_ref[b], preferred_element_type=F32)
        hr_ref[:, b * 512:(b + 1) * 512] = bu[:, :512]
        hi_ref[:, b * 512:(b + 1) * 512] = bu[:, 512:]

    for cb in range(SSM_NSTATE // W):
        sl = slice(cb * W, (cb + 1) * W)
        steps = [(pw_ref[2 * j, :, sl], pw_ref[2 * j + 1, :, sl], 1 << j) for j in range(3)]
        p_r, p_i = pw_ref[6, :, sl], pw_ref[7, :, sl]

        def tile(t, carry, sl=sl, steps=steps, p_r=p_r, p_i=p_i):
            c_r, c_i = carry
            rows = pl.ds(pl.multiple_of(t * 8, 8), 8)
            x_r, x_i = hr_ref[rows, sl], hi_ref[rows, sl]
            for l_r, l_i, d in steps:
                s_r, s_i = pltpu.roll(x_r, d, 0), pltpu.roll(x_i, d, 0)
                x_r, x_i = x_r + l_r * s_r - l_i * s_i, x_i + l_r * s_i + l_i * s_r
            x_r, x_i = x_r + p_r * c_r - p_i * c_i, x_i + p_r * c_i + p_i * c_r
            hr_ref[rows, sl] = x_r
            hi_ref[rows, sl] = x_i
            return (jnp.broadcast_to(x_r[7:8, :], (8, W)), jnp.broadcast_to(x_i[7:8, :], (8, W)))

        c_r, c_i = lax.fori_loop(0, T // 8, tile, (car_ref[:, sl], cai_ref[:, sl]))
        car_ref[:, sl] = c_r
        cai_ref[:, sl] = c_i

    ys = []
    for b in range(4):
        h_r = hr_ref[:, b * 512:(b + 1) * 512].astype(BF16)
        h_i = hi_ref[:, b * 512:(b + 1) * 512].astype(BF16)
        ys.append(jnp.dot(h_r, cmat_ref[b, :512, :], preferred_element_type=F32)
                  + jnp.dot(h_i, cmat_ref[b, 512:, :], preferred_element_type=F32))
    y = jnp.concatenate(ys, axis=1) + d_ref[...] * u
    z = _gelu_tanh(y)
    gate = _sigmoid(jnp.dot(z.astype(BF16), wglu_ref[...], preferred_element_type=F32))
    o_ref[...] = (z * gate).astype(BF16)


def _s5_params(a_re, a_im, log_dt, b_re, b_im, c_re, c_im):
    dt = jnp.exp(log_dt)[:, None]
    mag = jnp.exp(a_re * dt)
    lam_re, lam_im = mag * jnp.cos(a_im * dt), mag * jnp.sin(a_im * dt)
    den = a_re * a_re + a_im * a_im
    nr, ni = lam_re - 1.0, lam_im
    coef_re = (nr * a_re + ni * a_im) / den
    coef_im = (ni * a_re - nr * a_im) / den
    bbar_re = coef_re[..., None] * b_re - coef_im[..., None] * b_im
    bbar_im = coef_re[..., None] * b_im + coef_im[..., None] * b_re
    eye = jnp.eye(8, dtype=F32)
    to_b = lambda m: jnp.einsum('bgnc,gh->bgchn', m.reshape(4, 8, SSM_STATE, SSM_GROUP), eye).reshape(4, 128, 512)
    bmat = jnp.concatenate([to_b(bbar_re), to_b(bbar_im)], axis=-1).astype(BF16)
    to_c = lambda m: jnp.einsum('bgcn,gh->bgnhc', m.reshape(4, 8, SSM_GROUP, SSM_STATE), eye).reshape(4, 512, 128)
    cmat = jnp.concatenate([to_c(c_re), -to_c(c_im)], axis=1).astype(BF16)
    lr, li = lam_re.reshape(-1), lam_im.reshape(-1)
    pr, pi = [lr], [li]
    for _ in range(7):
        pr, pi = pr + [pr[-1] * lr - pi[-1] * li], pi + [pr[-1] * li + pi[-1] * lr]
    row = jnp.arange(8)[:, None]
    tabs = []
    for d in (1, 2, 4):
        tabs += [jnp.where(row >= d, pr[d - 1][None, :], 0.0), jnp.where(row >= d, pi[d - 1][None, :], 0.0)]
    tabs += [jnp.stack(pr, axis=0), jnp.stack(pi, axis=0)]
    return bmat, cmat, jnp.stack(tabs, axis=0)


def _s5(u, bmat, cmat, pw, d_skip, w_glu):
    L = u.shape[0]
    T = SCAN_ROWS
    row = pl.BlockSpec((T, 512), lambda i: (i, 0))
    full = lambda shape: pl.BlockSpec(shape, lambda i: (0,) * len(shape))
    return pl.pallas_call(
        _s5_kernel,
        grid=(L // T,),
        in_specs=[row, full((4, 128, 1024)), full((4, 1024, 128)), full((8, 8, SSM_NSTATE)),
                  full((1, 512)), full((512, 512))],
        out_specs=row,
        out_shape=jax.ShapeDtypeStruct((L, 512), BF16),
        scratch_shapes=[pltpu.VMEM((T, SSM_NSTATE), F32), pltpu.VMEM((T, SSM_NSTATE), F32),
                        pltpu.VMEM((8, SSM_NSTATE), F32), pltpu.VMEM((8, SSM_NSTATE), F32)],
        compiler_params=_cparams(("arbitrary",)),
        name="s5",
    )(u, bmat, cmat, pw, d_skip.reshape(1, -1), w_glu.astype(BF16))


def _kmean_kernel(k_ref, o_ref):
    rows = k_ref.shape[0]
    k = k_ref[...].astype(F32).reshape(rows // MOBA_BLOCK, MOBA_BLOCK, GROUP_WIDTH)
    o_ref[...] = jnp.mean(k, axis=1).astype(BF16)


def _kmean(k):
    L = k.shape[0]
    rows = 8 * MOBA_BLOCK if L % (8 * MOBA_BLOCK) == 0 else L
    return pl.pallas_call(
        _kmean_kernel,
        grid=(L // rows,),
        in_specs=[pl.BlockSpec((rows, GROUP_WIDTH), lambda i: (i, 0))],
        out_specs=pl.BlockSpec((rows // MOBA_BLOCK, GROUP_WIDTH), lambda i: (i, 0)),
        out_shape=jax.ShapeDtypeStruct((L // MOBA_BLOCK, GROUP_WIDTH), BF16),
        compiler_params=_cparams(("parallel",)),
        name="moba_kmean",
    )(k)


def _with_ones(v):
    return jnp.concatenate([v, jnp.ones(v.shape, v.dtype)], axis=1)


def _flash_step(s, v_ones, m_ref, acc_ref, idx, first):
    row_max = jnp.max(s, axis=1, keepdims=True)
    if first:
        m_new = jnp.broadcast_to(row_max, (s.shape[0], HEAD_DIM))
    else:
        m_old = m_ref[idx]
        m_new = jnp.maximum(m_old, row_max)
    p = jnp.concatenate([jnp.exp2(s[:, c:c + HEAD_DIM] - m_new) for c in range(0, s.shape[1], HEAD_DIM)], axis=1)
    pv = jnp.dot(p.astype(BF16), v_ones, preferred_element_type=F32)
    if first:
        acc_ref[idx] = pv
    else:
        a = jnp.exp2(m_old - m_new)
        acc_ref[idx] = jnp.concatenate([a, a], axis=1) * acc_ref[idx] + pv
    m_ref[idx] = m_new


def _flash_result(acc_ref, idx):
    acc = acc_ref[idx]
    return acc[:, :HEAD_DIM] / acc[:, HEAD_DIM:]


def _moba_kernel(q_ref, k_ref, v_ref, km_ref, o_ref, qa_ref, m_ref, acc_ref):
    TQ = q_ref.shape[0]
    own = pl.program_id(0)
    heads = [slice(h * HEAD_DIM, (h + 1) * HEAD_DIM) for h in range(N_HEADS)]

    gs = jnp.concatenate([lax.dot_general(q_ref[:, cs], km_ref[:, cs], _NT, preferred_element_type=F32)
                          for cs in heads], axis=0)
    blk = lax.broadcasted_iota(jnp.int32, gs.shape, 1)
    sel = _topk_mask(jnp.where(blk < own, gs, NEG_INF), MOBA_TOPK)
    bias = jnp.where(sel > 0.5, 0.0, NEG_INF).astype(BF16)
    for h, cs in enumerate(heads):
        qa_ref[h] = jnp.concatenate([q_ref[:, cs], bias[h * TQ:(h + 1) * TQ]], axis=1)

    own_rows = pl.ds(pl.multiple_of(own * MOBA_BLOCK, MOBA_BLOCK), MOBA_BLOCK)
    causal = (lax.broadcasted_iota(jnp.int32, (TQ, MOBA_BLOCK), 1)
              <= lax.broadcasted_iota(jnp.int32, (TQ, MOBA_BLOCK), 0))
    for h, cs in enumerate(heads):
        s = lax.dot_general(q_ref[:, cs], k_ref[own_rows, cs], _NT, preferred_element_type=F32)
        _flash_step(jnp.where(causal, s, NEG_INF), _with_ones(v_ref[own_rows, cs]), m_ref, acc_ref, h, True)

    def past(first_block, n_blocks):
        n = n_blocks * MOBA_BLOCK
        rows = pl.ds(pl.multiple_of(first_block * MOBA_BLOCK, MOBA_BLOCK), n)
        blk = first_block + lax.broadcasted_iota(jnp.int32, (n, HEAD_DIM), 0) // MOBA_BLOCK
        tag = jnp.where(lax.broadcasted_iota(jnp.int32, (n, HEAD_DIM), 1) == blk, 1.0, 0.0).astype(BF16)
        for h, cs in enumerate(heads):
            k_aug = jnp.concatenate([k_ref[rows, cs], tag], axis=1)
            s = lax.dot_general(qa_ref[h], k_aug, _NT, preferred_element_type=F32)
            _flash_step(s, _with_ones(v_ref[rows, cs]), m_ref, acc_ref, h, False)

    def group(t, carry):
        past(MOBA_STEP_BLOCKS * t, MOBA_STEP_BLOCKS)
        return carry

    lax.fori_loop(0, own // MOBA_STEP_BLOCKS, group, 0)
    size = MOBA_STEP_BLOCKS // 2
    while size >= 1:
        @pl.when((own & size) != 0)
        def _(size=size):
            past(own & ~(2 * size - 1), size)
        size //= 2

    for h, cs in enumerate(heads):
        o_ref[:, cs] = _flash_result(acc_ref, h).astype(BF16)


def _moba(q, k, v, kmean):
    L = q.shape[0]
    TQ = MOBA_BLOCK
    nb = L // MOBA_BLOCK
    assert nb <= HEAD_DIM
    km = jnp.pad(kmean, ((0, HEAD_DIM - nb), (0, 0)))
    tile = pl.BlockSpec((TQ, GROUP_WIDTH), lambda i: (i, 0))
    return pl.pallas_call(
        _moba_kernel,
        grid=(L // TQ,),
        in_specs=[tile, _resident((L, GROUP_WIDTH), lambda i: (0, 0)), _resident((L, GROUP_WIDTH), lambda i: (0, 0)),
                  _resident((HEAD_DIM, GROUP_WIDTH), lambda i: (0, 0))],
        out_specs=tile,
        out_shape=jax.ShapeDtypeStruct((L, GROUP_WIDTH), BF16),
        scratch_shapes=[pltpu.VMEM((N_HEADS, TQ, 2 * HEAD_DIM), BF16), pltpu.VMEM((N_HEADS, TQ, HEAD_DIM), F32),
                        pltpu.VMEM((N_HEADS, TQ, 2 * HEAD_DIM), F32)],
        compiler_params=_cparams(("parallel",)),
        name="moba",
    )(q, k, v, km)


def _compress_kernel(c_ref, w1_ref, w2_ref, pos_ref, o_ref):
    n = c_ref.shape[1]
    half = w1_ref.shape[1] // 2
    c = c_ref[0]
    a = jnp.dot(c, w1_ref[0, :half, :], preferred_element_type=F32)
    b = jnp.dot(c, w1_ref[0, half:, :], preferred_element_type=F32)
    bias = jnp.dot(pos_ref[0], w1_ref[0], preferred_element_type=F32)[0:1, :]
    hid = _gelu_tanh(a + pltpu.roll(b, n - 1, 0) + bias)
    out = jnp.dot(hid.astype(BF16), w2_ref[0], preferred_element_type=F32)
    last = lax.broadcasted_iota(jnp.int32, out.shape, 0) == n - 1
    o_ref[0] = jnp.where(last, 0.0, out).astype(BF16)


def _compress(cmp_kv, w1, w2, pos):
    L = cmp_kv.shape[1]
    n = L // NSA_CMP_STRIDE
    wide = NSA_CMP_STRIDE * HEAD_DIM
    chunks = cmp_kv.reshape(2, n, wide)
    hidden = w1.shape[-1]
    return pl.pallas_call(
        _compress_kernel,
        grid=(2,),
        in_specs=[pl.BlockSpec((1, n, wide), lambda j: (j, 0, 0)),
                  pl.BlockSpec((1, 2 * wide, hidden), lambda j: (j, 0, 0)),
                  pl.BlockSpec((1, hidden, HEAD_DIM), lambda j: (j, 0, 0)),
                  pl.BlockSpec((1, 8, 2 * wide), lambda j: (j, 0, 0))],
        out_specs=pl.BlockSpec((1, n, HEAD_DIM), lambda j: (j, 0, 0)),
        out_shape=jax.ShapeDtypeStruct((2, n, HEAD_DIM), BF16),
        compiler_params=_cparams(("parallel",)),
        name="nsa_compress",
    )(chunks, w1, w2, pos)


def _stack_heads(ref):
    return jnp.concatenate([ref[:, h * HEAD_DIM:(h + 1) * HEAD_DIM] for h in range(N_HEADS)], axis=0)


def _nsa_cmp_kernel(q_ref, kc_ref, vc_ref, agg_ref, o_ref, sel_ref):
    TQ = q_ref.shape[0]
    nc = kc_ref.shape[1]
    ns = nc * NSA_CMP_STRIDE // NSA_SEL_BLOCK
    lanes = agg_ref.shape[1]
    start = pl.program_id(0) * TQ
    qs = _stack_heads(q_ref)

    def run(cols, width):
        s = lax.dot_general(qs, kc_ref[0, :cols, :], _NT, preferred_element_type=F32)
        cend = lax.broadcasted_iota(jnp.int32, s.shape, 1) * NSA_CMP_STRIDE + (NSA_CMP_LEN - 1)
        ok = cend <= start + (lax.broadcasted_iota(jnp.int32, s.shape, 0) & (TQ - 1))
        s = jnp.where(ok, s, NEG_INF)
        m = jnp.max(s, axis=1, keepdims=True)
        e = jnp.where(ok, jnp.exp2(s - m), 0.0)
        p = e / jnp.maximum(jnp.sum(e, axis=1, keepdims=True), 1e-30)
        oc = jnp.dot(p.astype(BF16), vc_ref[0, :cols, :], preferred_element_type=F32)
        for h in range(N_HEADS):
            o_ref[:, h * HEAD_DIM:(h + 1) * HEAD_DIM] = oc[h * TQ:(h + 1) * TQ, :]

        imp = p[0:TQ] + p[TQ:2 * TQ] + p[2 * TQ:3 * TQ] + p[3 * TQ:4 * TQ]
        agg = agg_ref[:cols, :width]
        imp_sel = sum(jnp.dot(t, agg, preferred_element_type=F32) for t in _split3(imp))
        blk = lax.broadcasted_iota(jnp.int32, imp_sel.shape, 1)
        own = (start + lax.broadcasted_iota(jnp.int32, imp_sel.shape, 0)) // NSA_SEL_BLOCK
        score = jnp.where(blk == own, FORCE_SCORE, jnp.where(blk < own, imp_sel, NEG_INF))
        sel = _topk_mask(score, min(NSA_TOPK, ns))
        sel_ref[:, :width] = jnp.where(sel > 0.5, 0.0, NEG_INF).astype(BF16)
        if width < lanes:
            sel_ref[:, width:] = jnp.full((TQ, lanes - width), NEG_INF, BF16)

    parts = 4 if nc % (4 * HEAD_DIM) == 0 else 1
    quarter = (start + TQ - 1) // (nc * NSA_CMP_STRIDE // parts)
    for k in range(parts):
        cols = (k + 1) * nc // parts
        width = min(lanes, -(-((k + 1) * ns // parts) // HEAD_DIM) * HEAD_DIM)
        pl.when(quarter == k)(functools.partial(run, cols, width))


def _nsa_cmp(nq, kvc, agg):
    L = nq.shape[0]
    TQ = 2 * Q_BLOCK
    nc = kvc.shape[1]
    lanes = agg.shape[1]
    return pl.pallas_call(
        _nsa_cmp_kernel,
        grid=(L // TQ,),
        in_specs=[pl.BlockSpec((TQ, GROUP_WIDTH), lambda i: (i, 0)),
                  pl.BlockSpec((1, nc, HEAD_DIM), lambda i: (0, 0, 0)),
                  pl.BlockSpec((1, nc, HEAD_DIM), lambda i: (1, 0, 0)),
                  pl.BlockSpec((nc, lanes), lambda i: (0, 0))],
        out_specs=[pl.BlockSpec((TQ, GROUP_WIDTH), lambda i: (i, 0)),
                   pl.BlockSpec((TQ, lanes), lambda i: (i, 0))],
        out_shape=[jax.ShapeDtypeStruct((L, GROUP_WIDTH), F32), jax.ShapeDtypeStruct((L, lanes), BF16)],
        compiler_params=_cparams(("parallel",)),
        name="nsa_cmp",
    )(nq, kvc, kvc, agg)


def _nsa_sel_kernel(q_ref, ks_ref, vs_ref, kw_ref, vw_ref, sb_ref, tag_ref, oc_ref, g_ref, o_ref,
                    qa_ref, m_ref, acc_ref):
    TQ = q_ref.shape[0]
    CK = tag_ref.shape[1]
    cpg = HEAD_DIM // (CK // NSA_SEL_BLOCK)
    R = N_HEADS * TQ
    halves = (slice(0, R // 2), slice(R // 2, R))
    start = pl.program_id(0) * TQ
    qs = _stack_heads(q_ref)
    qpos = start + (lax.broadcasted_iota(jnp.int32, (R, 1), 0) & (TQ - 1))

    bias = sb_ref[...]
    for g in range(sb_ref.shape[1] // HEAD_DIM):
        qa_ref[g] = jnp.concatenate(
            [qs, jnp.concatenate([bias[:, g * HEAD_DIM:(g + 1) * HEAD_DIM]] * N_HEADS, axis=0)], axis=1)

    def scores(c, n):
        rows = pl.ds(pl.multiple_of(c * CK, CK), n * CK)
        tags = tag_ref[pl.ds(c % cpg, n)].reshape(n * CK, HEAD_DIM)
        k_aug = jnp.concatenate([ks_ref[rows, :], tags], axis=1)
        return rows, [lax.dot_general(qa_ref[c // cpg, hs, :], k_aug, _NT, preferred_element_type=F32)
                      for hs in halves]

    c_last = start // CK
    rows, ss = scores(c_last, 1)
    kpos = c_last * CK + lax.broadcasted_iota(jnp.int32, ss[0].shape, 1)
    v_ones = _with_ones(vs_ref[rows, :])
    for idx, hs in enumerate(halves):
        _flash_step(jnp.where(kpos <= qpos[hs], ss[idx], NEG_INF), v_ones, m_ref, acc_ref, idx, True)

    def past(c, n):
        rows, ss = scores(c, n)
        v_ones = _with_ones(vs_ref[rows, :])
        for idx in range(2):
            _flash_step(ss[idx], v_ones, m_ref, acc_ref, idx, False)

    def group(t, carry):
        past(SEL_STEP_CHUNKS * t, SEL_STEP_CHUNKS)
        return carry

    lax.fori_loop(0, c_last // SEL_STEP_CHUNKS, group, 0)
    size = SEL_STEP_CHUNKS // 2
    while size >= 1:
        @pl.when((c_last & size) != 0)
        def _(size=size):
            past(c_last & ~(2 * size - 1), size)
        size //= 2
    o_sel = jnp.concatenate([_flash_result(acc_ref, 0), _flash_result(acc_ref, 1)], axis=0)

    span = TQ + NSA_WINDOW
    w0 = jnp.maximum(start - NSA_WINDOW, 0)
    rows = pl.ds(pl.multiple_of(w0, TQ), span)
    s = lax.dot_general(qs, kw_ref[rows, :], _NT, preferred_element_type=F32)
    wpos = w0 + lax.broadcasted_iota(jnp.int32, s.shape, 1)
    s = jnp.where(jnp.logical_and(wpos <= qpos, wpos > qpos - NSA_WINDOW), s, NEG_INF)
    p = jnp.exp2(s - jnp.max(s, axis=1, keepdims=True))
    o_win = jnp.dot(p.astype(BF16), vw_ref[rows, :], preferred_element_type=F32) / jnp.sum(p, axis=1, keepdims=True)

    g = g_ref[...]
    for h in range(N_HEADS):
        hs = slice(h * TQ, (h + 1) * TQ)
        cs = slice(h * HEAD_DIM, (h + 1) * HEAD_DIM)
        o_ref[:, cs] = (g[:, 3 * h:3 * h + 1] * oc_ref[:, cs] + g[:, 3 * h + 1:3 * h + 2] * o_sel[hs]
                        + g[:, 3 * h + 2:3 * h + 3] * o_win[hs]).astype(BF16)


def _nsa_sel(nq_rope, nkv, sel_bias, tags, o_cmp, gates):
    L = nq_rope.shape[0]
    TQ = Q_BLOCK
    lanes = sel_bias.shape[1]
    R = N_HEADS * TQ
    col = lambda j: _resident((L, HEAD_DIM), lambda i, j=j: (0, j))
    return pl.pallas_call(
        _nsa_sel_kernel,
        grid=(L // TQ,),
        in_specs=[pl.BlockSpec((TQ, GROUP_WIDTH), lambda i: (i, 0)),
                  col(0), col(1), col(2), col(3),
                  pl.BlockSpec((TQ, lanes), lambda i: (i, 0)),
                  _resident(tags.shape, lambda i: (0, 0, 0)),
                  pl.BlockSpec((TQ, GROUP_WIDTH), lambda i: (i, 0)),
                  pl.BlockSpec((TQ, HEAD_DIM), lambda i: (i, 0))],
        out_specs=pl.BlockSpec((TQ, GROUP_WIDTH), lambda i: (i, 0)),
        out_shape=jax.ShapeDtypeStruct((L, GROUP_WIDTH), BF16),
        scratch_shapes=[pltpu.VMEM((lanes // HEAD_DIM, R, 2 * HEAD_DIM), BF16), pltpu.VMEM((2, R // 2, HEAD_DIM), F32),
                        pltpu.VMEM((2, R // 2, 2 * HEAD_DIM), F32)],
        compiler_params=_cparams(("parallel",)),
        name="nsa_sel_win",
    )(nq_rope, nkv, nkv, nkv, nkv, sel_bias, tags, o_cmp, gates)


def _matmul_kernel(a_ref, b_ref, o_ref):
    o_ref[...] = jnp.dot(a_ref[...], b_ref[...], preferred_element_type=F32).astype(o_ref.dtype)


def _mem_proj(mem_b, w):
    n, width = mem_b.shape[0], w.shape[1]
    return pl.pallas_call(
        _matmul_kernel,
        grid=(width // 512,),
        in_specs=[pl.BlockSpec((n, D_MODEL), lambda j: (0, 0)), pl.BlockSpec((D_MODEL, 512), lambda j: (0, j))],
        out_specs=pl.BlockSpec((n, 512), lambda j: (0, j)),
        out_shape=jax.ShapeDtypeStruct((n, width), BF16),
        compiler_params=_cparams(("parallel",)),
        name="mem_proj",
    )(mem_b, w)


def _xattn_kernel(q_ref, k_ref, v_ref, o_ref):
    for h in range(N_HEADS):
        cs = slice(h * HEAD_DIM, (h + 1) * HEAD_DIM)
        s = lax.dot_general(q_ref[:, cs], k_ref[:, cs], _NT, preferred_element_type=F32)
        p = jnp.exp2(s - jnp.max(s, axis=1, keepdims=True))
        o = jnp.dot(p.astype(BF16), v_ref[:, cs], preferred_element_type=F32)
        o_ref[:, cs] = (o / jnp.sum(p, axis=1, keepdims=True)).astype(BF16)


def _xattn(xq, mem_kv):
    L = xq.shape[0]
    n = mem_kv.shape[0]
    row = pl.BlockSpec((ROW_TILE, GROUP_WIDTH), lambda i: (i, 0))
    return pl.pallas_call(
        _xattn_kernel,
        grid=(L // ROW_TILE,),
        in_specs=[row, pl.BlockSpec((n, GROUP_WIDTH), lambda i: (0, 0)),
                  pl.BlockSpec((n, GROUP_WIDTH), lambda i: (0, 1))],
        out_specs=row,
        out_shape=jax.ShapeDtypeStruct((L, GROUP_WIDTH), BF16),
        compiler_params=_cparams(("parallel",)),
        name="xattn",
    )(xq, mem_kv, mem_kv)


def _out_proj_kernel(y0_ref, y1_ref, y2_ref, y3_ref, w_ref, x_ref, g_ref, b_ref, o_ref, ob_ref):
    acc = ALPHA * x_ref[...]
    for j, y_ref in enumerate((y0_ref, y1_ref, y2_ref, y3_ref)):
        acc = acc + jnp.dot(y_ref[...], w_ref[j * GROUP_WIDTH:(j + 1) * GROUP_WIDTH, :], preferred_element_type=F32)
    y = _layer_norm(acc, g_ref[...], b_ref[...])
    o_ref[...] = y
    ob_ref[...] = y.astype(BF16)


def _out_proj(ys, w_o, x, g, b):
    L = x.shape[0]
    tm = ROW_TILE
    part = pl.BlockSpec((tm, GROUP_WIDTH), lambda i: (i, 0))
    row = pl.BlockSpec((tm, D_MODEL), lambda i: (i, 0))
    vec = pl.BlockSpec((1, D_MODEL), lambda i: (0, 0))
    return pl.pallas_call(
        _out_proj_kernel,
        grid=(L // tm,),
        in_specs=[part, part, part, part, _resident((D_MODEL, D_MODEL), lambda i: (0, 0)), row, vec, vec],
        out_specs=[row, row],
        out_shape=[jax.ShapeDtypeStruct((L, D_MODEL), F32), jax.ShapeDtypeStruct((L, D_MODEL), BF16)],
        compiler_params=_cparams(("parallel",)),
        name="out_proj_ln",
    )(*ys, w_o, x, g.reshape(1, -1), b.reshape(1, -1))


def _swiglu_step(x_ref, wg_ref, wu_ref, wd_ref, acc_ref, chunk):
    @pl.when(pl.program_id(1) == 0)
    def _():
        acc_ref[...] = jnp.zeros(acc_ref.shape, F32)

    wg, wu, wd = wg_ref[0].astype(BF16), wu_ref[0].astype(BF16), wd_ref[0].astype(BF16)
    for r in range(0, x_ref.shape[0], chunk):
        rows = slice(r, r + chunk)
        x = x_ref[rows, :]
        gate = jnp.dot(x, wg, preferred_element_type=F32)
        up = jnp.dot(x, wu, preferred_element_type=F32)
        hid = (gate * _sigmoid(gate) * up).astype(BF16)
        acc_ref[rows, :] += jnp.dot(hid, wd, preferred_element_type=F32)


def _ffn_dense_kernel(x_ref, wg_ref, wu_ref, wd_ref, xr_ref, g_ref, b_ref, o_ref, ob_ref, acc_ref):
    _swiglu_step(x_ref, wg_ref, wu_ref, wd_ref, acc_ref, FF_ROW_CHUNK)

    @pl.when(pl.program_id(1) == pl.num_programs(1) - 1)
    def _():
        y = _layer_norm(ALPHA * xr_ref[...] + acc_ref[...], g_ref[...], b_ref[...])
        o_ref[...] = y
        ob_ref[...] = y.astype(BF16)


def _ffn_dense(xb, x, wg, wu, wd, g, b):
    L = x.shape[0]
    tm, tf = ROW_TILE, FF_TILE
    row = pl.BlockSpec((tm, D_MODEL), lambda i, f: (i, 0))
    vec = pl.BlockSpec((1, D_MODEL), lambda i, f: (0, 0))
    return pl.pallas_call(
        _ffn_dense_kernel,
        grid=(L // tm, D_FF // tf),
        in_specs=[row,
                  pl.BlockSpec((1, D_MODEL, tf), lambda i, f: (0, 0, f)),
                  pl.BlockSpec((1, D_MODEL, tf), lambda i, f: (0, 0, f)),
                  pl.BlockSpec((1, tf, D_MODEL), lambda i, f: (0, f, 0)),
                  row, vec, vec],
        out_specs=[row, row],
        out_shape=[jax.ShapeDtypeStruct((L, D_MODEL), F32), jax.ShapeDtypeStruct((L, D_MODEL), BF16)],
        scratch_shapes=[pltpu.VMEM((tm, D_MODEL), F32)],
        compiler_params=_cparams(("parallel", "arbitrary")),
        name="ffn_dense_ln",
    )(xb, wg[None], wu[None], wd[None], x, g.reshape(1, -1), b.reshape(1, -1))


def _start_row_gather(idx_ref, base, n, src_hbm, dst_ref, sem):
    def body(r, carry):
        pltpu.make_async_copy(src_hbm.at[pl.ds(idx_ref[base + r], 1)], dst_ref.at[pl.ds(r, 1)], sem).start()
        return carry

    lax.fori_loop(0, n, body, 0, unroll=8)


def _wait_row_gather(n, src_hbm, dst_ref, sem):
    pltpu.make_async_copy(src_hbm.at[pl.ds(0, n)], dst_ref.at[pl.ds(0, n)], sem).wait()


def _ffn_moe_kernel(exp_ref, used_ref, tok_ref, x_hbm, wg_ref, wu_ref, wd_ref, o_ref, stage_ref, xb_ref, sem,
                    *, per, chunk):
    b, f = pl.program_id(0), pl.program_id(1)
    n_b, n_f = pl.num_programs(0), pl.num_programs(1)
    tm = stage_ref.shape[0]
    used = used_ref[0]
    live = b < used

    @pl.when(jnp.logical_and(live, f == 0))
    def _():
        @pl.when(b == 0)
        def _():
            _start_row_gather(tok_ref, 0, tm, x_hbm, stage_ref, sem)

        _wait_row_gather(tm, x_hbm, stage_ref, sem)
        xb_ref[...] = stage_ref[...].astype(BF16)

    @pl.when(live)
    def _():
        base = jnp.minimum(b + 1, n_b - 1) * tm + f * per
        for j in range(per):
            pltpu.make_async_copy(x_hbm.at[pl.ds(tok_ref[base + j], 1)],
                                  stage_ref.at[pl.ds(f * per + j, 1)], sem).start()
        _swiglu_step(xb_ref, wg_ref, wu_ref, wd_ref, o_ref, chunk)

    @pl.when(jnp.logical_and(b == used - 1, f == n_f - 1))
    def _():
        _wait_row_gather(tm, x_hbm, stage_ref, sem)

    @pl.when(jnp.logical_and(jnp.logical_not(live), f == 0))
    def _():
        o_ref[...] = jnp.zeros(o_ref.shape, F32)


def _ffn_moe(x, buf_tok, blk_exp, n_used, wg, wu, wd, layer):
    P = buf_tok.shape[0]
    tm, tf = MOE_ROWS, MOE_FF_TILE
    nf = D_FF // tf
    assert tm % nf == 0 and tm % MOE_ROW_CHUNK == 0
    ff = lambda b, f, used: jnp.where(b < used[0], f, nf - 1)
    grid_spec = pltpu.PrefetchScalarGridSpec(
        num_scalar_prefetch=3,
        grid=(P // tm, nf),
        in_specs=[pl.BlockSpec(memory_space=pl.ANY),
                  pl.BlockSpec((None, 1, D_MODEL, tf), lambda b, f, e, u, t: (layer, e[b], 0, ff(b, f, u))),
                  pl.BlockSpec((None, 1, D_MODEL, tf), lambda b, f, e, u, t: (layer, e[b], 0, ff(b, f, u))),
                  pl.BlockSpec((None, 1, tf, D_MODEL), lambda b, f, e, u, t: (layer, e[b], ff(b, f, u), 0))],
        out_specs=pl.BlockSpec((tm, D_MODEL), lambda b, f, e, u, t: (b, 0)),
        scratch_shapes=[pltpu.VMEM((tm, D_MODEL), F32), pltpu.VMEM((tm, D_MODEL), BF16),
                        pltpu.SemaphoreType.DMA(())],
    )
    return pl.pallas_call(
        functools.partial(_ffn_moe_kernel, per=tm // nf, chunk=MOE_ROW_CHUNK),
        grid_spec=grid_spec,
        out_shape=jax.ShapeDtypeStruct((P, D_MODEL), F32),
        compiler_params=_cparams(("arbitrary", "arbitrary"), 56),
        name="ffn_moe",
    )(blk_exp, n_used, buf_tok, x, wg, wu, wd)


def _router_kernel(x_ref, r_ref, o_ref):
    xs = _split3(x_ref[...])
    rs = (r_ref[0], r_ref[1], r_ref[2])
    logits = None
    for i, j in ((0, 0), (0, 1), (1, 0), (1, 1), (0, 2), (2, 0)):
        t = jnp.dot(xs[i], rs[j], preferred_element_type=F32)
        logits = t if logits is None else logits + t
    lane = lax.broadcasted_iota(jnp.int32, logits.shape, 1)
    lane_f = lane.astype(F32)
    logits = jnp.where(lane < N_EXPERTS, logits, NEG_INF)
    m1 = jnp.max(logits, axis=1, keepdims=True)
    i1 = jnp.min(jnp.where(logits == m1, lane_f, 128.0), axis=1, keepdims=True)
    rest = jnp.where(lane_f == i1, NEG_INF, logits)
    m2 = jnp.max(rest, axis=1, keepdims=True)
    i2 = jnp.min(jnp.where(rest == m2, lane_f, 128.0), axis=1, keepdims=True)
    e2 = jnp.exp(m2 - m1)
    g1 = 1.0 / (1.0 + e2)
    g2 = e2 / (1.0 + e2)
    o_ref[...] = jnp.where(lane == 0, i1, jnp.where(lane == 1, i2, jnp.where(lane == 2, g1, jnp.where(lane == 3, g2, 0.0))))


def _router(x, router3):
    L = x.shape[0]
    return pl.pallas_call(
        _router_kernel,
        grid=(L // ROW_TILE,),
        in_specs=[pl.BlockSpec((ROW_TILE, D_MODEL), lambda i: (i, 0)),
                  pl.BlockSpec((3, D_MODEL, 128), lambda i: (0, 0, 0))],
        out_specs=pl.BlockSpec((ROW_TILE, 128), lambda i: (i, 0)),
        out_shape=jax.ShapeDtypeStruct((L, 128), F32),
        compiler_params=_cparams(("parallel",)),
        name="moe_router",
    )(x, router3)


def _combine_kernel(da_ref, db_ref, x_ref, gw_ref, g_ref, b_ref, y_hbm, o_ref, ob_ref, ya_ref, yb_ref, sem):
    i, n = pl.program_id(0), pl.num_programs(0)
    tm = x_ref.shape[0]
    slot = i % 2

    def start(step, slot):
        _start_row_gather(da_ref, step * tm, tm, y_hbm, ya_ref.at[slot], sem.at[slot])
        _start_row_gather(db_ref, step * tm, tm, y_hbm, yb_ref.at[slot], sem.at[slot])

    @pl.when(i == 0)
    def _():
        start(0, 0)

    @pl.when(i + 1 < n)
    def _():
        start(i + 1, 1 - slot)

    _wait_row_gather(tm, y_hbm, ya_ref.at[slot], sem.at[slot])
    _wait_row_gather(tm, y_hbm, yb_ref.at[slot], sem.at[slot])
    gw = gw_ref[...]
    f = gw[:, 2:3] * ya_ref[slot] + gw[:, 3:4] * yb_ref[slot]
    y = _layer_norm(ALPHA * x_ref[...] + f, g_ref[...], b_ref[...])
    o_ref[...] = y
    ob_ref[...] = y.astype(BF16)


def _combine(x, y_rows, dest_a, dest_b, route, g, b):
    L = x.shape[0]
    tm = COMBINE_ROWS
    row = pl.BlockSpec((tm, D_MODEL), lambda i, a, b: (i, 0))
    vec = pl.BlockSpec((1, D_MODEL), lambda i, a, b: (0, 0))
    grid_spec = pltpu.PrefetchScalarGridSpec(
        num_scalar_prefetch=2,
        grid=(L // tm,),
        in_specs=[row, pl.BlockSpec((tm, 128), lambda i, a, b: (i, 0)), vec, vec, pl.BlockSpec(memory_space=pl.ANY)],
        out_specs=[row, row],
        scratch_shapes=[pltpu.VMEM((2, tm, D_MODEL), F32), pltpu.VMEM((2, tm, D_MODEL), F32),
                        pltpu.SemaphoreType.DMA((2,))],
    )
    return pl.pallas_call(
        _combine_kernel,
        grid_spec=grid_spec,
        out_shape=[jax.ShapeDtypeStruct((L, D_MODEL), F32), jax.ShapeDtypeStruct((L, D_MODEL), BF16)],
        compiler_params=_cparams(("arbitrary",)),
        name="moe_combine_ln",
    )(dest_a, dest_b, x, route, g.reshape(1, -1), b.reshape(1, -1), y_rows)


def _moe(x, router, wg, wu, wd, layer, g, b):
    T = x.shape[0]
    r32 = jnp.pad(router, ((0, 0), (0, 128 - N_EXPERTS)))
    r1 = r32.astype(BF16)
    r2 = (r32 - r1.astype(F32)).astype(BF16)
    r3 = (r32 - r1.astype(F32) - r2.astype(F32)).astype(BF16)
    route = _router(x, jnp.stack([r1, r2, r3]))
    e_flat = route[:, :TOP_K].astype(jnp.int32).reshape(-1)
    onehot = (e_flat[:, None] == jnp.arange(N_EXPERTS, dtype=jnp.int32)[None, :]).astype(jnp.int32)
    csum = jnp.cumsum(onehot, axis=0)
    rank = jnp.sum(onehot * csum, axis=1) - 1
    counts = csum[-1]
    padded = (counts + MOE_ROWS - 1) // MOE_ROWS * MOE_ROWS
    pend = jnp.cumsum(padded)
    dest = (pend - padded)[e_flat] + rank
    n_blocks = -(-(T * TOP_K + N_EXPERTS * (MOE_ROWS - 1)) // MOE_ROWS)
    P = n_blocks * MOE_ROWS
    tok = jnp.repeat(jnp.arange(T, dtype=jnp.int32), TOP_K)
    buf_tok = jnp.zeros((P,), jnp.int32).at[dest].set(tok)
    n_used = (pend[-1] // MOE_ROWS).astype(jnp.int32).reshape(1)
    blk_start = jnp.arange(n_blocks, dtype=jnp.int32) * MOE_ROWS
    blk_exp = jnp.minimum(jnp.searchsorted(pend, jnp.minimum(blk_start, pend[-1] - 1), side='right'),
                          N_EXPERTS - 1).astype(jnp.int32)
    y_rows = _ffn_moe(x, buf_tok, blk_exp, n_used, wg, wu, wd, layer)
    d2 = dest.reshape(T, TOP_K)
    return _combine(x, y_rows, d2[:, 0], d2[:, 1], route, g, b)


def _rope_tables(L):
    inv_freq = 1.0 / (ROPE_THETA ** (jnp.arange(0, ROPE_DIM, 2, dtype=F32) / ROPE_DIM))
    ang = jnp.arange(L, dtype=F32)[:, None] * inv_freq[None, :]
    cos, sin = jnp.cos(ang), jnp.sin(ang)
    z = lambda n: jnp.zeros((L, n), F32)
    cosf = jnp.concatenate([cos, cos, jnp.ones((L, HEAD_DIM - ROPE_DIM), F32)], axis=1)
    sa = jnp.concatenate([-sin, z(HEAD_DIM - 16)], axis=1)
    sb = jnp.concatenate([z(16), sin, z(HEAD_DIM - ROPE_DIM)], axis=1)
    return cosf, sa, sb


def _w_in_pack_kernel(a_ref, b_ref, c_ref, o_ref):
    o_ref[:, :3328] = a_ref[...].astype(BF16)
    o_ref[:, 3328:3840] = b_ref[...].astype(BF16)
    o_ref[:, 3840:] = c_ref[...].astype(BF16)


def _reorder_w_in(w_all, layer):
    tail = w_all[layer, :, 3328:]
    gates = jnp.pad(tail[:, :12], ((0, 0), (0, 128 - 12)))
    rows = 256
    blk = lambda n: pl.BlockSpec((rows, n), lambda i: (i, 0))
    return pl.pallas_call(
        _w_in_pack_kernel,
        grid=(D_MODEL // rows,),
        in_specs=[pl.BlockSpec((None, rows, 3328), lambda i: (layer, i, 0)), blk(512), blk(128)],
        out_specs=blk(PROJ_WIDTH),
        out_shape=jax.ShapeDtypeStruct((D_MODEL, PROJ_WIDTH), BF16),
        compiler_params=_cparams(("parallel",)),
        name="w_in_pack",
    )(w_all, tail[:, 12:], gates)


def _trunk(x, mem, p):
    L = x.shape[0]
    ns, nc = L // NSA_SEL_BLOCK, L // NSA_CMP_STRIDE
    cosf, sa, sb = _rope_tables(L)
    lanes = -(-ns // HEAD_DIM) * HEAD_DIM
    cj = jnp.arange(nc)[:, None] - 4 * jnp.arange(lanes)[None, :]
    agg = jnp.logical_and(cj >= -1, cj <= 3).astype(BF16)
    ck = min(SEL_CHUNK, L)
    per = ck // NSA_SEL_BLOCK
    n_tag = min(HEAD_DIM // per, L // ck)
    blk_id = jnp.arange(n_tag)[:, None, None] * per + jnp.arange(ck)[None, :, None] // NSA_SEL_BLOCK
    tags = (blk_id == jnp.arange(HEAD_DIM)[None, None, :]).astype(BF16)
    mem_b = mem.astype(BF16)

    x, xb = _ln_in(x, p['ln_in_g'], p['ln_in_b'])
    for i in range(DEPTH):
        u, mq, mk, mv, nqc, nqr, cmp_kv, nkv, xq, gates = _proj(xb, _reorder_w_in(p['w_in'], i), cosf, sa, sb)
        bmat, cmat, pw = _s5_params(p['ssm_a_re'][i], p['ssm_a_im'][i], p['ssm_log_dt'][i], p['ssm_b_re'][i],
                                    p['ssm_b_im'][i], p['ssm_c_re'][i], p['ssm_c_im'][i])
        y_ssm = _s5(u, bmat, cmat, pw, p['ssm_d'][i], p['ssm_w_glu'][i])
        y_moba = _moba(mq, mk, mv, _kmean(mk))
        pos = jnp.stack([p['nsa_pos_k'][i].reshape(1, -1), p['nsa_pos_v'][i].reshape(1, -1)])
        pos = jnp.broadcast_to(pos, (2, 8, pos.shape[-1])).astype(BF16)
        kvc = _compress(cmp_kv, jnp.stack([p['nsa_ck1'][i], p['nsa_cv1'][i]]).astype(BF16),
                        jnp.stack([p['nsa_ck2'][i], p['nsa_cv2'][i]]).astype(BF16), pos)
        o_cmp, sel_bias = _nsa_cmp(nqc, kvc, agg)
        y_nsa = _nsa_sel(nqr, nkv, sel_bias, tags, o_cmp, gates)
        mem_kv = _mem_proj(mem_b, jnp.concatenate([p['mem_wk'][i], p['mem_wv'][i]], axis=1).astype(BF16))
        y_x = _xattn(xq, mem_kv)
        x, xb = _out_proj((y_ssm, y_moba, y_nsa, y_x), p['w_o'][i].astype(BF16), x, p['ln1_g'][i], p['ln1_b'][i])
        j = i // 2
        if i % 2 == 0:
            x, xb = _ffn_dense(xb, x, p['ffn_w_gate'][j].astype(BF16), p['ffn_w_up'][j].astype(BF16),
                               p['ffn_w_down'][j].astype(BF16), p['ln2_g'][i], p['ln2_b'][i])
        else:
            x, xb = _moe(x, p['moe_router'][j], p['moe_w_gate'], p['moe_w_up'], p['moe_w_down'], j,
                         p['ln2_g'][i], p['ln2_b'][i])
    return x


def kernel(x, mem, ln_in_g, ln_in_b, w_in, ssm_a_re, ssm_a_im, ssm_log_dt, ssm_b_re, ssm_b_im, ssm_c_re, ssm_c_im,
           ssm_d, ssm_w_glu, nsa_pos_k, nsa_pos_v, nsa_ck1, nsa_ck2, nsa_cv1, nsa_cv2, mem_wk, mem_wv, w_o, ln1_g,
           ln1_b, ln2_g, ln2_b, ffn_w_gate, ffn_w_up, ffn_w_down, moe_router, moe_w_gate, moe_w_up, moe_w_down):
    p = dict(ln_in_g=ln_in_g, ln_in_b=ln_in_b, w_in=w_in, ssm_a_re=ssm_a_re, ssm_a_im=ssm_a_im,
             ssm_log_dt=ssm_log_dt, ssm_b_re=ssm_b_re, ssm_b_im=ssm_b_im, ssm_c_re=ssm_c_re, ssm_c_im=ssm_c_im,
             ssm_d=ssm_d, ssm_w_glu=ssm_w_glu, nsa_pos_k=nsa_pos_k, nsa_pos_v=nsa_pos_v, nsa_ck1=nsa_ck1,
             nsa_ck2=nsa_ck2, nsa_cv1=nsa_cv1, nsa_cv2=nsa_cv2, mem_wk=mem_wk, mem_wv=mem_wv, w_o=w_o,
             ln1_g=ln1_g, ln1_b=ln1_b, ln2_g=ln2_g, ln2_b=ln2_b, ffn_w_gate=ffn_w_gate, ffn_w_up=ffn_w_up,
             ffn_w_down=ffn_w_down, moe_router=moe_router, moe_w_gate=moe_w_gate, moe_w_up=moe_w_up,
             moe_w_down=moe_w_down)
    assert x.shape[0] == 1 and mem.shape[0] == 1
    return _trunk(x[0], mem[0], p)[None]
```

```python
import functools
import math

import jax
import jax.numpy as jnp
from jax import lax
from jax.experimental import pallas as pl
from jax.experimental.pallas import tpu as pltpu

F32 = jnp.float32
BF16 = jnp.bfloat16

D_MODEL = 2048
DEPTH = 4
HEAD_DIM = 128
GROUP_WIDTH = 512
N_HEADS = 4
SSM_GROUP = 16
SSM_NG = 32
SSM_STATE = 64
SSM_NSTATE = SSM_NG * SSM_STATE
MOBA_BLOCK = 256
MOBA_TOPK = 3
NSA_CMP_LEN = 32
NSA_CMP_STRIDE = 16
NSA_SEL_BLOCK = 64
NSA_TOPK = 16
NSA_WINDOW = 512
ROPE_THETA = 500000.0
ROPE_DIM = 32
D_FF = 5632
N_EXPERTS = 8
TOP_K = 2
Q_BLOCK = 128
LN_EPS = 1e-5
NEG_INF = -1e30
FORCE_SCORE = 1e9
ALPHA = (2.0 * DEPTH) ** 0.25
SCALE = HEAD_DIM ** -0.5 * math.log2(math.e)

ROW_TILE = 512
SCAN_ROWS = 512
SCAN_LANES = 512
SEL_CHUNK = 512
SEL_STEP_CHUNKS = 4
MOBA_STEP_BLOCKS = 8
FF_TILE = 512
FF_ROW_CHUNK = 512
MOE_ROWS = 1056
MOE_ROW_CHUNK = 1056
MOE_FF_TILE = 256
COMBINE_ROWS = 256
PROJ_WIDTH = 3968

_NT = (((1,), (1,)), ((), ()))


def _cparams(sem, vmem_mb=48):
    return pltpu.CompilerParams(dimension_semantics=sem, vmem_limit_bytes=vmem_mb * 1024 * 1024)


def _resident(block_shape, index_map):
    return pl.BlockSpec(block_shape, index_map, pipeline_mode=pl.Buffered(1))


def _layer_norm(r, g, b):
    mu = jnp.mean(r, axis=-1, keepdims=True)
    d = r - mu
    var = jnp.mean(d * d, axis=-1, keepdims=True)
    return d * lax.rsqrt(var + LN_EPS) * g + b


def _gelu_tanh(x):
    return x * (0.5 * (1.0 + jnp.tanh(math.sqrt(2.0 / math.pi) * (x + 0.044715 * (x * x * x)))))


def _sigmoid(x):
    return 1.0 / (1.0 + jnp.exp(-x))


def _split3(a):
    a1 = a.astype(BF16)
    r1 = a - a1.astype(F32)
    a2 = r1.astype(BF16)
    a3 = (r1 - a2.astype(F32)).astype(BF16)
    return a1, a2, a3


def _topk_mask(score, k):
    ids = lax.broadcasted_iota(jnp.int32, score.shape, 1).astype(F32)

    def body(_, sc):
        m = jnp.max(sc, axis=1, keepdims=True)
        idx = jnp.min(jnp.where(sc == m, ids, float(score.shape[1])), axis=1, keepdims=True)
        return jnp.where(ids == idx, NEG_INF, sc)

    left = lax.fori_loop(0, k, body, score)
    return jnp.where(jnp.logical_and(score > 0.5 * NEG_INF, left <= 0.5 * NEG_INF), 1.0, 0.0)


def _ln_in_kernel(x_ref, g_ref, b_ref, o_ref, ob_ref):
    y = _layer_norm(x_ref[...], g_ref[...], b_ref[...])
    o_ref[...] = y
    ob_ref[...] = y.astype(BF16)


def _ln_in(x, g, b):
    L = x.shape[0]
    row = pl.BlockSpec((ROW_TILE, D_MODEL), lambda i: (i, 0))
    vec = pl.BlockSpec((1, D_MODEL), lambda i: (0, 0))
    return pl.pallas_call(
        _ln_in_kernel,
        grid=(L // ROW_TILE,),
        in_specs=[row, vec, vec],
        out_specs=[row, row],
        out_shape=[jax.ShapeDtypeStruct((L, D_MODEL), F32), jax.ShapeDtypeStruct((L, D_MODEL), BF16)],
        compiler_params=_cparams(("parallel",)),
        name="ln_in",
    )(x, g.reshape(1, -1), b.reshape(1, -1))


def _proj_kernel(x_ref, w_ref, cos_ref, sa_ref, sb_ref,
                 u_ref, mq_ref, mk_ref, mv_ref, nqc_ref, nqr_ref, cmp_ref, nkv_ref, xq_ref, g_ref):
    x = x_ref[...]
    cos, sa, sb = cos_ref[...], sa_ref[...], sb_ref[...]

    def dot(c0, n):
        return jnp.dot(x, w_ref[:, c0:c0 + n], preferred_element_type=F32)

    def rope(t):
        return t * cos + pltpu.roll(t, HEAD_DIM - 16, 1) * sa + pltpu.roll(t, 16, 1) * sb

    heads = [slice(h * HEAD_DIM, (h + 1) * HEAD_DIM) for h in range(N_HEADS)]
    u_ref[...] = dot(0, 512)
    t = dot(512, 512)
    for cs in heads:
        mq_ref[:, cs] = (rope(t[:, cs]) * SCALE).astype(BF16)
    t = dot(1024, 512)
    for cs in heads:
        mk_ref[:, cs] = rope(t[:, cs]).astype(BF16)
    mv_ref[...] = dot(1536, 512).astype(BF16)
    t = dot(2048, 512)
    nqc_ref[...] = (t * SCALE).astype(BF16)
    for cs in heads:
        nqr_ref[:, cs] = (rope(t[:, cs]) * SCALE).astype(BF16)
    t = dot(2560, 512)
    cmp_ref[0] = t[:, heads[0]].astype(BF16)
    cmp_ref[1] = t[:, heads[1]].astype(BF16)
    nkv_ref[:, heads[0]] = rope(t[:, heads[2]]).astype(BF16)
    nkv_ref[:, heads[1]] = t[:, heads[3]].astype(BF16)
    t = dot(3072, 256)
    nkv_ref[:, heads[2]] = rope(t[:, heads[0]]).astype(BF16)
    nkv_ref[:, heads[3]] = t[:, heads[1]].astype(BF16)
    t = dot(3328, 640)
    xq_ref[...] = (t[:, :512] * SCALE).astype(BF16)
    g_ref[...] = _sigmoid(t[:, 512:])


def _proj(xb, w, cosf, sa, sb):
    L = xb.shape[0]
    tm = ROW_TILE
    row = lambda n: pl.BlockSpec((tm, n), lambda i: (i, 0))
    outs = [
        (jax.ShapeDtypeStruct((L, 512), F32), row(512)),
        (jax.ShapeDtypeStruct((L, 512), BF16), row(512)),
        (jax.ShapeDtypeStruct((L, 512), BF16), row(512)),
        (jax.ShapeDtypeStruct((L, 512), BF16), row(512)),
        (jax.ShapeDtypeStruct((L, 512), BF16), row(512)),
        (jax.ShapeDtypeStruct((L, 512), BF16), row(512)),
        (jax.ShapeDtypeStruct((2, L, 128), BF16), pl.BlockSpec((2, tm, 128), lambda i: (0, i, 0))),
        (jax.ShapeDtypeStruct((L, 512), BF16), row(512)),
        (jax.ShapeDtypeStruct((L, 512), BF16), row(512)),
        (jax.ShapeDtypeStruct((L, 128), F32), row(128)),
    ]
    return pl.pallas_call(
        _proj_kernel,
        grid=(L // tm,),
        in_specs=[row(D_MODEL), _resident((D_MODEL, PROJ_WIDTH), lambda i: (0, 0)),
                  row(128), row(128), row(128)],
        out_specs=[o[1] for o in outs],
        out_shape=[o[0] for o in outs],
        compiler_params=_cparams(("parallel",), 56),
        name="in_proj",
    )(xb, w, cosf, sa, sb)


def _s5_kernel(u_ref, bmat_ref, cmat_ref, pw_ref, d_ref, wglu_ref, o_ref, hr_ref, hi_ref, car_ref, cai_ref):
    T = u_ref.shape[0]
    W = SCAN_LANES

    @pl.when(pl.program_id(0) == 0)
    def _():
        car_ref[...] = jnp.zeros(car_ref.shape, F32)
        cai_ref[...] = jnp.zeros(cai_ref.shape, F32)

    u = u_ref[...]
    ub = u.astype(BF16)
    for b in range(4):
        bu = jnp.dot(ub[:, b * 128:(b + 1) * 128], bmat_ref[b], preferred_element_type=F32)
        hr_ref[:, b * 512:(b + 1) * 512] = bu[:, :512]
        hi_ref[:, b * 512:(b + 1) * 512] = bu[:, 512:]

    for cb in range(SSM_NSTATE // W):
        sl = slice(cb * W, (cb + 1) * W)
        steps = [(pw_ref[2 * j, :, sl], pw_ref[2 * j + 1, :, sl], 1 << j) for j in range(3)]
        p_r, p_i = pw_ref[6, :, sl], pw_ref[7, :, sl]

        def tile(t, carry, sl=sl, steps=steps, p_r=p_r, p_i=p_i):
            c_r, c_i = carry
            rows = pl.ds(pl.multiple_of(t * 8, 8), 8)
            x_r, x_i = hr_ref[rows, sl], hi_ref[rows, sl]
            for l_r, l_i, d in steps:
                s_r, s_i = pltpu.roll(x_r, d, 0), pltpu.roll(x_i, d, 0)
                x_r, x_i = x_r + l_r * s_r - l_i * s_i, x_i + l_r * s_i + l_i * s_r
            x_r, x_i = x_r + p_r * c_r - p_i * c_i, x_i + p_r * c_i + p_i * c_r
            hr_ref[rows, sl] = x_r
            hi_ref[rows, sl] = x_i
            return (jnp.broadcast_to(x_r[7:8, :], (8, W)), jnp.broadcast_to(x_i[7:8, :], (8, W)))

        c_r, c_i = lax.fori_loop(0, T // 8, tile, (car_ref[:, sl], cai_ref[:, sl]))
        car_ref[:, sl] = c_r
        cai_ref[:, sl] = c_i

    ys = []
    for b in range(4):
        h_r = hr_ref[:, b * 512:(b + 1) * 512].astype(BF16)
        h_i = hi_ref[:, b * 512:(b + 1) * 512].astype(BF16)
        ys.append(jnp.dot(h_r, cmat_ref[b, :512, :], preferred_element_type=F32)
                  + jnp.dot(h_i, cmat_ref[b, 512:, :], preferred_element_type=F32))
    y = jnp.concatenate(ys, axis=1) + d_ref[...] * u
    z = _gelu_tanh(y)
    gate = _sigmoid(jnp.dot(z.astype(BF16), wglu_ref[...], preferred_element_type=F32))
    o_ref[...] = (z * gate).astype(BF16)


def _s5_params(a_re, a_im, log_dt, b_re, b_im, c_re, c_im):
    dt = jnp.exp(log_dt)[:, None]
    mag = jnp.exp(a_re * dt)
    lam_re, lam_im = mag * jnp.cos(a_im * dt), mag * jnp.sin(a_im * dt)
    den = a_re * a_re + a_im * a_im
    nr, ni = lam_re - 1.0, lam_im
    coef_re = (nr * a_re + ni * a_im) / den
    coef_im = (ni * a_re - nr * a_im) / den
    bbar_re = coef_re[..., None] * b_re - coef_im[..., None] * b_im
    bbar_im = coef_re[..., None] * b_im + coef_im[..., None] * b_re
    eye = jnp.eye(8, dtype=F32)
    to_b = lambda m: jnp.einsum('bgnc,gh->bgchn', m.reshape(4, 8, SSM_STATE, SSM_GROUP), eye).reshape(4, 128, 512)
    bmat = jnp.concatenate([to_b(bbar_re), to_b(bbar_im)], axis=-1).astype(BF16)
    to_c = lambda m: jnp.einsum('bgcn,gh->bgnhc', m.reshape(4, 8, SSM_GROUP, SSM_STATE), eye).reshape(4, 512, 128)
    cmat = jnp.concatenate([to_c(c_re), -to_c(c_im)], axis=1).astype(BF16)
    lr, li = lam_re.reshape(-1), lam_im.reshape(-1)
    pr, pi = [lr], [li]
    for _ in range(7):
        pr, pi = pr + [pr[-1] * lr - pi[-1] * li], pi + [pr[-1] * li + pi[-1] * lr]
    row = jnp.arange(8)[:, None]
    tabs = []
    for d in (1, 2, 4):
        tabs += [jnp.where(row >= d, pr[d - 1][None, :], 0.0), jnp.where(row >= d, pi[d - 1][None, :], 0.0)]
    tabs += [jnp.stack(pr, axis=0), jnp.stack(pi, axis=0)]
    return bmat, cmat, jnp.stack(tabs, axis=0)


def _s5(u, bmat, cmat, pw, d_skip, w_glu):
    L = u.shape[0]
    T = SCAN_ROWS
    row = pl.BlockSpec((T, 512), lambda i: (i, 0))
    full = lambda shape: pl.BlockSpec(shape, lambda i: (0,) * len(shape))
    return pl.pallas_call(
        _s5_kernel,
        grid=(L // T,),
        in_specs=[row, full((4, 128, 1024)), full((4, 1024, 128)), full((8, 8, SSM_NSTATE)),
                  full((1, 512)), full((512, 512))],
        out_specs=row,
        out_shape=jax.ShapeDtypeStruct((L, 512), BF16),
        scratch_shapes=[pltpu.VMEM((T, SSM_NSTATE), F32), pltpu.VMEM((T, SSM_NSTATE), F32),
                        pltpu.VMEM((8, SSM_NSTATE), F32), pltpu.VMEM((8, SSM_NSTATE), F32)],
        compiler_params=_cparams(("arbitrary",)),
        name="s5",
    )(u, bmat, cmat, pw, d_skip.reshape(1, -1), w_glu.astype(BF16))


def _kmean_kernel(k_ref, o_ref):
    rows = k_ref.shape[0]
    k = k_ref[...].astype(F32).reshape(rows // MOBA_BLOCK, MOBA_BLOCK, GROUP_WIDTH)
    o_ref[...] = jnp.mean(k, axis=1).astype(BF16)


def _kmean(k):
    L = k.shape[0]
    rows = 8 * MOBA_BLOCK if L % (8 * MOBA_BLOCK) == 0 else L
    return pl.pallas_call(
        _kmean_kernel,
        grid=(L // rows,),
        in_specs=[pl.BlockSpec((rows, GROUP_WIDTH), lambda i: (i, 0))],
        out_specs=pl.BlockSpec((rows // MOBA_BLOCK, GROUP_WIDTH), lambda i: (i, 0)),
        out_shape=jax.ShapeDtypeStruct((L // MOBA_BLOCK, GROUP_WIDTH), BF16),
        compiler_params=_cparams(("parallel",)),
        name="moba_kmean",
    )(k)


def _with_ones(v):
    return jnp.concatenate([v, jnp.ones(v.shape, v.dtype)], axis=1)


def _flash_step(s, v_ones, m_ref, acc_ref, idx, first):
    row_max = jnp.max(s, axis=1, keepdims=True)
    if first:
        m_new = jnp.broadcast_to(row_max, (s.shape[0], HEAD_DIM))
    else:
        m_old = m_ref[idx]
        m_new = jnp.maximum(m_old, row_max)
    p = jnp.concatenate([jnp.exp2(s[:, c:c + HEAD_DIM] - m_new) for c in range(0, s.shape[1], HEAD_DIM)], axis=1)
    pv = jnp.dot(p.astype(BF16), v_ones, preferred_element_type=F32)
    if first:
        acc_ref[idx] = pv
    else:
        a = jnp.exp2(m_old - m_new)
        acc_ref[idx] = jnp.concatenate([a, a], axis=1) * acc_ref[idx] + pv
    m_ref[idx] = m_new


def _flash_result(acc_ref, idx):
    acc = acc_ref[idx]
    return acc[:, :HEAD_DIM] / acc[:, HEAD_DIM:]


def _moba_kernel(q_ref, k_ref, v_ref, km_ref, o_ref, qa_ref, m_ref, acc_ref):
    TQ = q_ref.shape[0]
    own = pl.program_id(0)
    heads = [slice(h * HEAD_DIM, (h + 1) * HEAD_DIM) for h in range(N_HEADS)]

    gs = jnp.concatenate([lax.dot_general(q_ref[:, cs], km_ref[:, cs], _NT, preferred_element_type=F32)
                          for cs in heads], axis=0)
    blk = lax.broadcasted_iota(jnp.int32, gs.shape, 1)
    sel = _topk_mask(jnp.where(blk < own, gs, NEG_INF), MOBA_TOPK)
    bias = jnp.where(sel > 0.5, 0.0, NEG_INF).astype(BF16)
    for h, cs in enumerate(heads):
        qa_ref[h] = jnp.concatenate([q_ref[:, cs], bias[h * TQ:(h + 1) * TQ]], axis=1)

    own_rows = pl.ds(pl.multiple_of(own * MOBA_BLOCK, MOBA_BLOCK), MOBA_BLOCK)
    causal = (lax.broadcasted_iota(jnp.int32, (TQ, MOBA_BLOCK), 1)
              <= lax.broadcasted_iota(jnp.int32, (TQ, MOBA_BLOCK), 0))
    for h, cs in enumerate(heads):
        s = lax.dot_general(q_ref[:, cs], k_ref[own_rows, cs], _NT, preferred_element_type=F32)
        _flash_step(jnp.where(causal, s, NEG_INF), _with_ones(v_ref[own_rows, cs]), m_ref, acc_ref, h, True)

    def past(first_block, n_blocks):
        n = n_blocks * MOBA_BLOCK
        rows = pl.ds(pl.multiple_of(first_block * MOBA_BLOCK, MOBA_BLOCK), n)
        blk = first_block + lax.broadcasted_iota(jnp.int32, (n, HEAD_DIM), 0) // MOBA_BLOCK
        tag = jnp.where(lax.broadcasted_iota(jnp.int32, (n, HEAD_DIM), 1) == blk, 1.0, 0.0).astype(BF16)
        for h, cs in enumerate(heads):
            k_aug = jnp.concatenate([k_ref[rows, cs], tag], axis=1)
            s = lax.dot_general(qa_ref[h], k_aug, _NT, preferred_element_type=F32)
            _flash_step(s, _with_ones(v_ref[rows, cs]), m_ref, acc_ref, h, False)

    def group(t, carry):
        past(MOBA_STEP_BLOCKS * t, MOBA_STEP_BLOCKS)
        return carry

    lax.fori_loop(0, own // MOBA_STEP_BLOCKS, group, 0)
    size = MOBA_STEP_BLOCKS // 2
    while size >= 1:
        @pl.when((own & size) != 0)
        def _(size=size):
            past(own & ~(2 * size - 1), size)
        size //= 2

    for h, cs in enumerate(heads):
        o_ref[:, cs] = _flash_result(acc_ref, h).astype(BF16)


def _moba(q, k, v, kmean):
    L = q.shape[0]
    TQ = MOBA_BLOCK
    nb = L // MOBA_BLOCK
    assert nb <= HEAD_DIM
    km = jnp.pad(kmean, ((0, HEAD_DIM - nb), (0, 0)))
    tile = pl.BlockSpec((TQ, GROUP_WIDTH), lambda i: (i, 0))
    return pl.pallas_call(
        _moba_kernel,
        grid=(L // TQ,),
        in_specs=[tile, _resident((L, GROUP_WIDTH), lambda i: (0, 0)), _resident((L, GROUP_WIDTH), lambda i: (0, 0)),
                  _resident((HEAD_DIM, GROUP_WIDTH), lambda i: (0, 0))],
        out_specs=tile,
        out_shape=jax.ShapeDtypeStruct((L, GROUP_WIDTH), BF16),
        scratch_shapes=[pltpu.VMEM((N_HEADS, TQ, 2 * HEAD_DIM), BF16), pltpu.VMEM((N_HEADS, TQ, HEAD_DIM), F32),
                        pltpu.VMEM((N_HEADS, TQ, 2 * HEAD_DIM), F32)],
        compiler_params=_cparams(("parallel",)),
        name="moba",
    )(q, k, v, km)


def _compress_kernel(c_ref, w1_ref, w2_ref, pos_ref, o_ref):
    n = c_ref.shape[1]
    half = w1_ref.shape[1] // 2
    c = c_ref[0]
    a = jnp.dot(c, w1_ref[0, :half, :], preferred_element_type=F32)
    b = jnp.dot(c, w1_ref[0, half:, :], preferred_element_type=F32)
    bias = jnp.dot(pos_ref[0], w1_ref[0], preferred_element_type=F32)[0:1, :]
    hid = _gelu_tanh(a + pltpu.roll(b, n - 1, 0) + bias)
    out = jnp.dot(hid.astype(BF16), w2_ref[0], preferred_element_type=F32)
    last = lax.broadcasted_iota(jnp.int32, out.shape, 0) == n - 1
    o_ref[0] = jnp.where(last, 0.0, out).astype(BF16)


def _compress(cmp_kv, w1, w2, pos):
    L = cmp_kv.shape[1]
    n = L // NSA_CMP_STRIDE
    wide = NSA_CMP_STRIDE * HEAD_DIM
    chunks = cmp_kv.reshape(2, n, wide)
    hidden = w1.shape[-1]
    return pl.pallas_call(
        _compress_kernel,
        grid=(2,),
        in_specs=[pl.BlockSpec((1, n, wide), lambda j: (j, 0, 0)),
                  pl.BlockSpec((1, 2 * wide, hidden), lambda j: (j, 0, 0)),
                  pl.BlockSpec((1, hidden, HEAD_DIM), lambda j: (j, 0, 0)),
                  pl.BlockSpec((1, 8, 2 * wide), lambda j: (j, 0, 0))],
        out_specs=pl.BlockSpec((1, n, HEAD_DIM), lambda j: (j, 0, 0)),
        out_shape=jax.ShapeDtypeStruct((2, n, HEAD_DIM), BF16),
        compiler_params=_cparams(("parallel",)),
        name="nsa_compress",
    )(chunks, w1, w2, pos)


def _stack_heads(ref):
    return jnp.concatenate([ref[:, h * HEAD_DIM:(h + 1) * HEAD_DIM] for h in range(N_HEADS)], axis=0)


def _nsa_cmp_kernel(q_ref, kc_ref, vc_ref, agg_ref, o_ref, sel_ref):
    TQ = q_ref.shape[0]
    nc = kc_ref.shape[1]
    ns = nc * NSA_CMP_STRIDE // NSA_SEL_BLOCK
    lanes = agg_ref.shape[1]
    start = pl.program_id(0) * TQ
    qs = _stack_heads(q_ref)

    def run(cols, width):
        s = lax.dot_general(qs, kc_ref[0, :cols, :], _NT, preferred_element_type=F32)
        cend = lax.broadcasted_iota(jnp.int32, s.shape, 1) * NSA_CMP_STRIDE + (NSA_CMP_LEN - 1)
        ok = cend <= start + (lax.broadcasted_iota(jnp.int32, s.shape, 0) & (TQ - 1))
        s = jnp.where(ok, s, NEG_INF)
        m = jnp.max(s, axis=1, keepdims=True)
        e = jnp.where(ok, jnp.exp2(s - m), 0.0)
        p = e / jnp.maximum(jnp.sum(e, axis=1, keepdims=True), 1e-30)
        oc = jnp.dot(p.astype(BF16), vc_ref[0, :cols, :], preferred_element_type=F32)
        for h in range(N_HEADS):
            o_ref[:, h * HEAD_DIM:(h + 1) * HEAD_DIM] = oc[h * TQ:(h + 1) * TQ, :]

        imp = p[0:TQ] + p[TQ:2 * TQ] + p[2 * TQ:3 * TQ] + p[3 * TQ:4 * TQ]
        agg = agg_ref[:cols, :width]
        imp_sel = sum(jnp.dot(t, agg, preferred_element_type=F32) for t in _split3(imp))
        blk = lax.broadcasted_iota(jnp.int32, imp_sel.shape, 1)
        own = (start + lax.broadcasted_iota(jnp.int32, imp_sel.shape, 0)) // NSA_SEL_BLOCK
        score = jnp.where(blk == own, FORCE_SCORE, jnp.where(blk < own, imp_sel, NEG_INF))
        sel = _topk_mask(score, min(NSA_TOPK, ns))
        sel_ref[:, :width] = jnp.where(sel > 0.5, 0.0, NEG_INF).astype(BF16)
        if width < lanes:
            sel_ref[:, width:] = jnp.full((TQ, lanes - width), NEG_INF, BF16)

    parts = 4 if nc % (4 * HEAD_DIM) == 0 else 1
    quarter = (start + TQ - 1) // (nc * NSA_CMP_STRIDE // parts)
    for k in range(parts):
        cols = (k + 1) * nc // parts
        width = min(lanes, -(-((k + 1) * ns // parts) // HEAD_DIM) * HEAD_DIM)
        pl.when(quarter == k)(functools.partial(run, cols, width))


def _nsa_cmp(nq, kvc, agg):
    L = nq.shape[0]
    TQ = 2 * Q_BLOCK
    nc = kvc.shape[1]
    lanes = agg.shape[1]
    return pl.pallas_call(
        _nsa_cmp_kernel,
        grid=(L // TQ,),
        in_specs=[pl.BlockSpec((TQ, GROUP_WIDTH), lambda i: (i, 0)),
                  pl.BlockSpec((1, nc, HEAD_DIM), lambda i: (0, 0, 0)),
                  pl.BlockSpec((1, nc, HEAD_DIM), lambda i: (1, 0, 0)),
                  pl.BlockSpec((nc, lanes), lambda i: (0, 0))],
        out_specs=[pl.BlockSpec((TQ, GROUP_WIDTH), lambda i: (i, 0)),
                   pl.BlockSpec((TQ, lanes), lambda i: (i, 0))],
        out_shape=[jax.ShapeDtypeStruct((L, GROUP_WIDTH), F32), jax.ShapeDtypeStruct((L, lanes), BF16)],
        compiler_params=_cparams(("parallel",)),
        name="nsa_cmp",
    )(nq, kvc, kvc, agg)


def _nsa_sel_kernel(q_ref, ks_ref, vs_ref, kw_ref, vw_ref, sb_ref, tag_ref, oc_ref, g_ref, o_ref,
                    qa_ref, m_ref, acc_ref):
    TQ = q_ref.shape[0]
    CK = tag_ref.shape[1]
    cpg = HEAD_DIM // (CK // NSA_SEL_BLOCK)
    R = N_HEADS * TQ
    halves = (slice(0, R // 2), slice(R // 2, R))
    start = pl.program_id(0) * TQ
    qs = _stack_heads(q_ref)
    qpos = start + (lax.broadcasted_iota(jnp.int32, (R, 1), 0) & (TQ - 1))

    bias = sb_ref[...]
    for g in range(sb_ref.shape[1] // HEAD_DIM):
        qa_ref[g] = jnp.concatenate(
            [qs, jnp.concatenate([bias[:, g * HEAD_DIM:(g + 1) * HEAD_DIM]] * N_HEADS, axis=0)], axis=1)

    def scores(c, n):
        rows = pl.ds(pl.multiple_of(c * CK, CK), n * CK)
        tags = tag_ref[pl.ds(c % cpg, n)].reshape(n * CK, HEAD_DIM)
        k_aug = jnp.concatenate([ks_ref[rows, :], tags], axis=1)
        return rows, [lax.dot_general(qa_ref[c // cpg, hs, :], k_aug, _NT, preferred_element_type=F32)
                      for hs in halves]

    c_last = start // CK
    rows, ss = scores(c_last, 1)
    kpos = c_last * CK + lax.broadcasted_iota(jnp.int32, ss[0].shape, 1)
    v_ones = _with_ones(vs_ref[rows, :])
    for idx, hs in enumerate(halves):
        _flash_step(jnp.where(kpos <= qpos[hs], ss[idx], NEG_INF), v_ones, m_ref, acc_ref, idx, True)

    def past(c, n):
        rows, ss = scores(c, n)
        v_ones = _with_ones(vs_ref[rows, :])
        for idx in range(2):
            _flash_step(ss[idx], v_ones, m_ref, acc_ref, idx, False)

    def group(t, carry):
        past(SEL_STEP_CHUNKS * t, SEL_STEP_CHUNKS)
        return carry

    lax.fori_loop(0, c_last // SEL_STEP_CHUNKS, group, 0)
    size = SEL_STEP_CHUNKS // 2
    while size >= 1:
        @pl.when((c_last & size) != 0)
        def _(size=size):
            past(c_last & ~(2 * size - 1), size)
        size //= 2
    o_sel = jnp.concatenate([_flash_result(acc_ref, 0), _flash_result(acc_ref, 1)], axis=0)

    span = TQ + NSA_WINDOW
    w0 = jnp.maximum(start - NSA_WINDOW, 0)
    rows = pl.ds(pl.multiple_of(w0, TQ), span)
    s = lax.dot_general(qs, kw_ref[rows, :], _NT, preferred_element_type=F32)
    wpos = w0 + lax.broadcasted_iota(jnp.int32, s.shape, 1)
    s = jnp.where(jnp.logical_and(wpos <= qpos, wpos > qpos - NSA_WINDOW), s, NEG_INF)
    p = jnp.exp2(s - jnp.max(s, axis=1, keepdims=True))
    o_win = jnp.dot(p.astype(BF16), vw_ref[rows, :], preferred_element_type=F32) / jnp.sum(p, axis=1, keepdims=True)

    g = g_ref[...]
    for h in range(N_HEADS):
        hs = slice(h * TQ, (h + 1) * TQ)
        cs = slice(h * HEAD_DIM, (h + 1) * HEAD_DIM)
        o_ref[:, cs] = (g[:, 3 * h:3 * h + 1] * oc_ref[:, cs] + g[:, 3 * h + 1:3 * h + 2] * o_sel[hs]
                        + g[:, 3 * h + 2:3 * h + 3] * o_win[hs]).astype(BF16)


def _nsa_sel(nq_rope, nkv, sel_bias, tags, o_cmp, gates):
    L = nq_rope.shape[0]
    TQ = Q_BLOCK
    lanes = sel_bias.shape[1]
    R = N_HEADS * TQ
    col = lambda j: _resident((L, HEAD_DIM), lambda i, j=j: (0, j))
    return pl.pallas_call(
        _nsa_sel_kernel,
        grid=(L // TQ,),
        in_specs=[pl.BlockSpec((TQ, GROUP_WIDTH), lambda i: (i, 0)),
                  col(0), col(1), col(2), col(3),
                  pl.BlockSpec((TQ, lanes), lambda i: (i, 0)),
                  _resident(tags.shape, lambda i: (0, 0, 0)),
                  pl.BlockSpec((TQ, GROUP_WIDTH), lambda i: (i, 0)),
                  pl.BlockSpec((TQ, HEAD_DIM), lambda i: (i, 0))],
        out_specs=pl.BlockSpec((TQ, GROUP_WIDTH), lambda i: (i, 0)),
        out_shape=jax.ShapeDtypeStruct((L, GROUP_WIDTH), BF16),
        scratch_shapes=[pltpu.VMEM((lanes // HEAD_DIM, R, 2 * HEAD_DIM), BF16), pltpu.VMEM((2, R // 2, HEAD_DIM), F32),
                        pltpu.VMEM((2, R // 2, 2 * HEAD_DIM), F32)],
        compiler_params=_cparams(("parallel",)),
        name="nsa_sel_win",
    )(nq_rope, nkv, nkv, nkv, nkv, sel_bias, tags, o_cmp, gates)


def _matmul_kernel(a_ref, b_ref, o_ref):
    o_ref[...] = jnp.dot(a_ref[...], b_ref[...], preferred_element_type=F32).astype(o_ref.dtype)


def _mem_proj(mem_b, w):
    n, width = mem_b.shape[0], w.shape[1]
    return pl.pallas_call(
        _matmul_kernel,
        grid=(width // 512,),
        in_specs=[pl.BlockSpec((n, D_MODEL), lambda j: (0, 0)), pl.BlockSpec((D_MODEL, 512), lambda j: (0, j))],
        out_specs=pl.BlockSpec((n, 512), lambda j: (0, j)),
        out_shape=jax.ShapeDtypeStruct((n, width), BF16),
        compiler_params=_cparams(("parallel",)),
        name="mem_proj",
    )(mem_b, w)


def _xattn_kernel(q_ref, k_ref, v_ref, o_ref):
    for h in range(N_HEADS):
        cs = slice(h * HEAD_DIM, (h + 1) * HEAD_DIM)
        s = lax.dot_general(q_ref[:, cs], k_ref[:, cs], _NT, preferred_element_type=F32)
        p = jnp.exp2(s - jnp.max(s, axis=1, keepdims=True))
        o = jnp.dot(p.astype(BF16), v_ref[:, cs], preferred_element_type=F32)
        o_ref[:, cs] = (o / jnp.sum(p, axis=1, keepdims=True)).astype(BF16)


def _xattn(xq, mem_kv):
    L = xq.shape[0]
    n = mem_kv.shape[0]
    row = pl.BlockSpec((ROW_TILE, GROUP_WIDTH), lambda i: (i, 0))
    return pl.pallas_call(
        _xattn_kernel,
        grid=(L // ROW_TILE,),
        in_specs=[row, pl.BlockSpec((n, GROUP_WIDTH), lambda i: (0, 0)),
                  pl.BlockSpec((n, GROUP_WIDTH), lambda i: (0, 1))],
        out_specs=row,
        out_shape=jax.ShapeDtypeStruct((L, GROUP_WIDTH), BF16),
        compiler_params=_cparams(("parallel",)),
        name="xattn",
    )(xq, mem_kv, mem_kv)


def _out_proj_kernel(y0_ref, y1_ref, y2_ref, y3_ref, w_ref, x_ref, g_ref, b_ref, o_ref, ob_ref):
    acc = ALPHA * x_ref[...]
    for j, y_ref in enumerate((y0_ref, y1_ref, y2_ref, y3_ref)):
        acc = acc + jnp.dot(y_ref[...], w_ref[j * GROUP_WIDTH:(j + 1) * GROUP_WIDTH, :], preferred_element_type=F32)
    y = _layer_norm(acc, g_ref[...], b_ref[...])
    o_ref[...] = y
    ob_ref[...] = y.astype(BF16)


def _out_proj(ys, w_o, x, g, b):
    L = x.shape[0]
    tm = ROW_TILE
    part = pl.BlockSpec((tm, GROUP_WIDTH), lambda i: (i, 0))
    row = pl.BlockSpec((tm, D_MODEL), lambda i: (i, 0))
    vec = pl.BlockSpec((1, D_MODEL), lambda i: (0, 0))
    return pl.pallas_call(
        _out_proj_kernel,
        grid=(L // tm,),
        in_specs=[part, part, part, part, _resident((D_MODEL, D_MODEL), lambda i: (0, 0)), row, vec, vec],
        out_specs=[row, row],
        out_shape=[jax.ShapeDtypeStruct((L, D_MODEL), F32), jax.ShapeDtypeStruct((L, D_MODEL), BF16)],
        compiler_params=_cparams(("parallel",)),
        name="out_proj_ln",
    )(*ys, w_o, x, g.reshape(1, -1), b.reshape(1, -1))


def _swiglu_step(x_ref, wg_ref, wu_ref, wd_ref, acc_ref, chunk):
    @pl.when(pl.program_id(1) == 0)
    def _():
        acc_ref[...] = jnp.zeros(acc_ref.shape, F32)

    wg, wu, wd = wg_ref[0].astype(BF16), wu_ref[0].astype(BF16), wd_ref[0].astype(BF16)
    for r in range(0, x_ref.shape[0], chunk):
        rows = slice(r, r + chunk)
        x = x_ref[rows, :]
        gate = jnp.dot(x, wg, preferred_element_type=F32)
        up = jnp.dot(x, wu, preferred_element_type=F32)
        hid = (gate * _sigmoid(gate) * up).astype(BF16)
        acc_ref[rows, :] += jnp.dot(hid, wd, preferred_element_type=F32)


def _ffn_dense_kernel(x_ref, wg_ref, wu_ref, wd_ref, xr_ref, g_ref, b_ref, o_ref, ob_ref, acc_ref):
    _swiglu_step(x_ref, wg_ref, wu_ref, wd_ref, acc_ref, FF_ROW_CHUNK)

    @pl.when(pl.program_id(1) == pl.num_programs(1) - 1)
    def _():
        y = _layer_norm(ALPHA * xr_ref[...] + acc_ref[...], g_ref[...], b_ref[...])
        o_ref[...] = y
        ob_ref[...] = y.astype(BF16)


def _ffn_dense(xb, x, wg, wu, wd, g, b):
    L = x.shape[0]
    tm, tf = ROW_TILE, FF_TILE
    row = pl.BlockSpec((tm, D_MODEL), lambda i, f: (i, 0))
    vec = pl.BlockSpec((1, D_MODEL), lambda i, f: (0, 0))
    return pl.pallas_call(
        _ffn_dense_kernel,
        grid=(L // tm, D_FF // tf),
        in_specs=[row,
                  pl.BlockSpec((1, D_MODEL, tf), lambda i, f: (0, 0, f)),
                  pl.BlockSpec((1, D_MODEL, tf), lambda i, f: (0, 0, f)),
                  pl.BlockSpec((1, tf, D_MODEL), lambda i, f: (0, f, 0)),
                  row, vec, vec],
        out_specs=[row, row],
        out_shape=[jax.ShapeDtypeStruct((L, D_MODEL), F32), jax.ShapeDtypeStruct((L, D_MODEL), BF16)],
        scratch_shapes=[pltpu.VMEM((tm, D_MODEL), F32)],
        compiler_params=_cparams(("parallel", "arbitrary")),
        name="ffn_dense_ln",
    )(xb, wg[None], wu[None], wd[None], x, g.reshape(1, -1), b.reshape(1, -1))


def _start_row_gather(idx_ref, base, n, src_hbm, dst_ref, sem):
    def body(r, carry):
        pltpu.make_async_copy(src_hbm.at[pl.ds(idx_ref[base + r], 1)], dst_ref.at[pl.ds(r, 1)], sem).start()
        return carry

    lax.fori_loop(0, n, body, 0, unroll=8)


def _wait_row_gather(n, src_hbm, dst_ref, sem):
    pltpu.make_async_copy(src_hbm.at[pl.ds(0, n)], dst_ref.at[pl.ds(0, n)], sem).wait()


def _ffn_moe_kernel(exp_ref, used_ref, tok_ref, x_hbm, wg_ref, wu_ref, wd_ref, o_ref, stage_ref, xb_ref, sem,
                    *, per, chunk):
    b, f = pl.program_id(0), pl.program_id(1)
    n_b, n_f = pl.num_programs(0), pl.num_programs(1)
    tm = stage_ref.shape[0]
    used = used_ref[0]
    live = b < used

    @pl.when(jnp.logical_and(live, f == 0))
    def _():
        @pl.when(b == 0)
        def _():
            _start_row_gather(tok_ref, 0, tm, x_hbm, stage_ref, sem)

        _wait_row_gather(tm, x_hbm, stage_ref, sem)
        xb_ref[...] = stage_ref[...].astype(BF16)

    @pl.when(live)
    def _():
        base = jnp.minimum(b + 1, n_b - 1) * tm + f * per
        for j in range(per):
            pltpu.make_async_copy(x_hbm.at[pl.ds(tok_ref[base + j], 1)],
                                  stage_ref.at[pl.ds(f * per + j, 1)], sem).start()
        _swiglu_step(xb_ref, wg_ref, wu_ref, wd_ref, o_ref, chunk)

    @pl.when(jnp.logical_and(b == used - 1, f == n_f - 1))
    def _():
        _wait_row_gather(tm, x_hbm, stage_ref, sem)

    @pl.when(jnp.logical_and(jnp.logical_not(live), f == 0))
    def _():
        o_ref[...] = jnp.zeros(o_ref.shape, F32)


def _ffn_moe(x, buf_tok, blk_exp, n_used, wg, wu, wd, layer):
    P = buf_tok.shape[0]
    tm, tf = MOE_ROWS, MOE_FF_TILE
    nf = D_FF // tf
    assert tm % nf == 0 and tm % MOE_ROW_CHUNK == 0
    ff = lambda b, f, used: jnp.where(b < used[0], f, nf - 1)
    grid_spec = pltpu.PrefetchScalarGridSpec(
        num_scalar_prefetch=3,
        grid=(P // tm, nf),
        in_specs=[pl.BlockSpec(memory_space=pl.ANY),
                  pl.BlockSpec((None, 1, D_MODEL, tf), lambda b, f, e, u, t: (layer, e[b], 0, ff(b, f, u))),
                  pl.BlockSpec((None, 1, D_MODEL, tf), lambda b, f, e, u, t: (layer, e[b], 0, ff(b, f, u))),
                  pl.BlockSpec((None, 1, tf, D_MODEL), lambda b, f, e, u, t: (layer, e[b], ff(b, f, u), 0))],
        out_specs=pl.BlockSpec((tm, D_MODEL), lambda b, f, e, u, t: (b, 0)),
        scratch_shapes=[pltpu.VMEM((tm, D_MODEL), F32), pltpu.VMEM((tm, D_MODEL), BF16),
                        pltpu.SemaphoreType.DMA(())],
    )
    return pl.pallas_call(
        functools.partial(_ffn_moe_kernel, per=tm // nf, chunk=MOE_ROW_CHUNK),
        grid_spec=grid_spec,
        out_shape=jax.ShapeDtypeStruct((P, D_MODEL), F32),
        compiler_params=_cparams(("arbitrary", "arbitrary"), 56),
        name="ffn_moe",
    )(blk_exp, n_used, buf_tok, x, wg, wu, wd)


def _router_kernel(x_ref, r_ref, o_ref):
    xs = _split3(x_ref[...])
    rs = (r_ref[0], r_ref[1], r_ref[2])
    logits = None
    for i, j in ((0, 0), (0, 1), (1, 0), (1, 1), (0, 2), (2, 0)):
        t = jnp.dot(xs[i], rs[j], preferred_element_type=F32)
        logits = t if logits is None else logits + t
    lane = lax.broadcasted_iota(jnp.int32, logits.shape, 1)
    lane_f = lane.astype(F32)
    logits = jnp.where(lane < N_EXPERTS, logits, NEG_INF)
    m1 = jnp.max(logits, axis=1, keepdims=True)
    i1 = jnp.min(jnp.where(logits == m1, lane_f, 128.0), axis=1, keepdims=True)
    rest = jnp.where(lane_f == i1, NEG_INF, logits)
    m2 = jnp.max(rest, axis=1, keepdims=True)
    i2 = jnp.min(jnp.where(rest == m2, lane_f, 128.0), axis=1, keepdims=True)
    e2 = jnp.exp(m2 - m1)
    g1 = 1.0 / (1.0 + e2)
    g2 = e2 / (1.0 + e2)
    o_ref[...] = jnp.where(lane == 0, i1, jnp.where(lane == 1, i2, jnp.where(lane == 2, g1, jnp.where(lane == 3, g2, 0.0))))


def _router(x, router3):
    L = x.shape[0]
    return pl.pallas_call(
        _router_kernel,
        grid=(L // ROW_TILE,),
        in_specs=[pl.BlockSpec((ROW_TILE, D_MODEL), lambda i: (i, 0)),
                  pl.BlockSpec((3, D_MODEL, 128), lambda i: (0, 0, 0))],
        out_specs=pl.BlockSpec((ROW_TILE, 128), lambda i: (i, 0)),
        out_shape=jax.ShapeDtypeStruct((L, 128), F32),
        compiler_params=_cparams(("parallel",)),
        name="moe_router",
    )(x, router3)


def _combine_kernel(da_ref, db_ref, x_ref, gw_ref, g_ref, b_ref, y_hbm, o_ref, ob_ref, ya_ref, yb_ref, sem):
    i, n = pl.program_id(0), pl.num_programs(0)
    tm = x_ref.shape[0]
    slot = i % 2

    def start(step, slot):
        _start_row_gather(da_ref, step * tm, tm, y_hbm, ya_ref.at[slot], sem.at[slot])
        _start_row_gather(db_ref, step * tm, tm, y_hbm, yb_ref.at[slot], sem.at[slot])

    @pl.when(i == 0)
    def _():
        start(0, 0)

    @pl.when(i + 1 < n)
    def _():
        start(i + 1, 1 - slot)

    _wait_row_gather(tm, y_hbm, ya_ref.at[slot], sem.at[slot])
    _wait_row_gather(tm, y_hbm, yb_ref.at[slot], sem.at[slot])
    gw = gw_ref[...]
    f = gw[:, 2:3] * ya_ref[slot] + gw[:, 3:4] * yb_ref[slot]
    y = _layer_norm(ALPHA * x_ref[...] + f, g_ref[...], b_ref[...])
    o_ref[...] = y
    ob_ref[...] = y.astype(BF16)


def _combine(x, y_rows, dest_a, dest_b, route, g, b):
    L = x.shape[0]
    tm = COMBINE_ROWS
    row = pl.BlockSpec((tm, D_MODEL), lambda i, a, b: (i, 0))
    vec = pl.BlockSpec((1, D_MODEL), lambda i, a, b: (0, 0))
    grid_spec = pltpu.PrefetchScalarGridSpec(
        num_scalar_prefetch=2,
        grid=(L // tm,),
        in_specs=[row, pl.BlockSpec((tm, 128), lambda i, a, b: (i, 0)), vec, vec, pl.BlockSpec(memory_space=pl.ANY)],
        out_specs=[row, row],
        scratch_shapes=[pltpu.VMEM((2, tm, D_MODEL), F32), pltpu.VMEM((2, tm, D_MODEL), F32),
                        pltpu.SemaphoreType.DMA((2,))],
    )
    return pl.pallas_call(
        _combine_kernel,
        grid_spec=grid_spec,
        out_shape=[jax.ShapeDtypeStruct((L, D_MODEL), F32), jax.ShapeDtypeStruct((L, D_MODEL), BF16)],
        compiler_params=_cparams(("arbitrary",)),
        name="moe_combine_ln",
    )(dest_a, dest_b, x, route, g.reshape(1, -1), b.reshape(1, -1), y_rows)


def _moe(x, router, wg, wu, wd, layer, g, b):
    T = x.shape[0]
    r32 = jnp.pad(router, ((0, 0), (0, 128 - N_EXPERTS)))
    r1 = r32.astype(BF16)
    r2 = (r32 - r1.astype(F32)).astype(BF16)
    r3 = (r32 - r1.astype(F32) - r2.astype(F32)).astype(BF16)
    route = _router(x, jnp.stack([r1, r2, r3]))
    e_flat = route[:, :TOP_K].astype(jnp.int32).reshape(-1)
    onehot = (e_flat[:, None] == jnp.arange(N_EXPERTS, dtype=jnp.int32)[None, :]).astype(jnp.int32)
    csum = jnp.cumsum(onehot, axis=0)
    rank = jnp.sum(onehot * csum, axis=1) - 1
    counts = csum[-1]
    padded = (counts + MOE_ROWS - 1) // MOE_ROWS * MOE_ROWS
    pend = jnp.cumsum(padded)
    dest = (pend - padded)[e_flat] + rank
    n_blocks = -(-(T * TOP_K + N_EXPERTS * (MOE_ROWS - 1)) // MOE_ROWS)
    P = n_blocks * MOE_ROWS
    tok = jnp.repeat(jnp.arange(T, dtype=jnp.int32), TOP_K)
    buf_tok = jnp.zeros((P,), jnp.int32).at[dest].set(tok)
    n_used = (pend[-1] // MOE_ROWS).astype(jnp.int32).reshape(1)
    blk_start = jnp.arange(n_blocks, dtype=jnp.int32) * MOE_ROWS
    blk_exp = jnp.minimum(jnp.searchsorted(pend, jnp.minimum(blk_start, pend[-1] - 1), side='right'),
                          N_EXPERTS - 1).astype(jnp.int32)
    y_rows = _ffn_moe(x, buf_tok, blk_exp, n_used, wg, wu, wd, layer)
    d2 = dest.reshape(T, TOP_K)
    return _combine(x, y_rows, d2[:, 0], d2[:, 1], route, g, b)


def _rope_tables(L):
    inv_freq = 1.0 / (ROPE_THETA ** (jnp.arange(0, ROPE_DIM, 2, dtype=F32) / ROPE_DIM))
    ang = jnp.arange(L, dtype=F32)[:, None] * inv_freq[None, :]
    cos, sin = jnp.cos(ang), jnp.sin(ang)
    z = lambda n: jnp.zeros((L, n), F32)
    cosf = jnp.concatenate([cos, cos, jnp.ones((L, HEAD_DIM - ROPE_DIM), F32)], axis=1)
    sa = jnp.concatenate([-sin, z(HEAD_DIM - 16)], axis=1)
    sb = jnp.concatenate([z(16), sin, z(HEAD_DIM - ROPE_DIM)], axis=1)
    return cosf, sa, sb


def _w_in_pack_kernel(a_ref, b_ref, c_ref, o_ref):
    o_ref[:, :3328] = a_ref[...].astype(BF16)
    o_ref[:, 3328:3840] = b_ref[...].astype(BF16)
    o_ref[:, 3840:] = c_ref[...].astype(BF16)


def _reorder_w_in(w_all, layer):
    tail = w_all[layer, :, 3328:]
    gates = jnp.pad(tail[:, :12], ((0, 0), (0, 128 - 12)))
    rows = 256
    blk = lambda n: pl.BlockSpec((rows, n), lambda i: (i, 0))
    return pl.pallas_call(
        _w_in_pack_kernel,
        grid=(D_MODEL // rows,),
        in_specs=[pl.BlockSpec((None, rows, 3328), lambda i: (layer, i, 0)), blk(512), blk(128)],
        out_specs=blk(PROJ_WIDTH),
        out_shape=jax.ShapeDtypeStruct((D_MODEL, PROJ_WIDTH), BF16),
        compiler_params=_cparams(("parallel",)),
        name="w_in_pack",
    )(w_all, tail[:, 12:], gates)


def _trunk(x, mem, p):
    L = x.shape[0]
    ns, nc = L // NSA_SEL_BLOCK, L // NSA_CMP_STRIDE
    cosf, sa, sb = _rope_tables(L)
    lanes = -(-ns // HEAD_DIM) * HEAD_DIM
    cj = jnp.arange(nc)[:, None] - 4 * jnp.arange(lanes)[None, :]
    agg = jnp.logical_and(cj >= -1, cj <= 3).astype(BF16)
    ck = min(SEL_CHUNK, L)
    per = ck // NSA_SEL_BLOCK
    n_tag = min(HEAD_DIM // per, L // ck)
    blk_id = jnp.arange(n_tag)[:, None, None] * per + jnp.arange(ck)[None, :, None] // NSA_SEL_BLOCK
    tags = (blk_id == jnp.arange(HEAD_DIM)[None, None, :]).astype(BF16)
    mem_b = mem.astype(BF16)

    x, xb = _ln_in(x, p['ln_in_g'], p['ln_in_b'])
    for i in range(DEPTH):
        u, mq, mk, mv, nqc, nqr, cmp_kv, nkv, xq, gates = _proj(xb, _reorder_w_in(p['w_in'], i), cosf, sa, sb)
        bmat, cmat, pw = _s5_params(p['ssm_a_re'][i], p['ssm_a_im'][i], p['ssm_log_dt'][i], p['ssm_b_re'][i],
                                    p['ssm_b_im'][i], p['ssm_c_re'][i], p['ssm_c_im'][i])
        y_ssm = _s5(u, bmat, cmat, pw, p['ssm_d'][i], p['ssm_w_glu'][i])
        y_moba = _moba(mq, mk, mv, _kmean(mk))
        pos = jnp.stack([p['nsa_pos_k'][i].reshape(1, -1), p['nsa_pos_v'][i].reshape(1, -1)])
        pos = jnp.broadcast_to(pos, (2, 8, pos.shape[-1])).astype(BF16)
        kvc = _compress(cmp_kv, jnp.stack([p['nsa_ck1'][i], p['nsa_cv1'][i]]).astype(BF16),
                        jnp.stack([p['nsa_ck2'][i], p['nsa_cv2'][i]]).astype(BF16), pos)
        o_cmp, sel_bias = _nsa_cmp(nqc, kvc, agg)
        y_nsa = _nsa_sel(nqr, nkv, sel_bias, tags, o_cmp, gates)
        mem_kv = _mem_proj(mem_b, jnp.concatenate([p['mem_wk'][i], p['mem_wv'][i]], axis=1).astype(BF16))
        y_x = _xattn(xq, mem_kv)
        x, xb = _out_proj((y_ssm, y_moba, y_nsa, y_x), p['w_o'][i].astype(BF16), x, p['ln1_g'][i], p['ln1_b'][i])
        j = i // 2
        if i % 2 == 0:
            x, xb = _ffn_dense(xb, x, p['ffn_w_gate'][j].astype(BF16), p['ffn_w_up'][j].astype(BF16),
                               p['ffn_w_down'][j].astype(BF16), p['ln2_g'][i], p['ln2_b'][i])
        else:
            x, xb = _moe(x, p['moe_router'][j], p['moe_w_gate'], p['moe_w_up'], p['moe_w_down'], j,
                         p['ln2_g'][i], p['ln2_b'][i])
    return x


def kernel(x, mem, ln_in_g, ln_in_b, w_in, ssm_a_re, ssm_a_im, ssm_log_dt, ssm_b_re, ssm_b_im, ssm_c_re, ssm_c_im,
           ssm_d, ssm_w_glu, nsa_pos_k, nsa_pos_v, nsa_ck1, nsa_ck2, nsa_cv1, nsa_cv2, mem_wk, mem_wv, w_o, ln1_g,
           ln1_b, ln2_g, ln2_b, ffn_w_gate, ffn_w_up, ffn_w_down, moe_router, moe_w_gate, moe_w_up, moe_w_down):
    p = dict(ln_in_g=ln_in_g, ln_in_b=ln_in_b, w_in=w_in, ssm_a_re=ssm_a_re, ssm_a_im=ssm_a_im,
             ssm_log_dt=ssm_log_dt, ssm_b_re=ssm_b_re, ssm_b_im=ssm_b_im, ssm_c_re=ssm_c_re, ssm_c_im=ssm_c_im,
             ssm_d=ssm_d, ssm_w_glu=ssm_w_glu, nsa_pos_k=nsa_pos_k, nsa_pos_v=nsa_pos_v, nsa_ck1=nsa_ck1,
             nsa_ck2=nsa_ck2, nsa_cv1=nsa_cv1, nsa_cv2=nsa_cv2, mem_wk=mem_wk, mem_wv=mem_wv, w_o=w_o,
             ln1_g=ln1_g, ln1_b=ln1_b, ln2_g=ln2_g, ln2_b=ln2_b, ffn_w_gate=ffn_w_gate, ffn_w_up=ffn_w_up,
             ffn_w_down=ffn_w_down, moe_router=moe_router, moe_w_gate=moe_w_gate, moe_w_up=moe_w_up,
             moe_w_down=moe_w_down)
    assert x.shape[0] == 1 and mem.shape[0] == 1
    return _trunk(x[0], mem[0], p)[None]
```

```python
import functools
import math

import jax
import jax.numpy as jnp
from jax import lax
from jax.experimental import pallas as pl
from jax.experimental.pallas import tpu as pltpu

F32 = jnp.float32
BF16 = jnp.bfloat16

D_MODEL = 2048
DEPTH = 4
HEAD_DIM = 128
GROUP_WIDTH = 512
N_HEADS = 4
SSM_GROUP = 16
SSM_NG = 32
SSM_STATE = 64
SSM_NSTATE = SSM_NG * SSM_STATE
MOBA_BLOCK = 256
MOBA_TOPK = 3
NSA_CMP_LEN = 32
NSA_CMP_STRIDE = 16
NSA_SEL_BLOCK = 64
NSA_TOPK = 16
NSA_WINDOW = 512
ROPE_THETA = 500000.0
ROPE_DIM = 32
D_FF = 5632
N_EXPERTS = 8
TOP_K = 2
Q_BLOCK = 128
LN_EPS = 1e-5
NEG_INF = -1e30
FORCE_SCORE = 1e9
ALPHA = (2.0 * DEPTH) ** 0.25
SCALE = HEAD_DIM ** -0.5 * math.log2(math.e)

ROW_TILE = 512
SCAN_ROWS = 512
SCAN_LANES = 512
SEL_CHUNK = 512
SEL_STEP_CHUNKS = 4
MOBA_STEP_BLOCKS = 8
FF_TILE = 512
FF_ROW_CHUNK = 512
MOE_ROWS = 1056
MOE_ROW_CHUNK = 528
MOE_FF_TILE = 256
COMBINE_ROWS = 256
PROJ_WIDTH = 3968

_NT = (((1,), (1,)), ((), ()))


def _cparams(sem, vmem_mb=48):
    return pltpu.CompilerParams(dimension_semantics=sem, vmem_limit_bytes=vmem_mb * 1024 * 1024)


def _resident(block_shape, index_map):
    return pl.BlockSpec(block_shape, index_map, pipeline_mode=pl.Buffered(1))


def _layer_norm(r, g, b):
    mu = jnp.mean(r, axis=-1, keepdims=True)
    d = r - mu
    var = jnp.mean(d * d, axis=-1, keepdims=True)
    return d * lax.rsqrt(var + LN_EPS) * g + b


def _gelu_tanh(x):
    return x * (0.5 * (1.0 + jnp.tanh(math.sqrt(2.0 / math.pi) * (x + 0.044715 * (x * x * x)))))


def _sigmoid(x):
    return 1.0 / (1.0 + jnp.exp(-x))


def _split3(a):
    a1 = a.astype(BF16)
    r1 = a - a1.astype(F32)
    a2 = r1.astype(BF16)
    a3 = (r1 - a2.astype(F32)).astype(BF16)
    return a1, a2, a3


def _topk_mask(score, k):
    ids = lax.broadcasted_iota(jnp.int32, score.shape, 1).astype(F32)

    def body(_, sc):
        m = jnp.max(sc, axis=1, keepdims=True)
        idx = jnp.min(jnp.where(sc == m, ids, float(score.shape[1])), axis=1, keepdims=True)
        return jnp.where(ids == idx, NEG_INF, sc)

    left = lax.fori_loop(0, k, body, score)
    return jnp.where(jnp.logical_and(score > 0.5 * NEG_INF, left <= 0.5 * NEG_INF), 1.0, 0.0)


def _ln_in_kernel(x_ref, g_ref, b_ref, o_ref, ob_ref):
    y = _layer_norm(x_ref[...], g_ref[...], b_ref[...])
    o_ref[...] = y
    ob_ref[...] = y.astype(BF16)


def _ln_in(x, g, b):
    L = x.shape[0]
    row = pl.BlockSpec((ROW_TILE, D_MODEL), lambda i: (i, 0))
    vec = pl.BlockSpec((1, D_MODEL), lambda i: (0, 0))
    return pl.pallas_call(
        _ln_in_kernel,
        grid=(L // ROW_TILE,),
        in_specs=[row, vec, vec],
        out_specs=[row, row],
        out_shape=[jax.ShapeDtypeStruct((L, D_MODEL), F32), jax.ShapeDtypeStruct((L, D_MODEL), BF16)],
        compiler_params=_cparams(("parallel",)),
        name="ln_in",
    )(x, g.reshape(1, -1), b.reshape(1, -1))


def _proj_kernel(x_ref, w_ref, cos_ref, sa_ref, sb_ref,
                 u_ref, mq_ref, mk_ref, mv_ref, nqc_ref, nqr_ref, cmp_ref, nkv_ref, xq_ref, g_ref):
    x = x_ref[...]
    cos, sa, sb = cos_ref[...], sa_ref[...], sb_ref[...]

    def dot(c0, n):
        return jnp.dot(x, w_ref[:, c0:c0 + n], preferred_element_type=F32)

    def rope(t):
        return t * cos + pltpu.roll(t, HEAD_DIM - 16, 1) * sa + pltpu.roll(t, 16, 1) * sb

    heads = [slice(h * HEAD_DIM, (h + 1) * HEAD_DIM) for h in range(N_HEADS)]
    u_ref[...] = dot(0, 512)
    t = dot(512, 512)
    for cs in heads:
        mq_ref[:, cs] = (rope(t[:, cs]) * SCALE).astype(BF16)
    t = dot(1024, 512)
    for cs in heads:
        mk_ref[:, cs] = rope(t[:, cs]).astype(BF16)
    mv_ref[...] = dot(1536, 512).astype(BF16)
    t = dot(2048, 512)
    nqc_ref[...] = (t * SCALE).astype(BF16)
    for cs in heads:
        nqr_ref[:, cs] = (rope(t[:, cs]) * SCALE).astype(BF16)
    t = dot(2560, 512)
    cmp_ref[0] = t[:, heads[0]].astype(BF16)
    cmp_ref[1] = t[:, heads[1]].astype(BF16)
    nkv_ref[:, heads[0]] = rope(t[:, heads[2]]).astype(BF16)
    nkv_ref[:, heads[1]] = t[:, heads[3]].astype(BF16)
    t = dot(3072, 256)
    nkv_ref[:, heads[2]] = rope(t[:, heads[0]]).astype(BF16)
    nkv_ref[:, heads[3]] = t[:, heads[1]].astype(BF16)
    t = dot(3328, 640)
    xq_ref[...] = (t[:, :512] * SCALE).astype(BF16)
    g_ref[...] = _sigmoid(t[:, 512:])


def _proj(xb, w, cosf, sa, sb):
    L = xb.shape[0]
    tm = ROW_TILE
    row = lambda n: pl.BlockSpec((tm, n), lambda i: (i, 0))
    outs = [
        (jax.ShapeDtypeStruct((L, 512), F32), row(512)),
        (jax.ShapeDtypeStruct((L, 512), BF16), row(512)),
        (jax.ShapeDtypeStruct((L, 512), BF16), row(512)),
        (jax.ShapeDtypeStruct((L, 512), BF16), row(512)),
        (jax.ShapeDtypeStruct((L, 512), BF16), row(512)),
        (jax.ShapeDtypeStruct((L, 512), BF16), row(512)),
        (jax.ShapeDtypeStruct((2, L, 128), BF16), pl.BlockSpec((2, tm, 128), lambda i: (0, i, 0))),
        (jax.ShapeDtypeStruct((L, 512), BF16), row(512)),
        (jax.ShapeDtypeStruct((L, 512), BF16), row(512)),
        (jax.ShapeDtypeStruct((L, 128), F32), row(128)),
    ]
    return pl.pallas_call(
        _proj_kernel,
        grid=(L // tm,),
        in_specs=[row(D_MODEL), _resident((D_MODEL, PROJ_WIDTH), lambda i: (0, 0)),
                  row(128), row(128), row(128)],
        out_specs=[o[1] for o in outs],
        out_shape=[o[0] for o in outs],
        compiler_params=_cparams(("parallel",), 56),
        name="in_proj",
    )(xb, w, cosf, sa, sb)


def _s5_kernel(u_ref, bmat_ref, cmat_ref, pw_ref, d_ref, wglu_ref, o_ref, hr_ref, hi_ref, car_ref, cai_ref):
    T = u_ref.shape[0]
    W = SCAN_LANES

    @pl.when(pl.program_id(0) == 0)
    def _():
        car_ref[...] = jnp.zeros(car_ref.shape, F32)
        cai_ref[...] = jnp.zeros(cai_ref.shape, F32)

    u = u_ref[...]
    ub = u.astype(BF16)
    for b in range(4):
        bu = jnp.dot(ub[:, b * 128:(b + 1) * 128], bmat_ref[b], preferred_element_type=F32)
        hr_ref[:, b * 512:(b + 1) * 512] = bu[:, :512]
        hi_ref[:, b * 512:(b + 1) * 512] = bu[:, 512:]

    for cb in range(SSM_NSTATE // W):
        sl = slice(cb * W, (cb + 1) * W)
        steps = [(pw_ref[2 * j, :, sl], pw_ref[2 * j + 1, :, sl], 1 << j) for j in range(3)]
        p_r, p_i = pw_ref[6, :, sl], pw_ref[7, :, sl]

        def tile(t, carry, sl=sl, steps=steps, p_r=p_r, p_i=p_i):
            c_r, c_i = carry
            rows = pl.ds(pl.multiple_of(t * 8, 8), 8)
            x_r, x_i = hr_ref[rows, sl], hi_ref[rows, sl]
            for l_r, l_i, d in steps:
                s_r, s_i = pltpu.roll(x_r, d, 0), pltpu.roll(x_i, d, 0)
                x_r, x_i = x_r + l_r * s_r - l_i * s_i, x_i + l_r * s_i + l_i * s_r
            x_r, x_i = x_r + p_r * c_r - p_i * c_i, x_i + p_r * c_i + p_i * c_r
            hr_ref[rows, sl] = x_r
            hi_ref[rows, sl] = x_i
            return (jnp.broadcast_to(x_r[7:8, :], (8, W)), jnp.broadcast_to(x_i[7:8, :], (8, W)))

        c_r, c_i = lax.fori_loop(0, T // 8, tile, (car_ref[:, sl], cai_ref[:, sl]))
        car_ref[:, sl] = c_r
        cai_ref[:, sl] = c_i

    ys = []
    for b in range(4):
        h_r = hr_ref[:, b * 512:(b + 1) * 512].astype(BF16)
        h_i = hi_ref[:, b * 512:(b + 1) * 512].astype(BF16)
        ys.append(jnp.dot(h_r, cmat_ref[b, :512, :], preferred_element_type=F32)
                  + jnp.dot(h_i, cmat_ref[b, 512:, :], preferred_element_type=F32))
    y = jnp.concatenate(ys, axis=1) + d_ref[...] * u
    z = _gelu_tanh(y)
    gate = _sigmoid(jnp.dot(z.astype(BF16), wglu_ref[...], preferred_element_type=F32))
    o_ref[...] = (z * gate).astype(BF16)


def _s5_params(a_re, a_im, log_dt, b_re, b_im, c_re, c_im):
    dt = jnp.exp(log_dt)[:, None]
    mag = jnp.exp(a_re * dt)
    lam_re, lam_im = mag * jnp.cos(a_im * dt), mag * jnp.sin(a_im * dt)
    den = a_re * a_re + a_im * a_im
    nr, ni = lam_re - 1.0, lam_im
    coef_re = (nr * a_re + ni * a_im) / den
    coef_im = (ni * a_re - nr * a_im) / den
    bbar_re = coef_re[..., None] * b_re - coef_im[..., None] * b_im
    bbar_im = coef_re[..., None] * b_im + coef_im[..., None] * b_re
    eye = jnp.eye(8, dtype=F32)
    to_b = lambda m: jnp.einsum('bgnc,gh->bgchn', m.reshape(4, 8, SSM_STATE, SSM_GROUP), eye).reshape(4, 128, 512)
    bmat = jnp.concatenate([to_b(bbar_re), to_b(bbar_im)], axis=-1).astype(BF16)
    to_c = lambda m: jnp.einsum('bgcn,gh->bgnhc', m.reshape(4, 8, SSM_GROUP, SSM_STATE), eye).reshape(4, 512, 128)
    cmat = jnp.concatenate([to_c(c_re), -to_c(c_im)], axis=1).astype(BF16)
    lr, li = lam_re.reshape(-1), lam_im.reshape(-1)
    pr, pi = [lr], [li]
    for _ in range(7):
        pr, pi = pr + [pr[-1] * lr - pi[-1] * li], pi + [pr[-1] * li + pi[-1] * lr]
    row = jnp.arange(8)[:, None]
    tabs = []
    for d in (1, 2, 4):
        tabs += [jnp.where(row >= d, pr[d - 1][None, :], 0.0), jnp.where(row >= d, pi[d - 1][None, :], 0.0)]
    tabs += [jnp.stack(pr, axis=0), jnp.stack(pi, axis=0)]
    return bmat, cmat, jnp.stack(tabs, axis=0)


def _s5(u, bmat, cmat, pw, d_skip, w_glu):
    L = u.shape[0]
    T = SCAN_ROWS
    row = pl.BlockSpec((T, 512), lambda i: (i, 0))
    full = lambda shape: pl.BlockSpec(shape, lambda i: (0,) * len(shape))
    return pl.pallas_call(
        _s5_kernel,
        grid=(L // T,),
        in_specs=[row, full((4, 128, 1024)), full((4, 1024, 128)), full((8, 8, SSM_NSTATE)),
                  full((1, 512)), full((512, 512))],
        out_specs=row,
        out_shape=jax.ShapeDtypeStruct((L, 512), BF16),
        scratch_shapes=[pltpu.VMEM((T, SSM_NSTATE), F32), pltpu.VMEM((T, SSM_NSTATE), F32),
                        pltpu.VMEM((8, SSM_NSTATE), F32), pltpu.VMEM((8, SSM_NSTATE), F32)],
        compiler_params=_cparams(("arbitrary",)),
        name="s5",
    )(u, bmat, cmat, pw, d_skip.reshape(1, -1), w_glu.astype(BF16))


def _kmean_kernel(k_ref, o_ref):
    rows = k_ref.shape[0]
    k = k_ref[...].astype(F32).reshape(rows // MOBA_BLOCK, MOBA_BLOCK, GROUP_WIDTH)
    o_ref[...] = jnp.mean(k, axis=1).astype(BF16)


def _kmean(k):
    L = k.shape[0]
    rows = 8 * MOBA_BLOCK if L % (8 * MOBA_BLOCK) == 0 else L
    return pl.pallas_call(
        _kmean_kernel,
        grid=(L // rows,),
        in_specs=[pl.BlockSpec((rows, GROUP_WIDTH), lambda i: (i, 0))],
        out_specs=pl.BlockSpec((rows // MOBA_BLOCK, GROUP_WIDTH), lambda i: (i, 0)),
        out_shape=jax.ShapeDtypeStruct((L // MOBA_BLOCK, GROUP_WIDTH), BF16),
        compiler_params=_cparams(("parallel",)),
        name="moba_kmean",
    )(k)


def _with_ones(v):
    return jnp.concatenate([v, jnp.ones(v.shape, v.dtype)], axis=1)


def _flash_step(s, v_ones, m_ref, acc_ref, idx, first):
    row_max = jnp.max(s, axis=1, keepdims=True)
    if first:
        m_new = jnp.broadcast_to(row_max, (s.shape[0], HEAD_DIM))
    else:
        m_old = m_ref[idx]
        m_new = jnp.maximum(m_old, row_max)
    p = jnp.concatenate([jnp.exp2(s[:, c:c + HEAD_DIM] - m_new) for c in range(0, s.shape[1], HEAD_DIM)], axis=1)
    pv = jnp.dot(p.astype(BF16), v_ones, preferred_element_type=F32)
    if first:
        acc_ref[idx] = pv
    else:
        a = jnp.exp2(m_old - m_new)
        acc_ref[idx] = jnp.concatenate([a, a], axis=1) * acc_ref[idx] + pv
    m_ref[idx] = m_new


def _flash_result(acc_ref, idx):
    acc = acc_ref[idx]
    return acc[:, :HEAD_DIM] / acc[:, HEAD_DIM:]


def _moba_kernel(q_ref, k_ref, v_ref, km_ref, o_ref, qa_ref, m_ref, acc_ref):
    TQ = q_ref.shape[0]
    own = pl.program_id(0)
    heads = [slice(h * HEAD_DIM, (h + 1) * HEAD_DIM) for h in range(N_HEADS)]

    gs = jnp.concatenate([lax.dot_general(q_ref[:, cs], km_ref[:, cs], _NT, preferred_element_type=F32)
                          for cs in heads], axis=0)
    blk = lax.broadcasted_iota(jnp.int32, gs.shape, 1)
    sel = _topk_mask(jnp.where(blk < own, gs, NEG_INF), MOBA_TOPK)
    bias = jnp.where(sel > 0.5, 0.0, NEG_INF).astype(BF16)
    for h, cs in enumerate(heads):
        qa_ref[h] = jnp.concatenate([q_ref[:, cs], bias[h * TQ:(h + 1) * TQ]], axis=1)

    own_rows = pl.ds(pl.multiple_of(own * MOBA_BLOCK, MOBA_BLOCK), MOBA_BLOCK)
    causal = (lax.broadcasted_iota(jnp.int32, (TQ, MOBA_BLOCK), 1)
              <= lax.broadcasted_iota(jnp.int32, (TQ, MOBA_BLOCK), 0))
    for h, cs in enumerate(heads):
        s = lax.dot_general(q_ref[:, cs], k_ref[own_rows, cs], _NT, preferred_element_type=F32)
        _flash_step(jnp.where(causal, s, NEG_INF), _with_ones(v_ref[own_rows, cs]), m_ref, acc_ref, h, True)

    def past(first_block, n_blocks):
        n = n_blocks * MOBA_BLOCK
        rows = pl.ds(pl.multiple_of(first_block * MOBA_BLOCK, MOBA_BLOCK), n)
        blk = first_block + lax.broadcasted_iota(jnp.int32, (n, HEAD_DIM), 0) // MOBA_BLOCK
        tag = jnp.where(lax.broadcasted_iota(jnp.int32, (n, HEAD_DIM), 1) == blk, 1.0, 0.0).astype(BF16)
        for h, cs in enumerate(heads):
            k_aug = jnp.concatenate([k_ref[rows, cs], tag], axis=1)
            s = lax.dot_general(qa_ref[h], k_aug, _NT, preferred_element_type=F32)
            _flash_step(s, _with_ones(v_ref[rows, cs]), m_ref, acc_ref, h, False)

    def group(t, carry):
        past(MOBA_STEP_BLOCKS * t, MOBA_STEP_BLOCKS)
        return carry

    lax.fori_loop(0, own // MOBA_STEP_BLOCKS, group, 0)
    size = MOBA_STEP_BLOCKS // 2
    while size >= 1:
        @pl.when((own & size) != 0)
        def _(size=size):
            past(own & ~(2 * size - 1), size)
        size //= 2

    for h, cs in enumerate(heads):
        o_ref[:, cs] = _flash_result(acc_ref, h).astype(BF16)


def _moba(q, k, v, kmean):
    L = q.shape[0]
    TQ = MOBA_BLOCK
    nb = L // MOBA_BLOCK
    assert nb <= HEAD_DIM
    km = jnp.pad(kmean, ((0, HEAD_DIM - nb), (0, 0)))
    tile = pl.BlockSpec((TQ, GROUP_WIDTH), lambda i: (i, 0))
    return pl.pallas_call(
        _moba_kernel,
        grid=(L // TQ,),
        in_specs=[tile, _resident((L, GROUP_WIDTH), lambda i: (0, 0)), _resident((L, GROUP_WIDTH), lambda i: (0, 0)),
                  _resident((HEAD_DIM, GROUP_WIDTH), lambda i: (0, 0))],
        out_specs=tile,
        out_shape=jax.ShapeDtypeStruct((L, GROUP_WIDTH), BF16),
        scratch_shapes=[pltpu.VMEM((N_HEADS, TQ, 2 * HEAD_DIM), BF16), pltpu.VMEM((N_HEADS, TQ, HEAD_DIM), F32),
                        pltpu.VMEM((N_HEADS, TQ, 2 * HEAD_DIM), F32)],
        compiler_params=_cparams(("parallel",)),
        name="moba",
    )(q, k, v, km)


def _compress_kernel(c_ref, w1_ref, w2_ref, pos_ref, o_ref):
    n = c_ref.shape[1]
    half = w1_ref.shape[1] // 2
    c = c_ref[0]
    a = jnp.dot(c, w1_ref[0, :half, :], preferred_element_type=F32)
    b = jnp.dot(c, w1_ref[0, half:, :], preferred_element_type=F32)
    bias = jnp.dot(pos_ref[0], w1_ref[0], preferred_element_type=F32)[0:1, :]
    hid = _gelu_tanh(a + pltpu.roll(b, n - 1, 0) + bias)
    out = jnp.dot(hid.astype(BF16), w2_ref[0], preferred_element_type=F32)
    last = lax.broadcasted_iota(jnp.int32, out.shape, 0) == n - 1
    o_ref[0] = jnp.where(last, 0.0, out).astype(BF16)


def _compress(cmp_kv, w1, w2, pos):
    L = cmp_kv.shape[1]
    n = L // NSA_CMP_STRIDE
    wide = NSA_CMP_STRIDE * HEAD_DIM
    chunks = cmp_kv.reshape(2, n, wide)
    hidden = w1.shape[-1]
    return pl.pallas_call(
        _compress_kernel,
        grid=(2,),
        in_specs=[pl.BlockSpec((1, n, wide), lambda j: (j, 0, 0)),
                  pl.BlockSpec((1, 2 * wide, hidden), lambda j: (j, 0, 0)),
                  pl.BlockSpec((1, hidden, HEAD_DIM), lambda j: (j, 0, 0)),
                  pl.BlockSpec((1, 8, 2 * wide), lambda j: (j, 0, 0))],
        out_specs=pl.BlockSpec((1, n, HEAD_DIM), lambda j: (j, 0, 0)),
        out_shape=jax.ShapeDtypeStruct((2, n, HEAD_DIM), BF16),
        compiler_params=_cparams(("parallel",)),
        name="nsa_compress",
    )(chunks, w1, w2, pos)


def _stack_heads(ref):
    return jnp.concatenate([ref[:, h * HEAD_DIM:(h + 1) * HEAD_DIM] for h in range(N_HEADS)], axis=0)


def _nsa_cmp_kernel(q_ref, kc_ref, vc_ref, agg_ref, o_ref, sel_ref):
    TQ = q_ref.shape[0]
    nc = kc_ref.shape[1]
    ns = nc * NSA_CMP_STRIDE // NSA_SEL_BLOCK
    lanes = agg_ref.shape[1]
    start = pl.program_id(0) * TQ
    qs = _stack_heads(q_ref)

    def run(cols, width):
        s = lax.dot_general(qs, kc_ref[0, :cols, :], _NT, preferred_element_type=F32)
        cend = lax.broadcasted_iota(jnp.int32, s.shape, 1) * NSA_CMP_STRIDE + (NSA_CMP_LEN - 1)
        ok = cend <= start + (lax.broadcasted_iota(jnp.int32, s.shape, 0) & (TQ - 1))
        s = jnp.where(ok, s, NEG_INF)
        m = jnp.max(s, axis=1, keepdims=True)
        e = jnp.where(ok, jnp.exp2(s - m), 0.0)
        p = e / jnp.maximum(jnp.sum(e, axis=1, keepdims=True), 1e-30)
        oc = jnp.dot(p.astype(BF16), vc_ref[0, :cols, :], preferred_element_type=F32)
        for h in range(N_HEADS):
            o_ref[:, h * HEAD_DIM:(h + 1) * HEAD_DIM] = oc[h * TQ:(h + 1) * TQ, :]

        imp = p[0:TQ] + p[TQ:2 * TQ] + p[2 * TQ:3 * TQ] + p[3 * TQ:4 * TQ]
        agg = agg_ref[:cols, :width]
        imp_sel = sum(jnp.dot(t, agg, preferred_element_type=F32) for t in _split3(imp))
        blk = lax.broadcasted_iota(jnp.int32, imp_sel.shape, 1)
        own = (start + lax.broadcasted_iota(jnp.int32, imp_sel.shape, 0)) // NSA_SEL_BLOCK
        score = jnp.where(blk == own, FORCE_SCORE, jnp.where(blk < own, imp_sel, NEG_INF))
        sel = _topk_mask(score, min(NSA_TOPK, ns))
        sel_ref[:, :width] = jnp.where(sel > 0.5, 0.0, NEG_INF).astype(BF16)
        if width < lanes:
            sel_ref[:, width:] = jnp.full((TQ, lanes - width), NEG_INF, BF16)

    parts = 4 if nc % (4 * HEAD_DIM) == 0 else 1
    quarter = (start + TQ - 1) // (nc * NSA_CMP_STRIDE // parts)
    for k in range(parts):
        cols = (k + 1) * nc // parts
        width = min(lanes, -(-((k + 1) * ns // parts) // HEAD_DIM) * HEAD_DIM)
        pl.when(quarter == k)(functools.partial(run, cols, width))


def _nsa_cmp(nq, kvc, agg):
    L = nq.shape[0]
    TQ = 2 * Q_BLOCK
    nc = kvc.shape[1]
    lanes = agg.shape[1]
    return pl.pallas_call(
        _nsa_cmp_kernel,
        grid=(L // TQ,),
        in_specs=[pl.BlockSpec((TQ, GROUP_WIDTH), lambda i: (i, 0)),
                  pl.BlockSpec((1, nc, HEAD_DIM), lambda i: (0, 0, 0)),
                  pl.BlockSpec((1, nc, HEAD_DIM), lambda i: (1, 0, 0)),
                  pl.BlockSpec((nc, lanes), lambda i: (0, 0))],
        out_specs=[pl.BlockSpec((TQ, GROUP_WIDTH), lambda i: (i, 0)),
                   pl.BlockSpec((TQ, lanes), lambda i: (i, 0))],
        out_shape=[jax.ShapeDtypeStruct((L, GROUP_WIDTH), F32), jax.ShapeDtypeStruct((L, lanes), BF16)],
        compiler_params=_cparams(("parallel",)),
        name="nsa_cmp",
    )(nq, kvc, kvc, agg)


def _nsa_sel_kernel(q_ref, ks_ref, vs_ref, kw_ref, vw_ref, sb_ref, tag_ref, oc_ref, g_ref, o_ref,
                    qa_ref, m_ref, acc_ref):
    TQ = q_ref.shape[0]
    CK = tag_ref.shape[1]
    cpg = HEAD_DIM // (CK // NSA_SEL_BLOCK)
    R = N_HEADS * TQ
    halves = (slice(0, R // 2), slice(R // 2, R))
    start = pl.program_id(0) * TQ
    qs = _stack_heads(q_ref)
    qpos = start + (lax.broadcasted_iota(jnp.int32, (R, 1), 0) & (TQ - 1))

    bias = sb_ref[...]
    for g in range(sb_ref.shape[1] // HEAD_DIM):
        qa_ref[g] = jnp.concatenate(
            [qs, jnp.concatenate([bias[:, g * HEAD_DIM:(g + 1) * HEAD_DIM]] * N_HEADS, axis=0)], axis=1)

    def scores(c, n):
        rows = pl.ds(pl.multiple_of(c * CK, CK), n * CK)
        tags = tag_ref[pl.ds(c % cpg, n)].reshape(n * CK, HEAD_DIM)
        k_aug = jnp.concatenate([ks_ref[rows, :], tags], axis=1)
        return rows, [lax.dot_general(qa_ref[c // cpg, hs, :], k_aug, _NT, preferred_element_type=F32)
                      for hs in halves]

    c_last = start // CK
    rows, ss = scores(c_last, 1)
    kpos = c_last * CK + lax.broadcasted_iota(jnp.int32, ss[0].shape, 1)
    v_ones = _with_ones(vs_ref[rows, :])
    for idx, hs in enumerate(halves):
        _flash_step(jnp.where(kpos <= qpos[hs], ss[idx], NEG_INF), v_ones, m_ref, acc_ref, idx, True)

    def past(c, n):
        rows, ss = scores(c, n)
        v_ones = _with_ones(vs_ref[rows, :])
        for idx in range(2):
            _flash_step(ss[idx], v_ones, m_ref, acc_ref, idx, False)

    def group(t, carry):
        past(SEL_STEP_CHUNKS * t, SEL_STEP_CHUNKS)
        return carry

    lax.fori_loop(0, c_last // SEL_STEP_CHUNKS, group, 0)
    size = SEL_STEP_CHUNKS // 2
    while size >= 1:
        @pl.when((c_last & size) != 0)
        def _(size=size):
            past(c_last & ~(2 * size - 1), size)
        size //= 2
    o_sel = jnp.concatenate([_flash_result(acc_ref, 0), _flash_result(acc_ref, 1)], axis=0)

    span = TQ + NSA_WINDOW
    w0 = jnp.maximum(start - NSA_WINDOW, 0)
    rows = pl.ds(pl.multiple_of(w0, TQ), span)
    s = lax.dot_general(qs, kw_ref[rows, :], _NT, preferred_element_type=F32)
    wpos = w0 + lax.broadcasted_iota(jnp.int32, s.shape, 1)
    s = jnp.where(jnp.logical_and(wpos <= qpos, wpos > qpos - NSA_WINDOW), s, NEG_INF)
    p = jnp.exp2(s - jnp.max(s, axis=1, keepdims=True))
    o_win = jnp.dot(p.astype(BF16), vw_ref[rows, :], preferred_element_type=F32) / jnp.sum(p, axis=1, keepdims=True)

    g = g_ref[...]
    for h in range(N_HEADS):
        hs = slice(h * TQ, (h + 1) * TQ)
        cs = slice(h * HEAD_DIM, (h + 1) * HEAD_DIM)
        o_ref[:, cs] = (g[:, 3 * h:3 * h + 1] * oc_ref[:, cs] + g[:, 3 * h + 1:3 * h + 2] * o_sel[hs]
                        + g[:, 3 * h + 2:3 * h + 3] * o_win[hs]).astype(BF16)


def _nsa_sel(nq_rope, nkv, sel_bias, tags, o_cmp, gates):
    L = nq_rope.shape[0]
    TQ = Q_BLOCK
    lanes = sel_bias.shape[1]
    R = N_HEADS * TQ
    col = lambda j: _resident((L, HEAD_DIM), lambda i, j=j: (0, j))
    return pl.pallas_call(
        _nsa_sel_kernel,
        grid=(L // TQ,),
        in_specs=[pl.BlockSpec((TQ, GROUP_WIDTH), lambda i: (i, 0)),
                  col(0), col(1), col(2), col(3),
                  pl.BlockSpec((TQ, lanes), lambda i: (i, 0)),
                  _resident(tags.shape, lambda i: (0, 0, 0)),
                  pl.BlockSpec((TQ, GROUP_WIDTH), lambda i: (i, 0)),
                  pl.BlockSpec((TQ, HEAD_DIM), lambda i: (i, 0))],
        out_specs=pl.BlockSpec((TQ, GROUP_WIDTH), lambda i: (i, 0)),
        out_shape=jax.ShapeDtypeStruct((L, GROUP_WIDTH), BF16),
        scratch_shapes=[pltpu.VMEM((lanes // HEAD_DIM, R, 2 * HEAD_DIM), BF16), pltpu.VMEM((2, R // 2, HEAD_DIM), F32),
                        pltpu.VMEM((2, R // 2, 2 * HEAD_DIM), F32)],
        compiler_params=_cparams(("parallel",)),
        name="nsa_sel_win",
    )(nq_rope, nkv, nkv, nkv, nkv, sel_bias, tags, o_cmp, gates)


def _matmul_kernel(a_ref, b_ref, o_ref):
    o_ref[...] = jnp.dot(a_ref[...], b_ref[...], preferred_element_type=F32).astype(o_ref.dtype)


def _mem_proj(mem_b, w):
    n, width = mem_b.shape[0], w.shape[1]
    return pl.pallas_call(
        _matmul_kernel,
        grid=(width // 512,),
        in_specs=[pl.BlockSpec((n, D_MODEL), lambda j: (0, 0)), pl.BlockSpec((D_MODEL, 512), lambda j: (0, j))],
        out_specs=pl.BlockSpec((n, 512), lambda j: (0, j)),
        out_shape=jax.ShapeDtypeStruct((n, width), BF16),
        compiler_params=_cparams(("parallel",)),
        name="mem_proj",
    )(mem_b, w)


def _xattn_kernel(q_ref, k_ref, v_ref, o_ref):
    for h in range(N_HEADS):
        cs = slice(h * HEAD_DIM, (h + 1) * HEAD_DIM)
        s = lax.dot_general(q_ref[:, cs], k_ref[:, cs], _NT, preferred_element_type=F32)
        p = jnp.exp2(s - jnp.max(s, axis=1, keepdims=True))
        o = jnp.dot(p.astype(BF16), v_ref[:, cs], preferred_element_type=F32)
        o_ref[:, cs] = (o / jnp.sum(p, axis=1, keepdims=True)).astype(BF16)


def _xattn(xq, mem_kv):
    L = xq.shape[0]
    n = mem_kv.shape[0]
    row = pl.BlockSpec((ROW_TILE, GROUP_WIDTH), lambda i: (i, 0))
    return pl.pallas_call(
        _xattn_kernel,
        grid=(L // ROW_TILE,),
        in_specs=[row, pl.BlockSpec((n, GROUP_WIDTH), lambda i: (0, 0)),
                  pl.BlockSpec((n, GROUP_WIDTH), lambda i: (0, 1))],
        out_specs=row,
        out_shape=jax.ShapeDtypeStruct((L, GROUP_WIDTH), BF16),
        compiler_params=_cparams(("parallel",)),
        name="xattn",
    )(xq, mem_kv, mem_kv)


def _out_proj_kernel(y0_ref, y1_ref, y2_ref, y3_ref, w_ref, x_ref, g_ref, b_ref, o_ref, ob_ref):
    acc = ALPHA * x_ref[...]
    for j, y_ref in enumerate((y0_ref, y1_ref, y2_ref, y3_ref)):
        acc = acc + jnp.dot(y_ref[...], w_ref[j * GROUP_WIDTH:(j + 1) * GROUP_WIDTH, :], preferred_element_type=F32)
    y = _layer_norm(acc, g_ref[...], b_ref[...])
    o_ref[...] = y
    ob_ref[...] = y.astype(BF16)


def _out_proj(ys, w_o, x, g, b):
    L = x.shape[0]
    tm = ROW_TILE
    part = pl.BlockSpec((tm, GROUP_WIDTH), lambda i: (i, 0))
    row = pl.BlockSpec((tm, D_MODEL), lambda i: (i, 0))
    vec = pl.BlockSpec((1, D_MODEL), lambda i: (0, 0))
    return pl.pallas_call(
        _out_proj_kernel,
        grid=(L // tm,),
        in_specs=[part, part, part, part, _resident((D_MODEL, D_MODEL), lambda i: (0, 0)), row, vec, vec],
        out_specs=[row, row],
        out_shape=[jax.ShapeDtypeStruct((L, D_MODEL), F32), jax.ShapeDtypeStruct((L, D_MODEL), BF16)],
        compiler_params=_cparams(("parallel",)),
        name="out_proj_ln",
    )(*ys, w_o, x, g.reshape(1, -1), b.reshape(1, -1))


def _swiglu_step(x_ref, wg_ref, wu_ref, wd_ref, acc_ref, chunk):
    @pl.when(pl.program_id(1) == 0)
    def _():
        acc_ref[...] = jnp.zeros(acc_ref.shape, F32)

    wg, wu, wd = wg_ref[0].astype(BF16), wu_ref[0].astype(BF16), wd_ref[0].astype(BF16)
    for r in range(0, x_ref.shape[0], chunk):
        rows = slice(r, r + chunk)
        x = x_ref[rows, :]
        gate = jnp.dot(x, wg, preferred_element_type=F32)
        up = jnp.dot(x, wu, preferred_element_type=F32)
        hid = (gate * _sigmoid(gate) * up).astype(BF16)
        acc_ref[rows, :] += jnp.dot(hid, wd, preferred_element_type=F32)


def _ffn_dense_kernel(x_ref, wg_ref, wu_ref, wd_ref, xr_ref, g_ref, b_ref, o_ref, ob_ref, acc_ref):
    _swiglu_step(x_ref, wg_ref, wu_ref, wd_ref, acc_ref, FF_ROW_CHUNK)

    @pl.when(pl.program_id(1) == pl.num_programs(1) - 1)
    def _():
        y = _layer_norm(ALPHA * xr_ref[...] + acc_ref[...], g_ref[...], b_ref[...])
        o_ref[...] = y
        ob_ref[...] = y.astype(BF16)


def _ffn_dense(xb, x, wg, wu, wd, g, b):
    L = x.shape[0]
    tm, tf = ROW_TILE, FF_TILE
    row = pl.BlockSpec((tm, D_MODEL), lambda i, f: (i, 0))
    vec = pl.BlockSpec((1, D_MODEL), lambda i, f: (0, 0))
    return pl.pallas_call(
        _ffn_dense_kernel,
        grid=(L // tm, D_FF // tf),
        in_specs=[row,
                  pl.BlockSpec((1, D_MODEL, tf), lambda i, f: (0, 0, f)),
                  pl.BlockSpec((1, D_MODEL, tf), lambda i, f: (0, 0, f)),
                  pl.BlockSpec((1, tf, D_MODEL), lambda i, f: (0, f, 0)),
                  row, vec, vec],
        out_specs=[row, row],
        out_shape=[jax.ShapeDtypeStruct((L, D_MODEL), F32), jax.ShapeDtypeStruct((L, D_MODEL), BF16)],
        scratch_shapes=[pltpu.VMEM((tm, D_MODEL), F32)],
        compiler_params=_cparams(("parallel", "arbitrary")),
        name="ffn_dense_ln",
    )(xb, wg[None], wu[None], wd[None], x, g.reshape(1, -1), b.reshape(1, -1))


def _start_row_gather(idx_ref, base, n, src_hbm, dst_ref, sem):
    def body(r, carry):
        pltpu.make_async_copy(src_hbm.at[pl.ds(idx_ref[base + r], 1)], dst_ref.at[pl.ds(r, 1)], sem).start()
        return carry

    lax.fori_loop(0, n, body, 0, unroll=8)


def _wait_row_gather(n, src_hbm, dst_ref, sem):
    pltpu.make_async_copy(src_hbm.at[pl.ds(0, n)], dst_ref.at[pl.ds(0, n)], sem).wait()


def _ffn_moe_kernel(exp_ref, used_ref, tok_ref, x_hbm, wg_ref, wu_ref, wd_ref, o_ref, stage_ref, xb_ref, sem,
                    *, per, chunk):
    b, f = pl.program_id(0), pl.program_id(1)
    n_b, n_f = pl.num_programs(0), pl.num_programs(1)
    tm = stage_ref.shape[0]
    used = used_ref[0]
    live = b < used

    @pl.when(jnp.logical_and(live, f == 0))
    def _():
        @pl.when(b == 0)
        def _():
            _start_row_gather(tok_ref, 0, tm, x_hbm, stage_ref, sem)

        _wait_row_gather(tm, x_hbm, stage_ref, sem)
        xb_ref[...] = stage_ref[...].astype(BF16)

    @pl.when(live)
    def _():
        base = jnp.minimum(b + 1, n_b - 1) * tm + f * per
        for j in range(per):
            pltpu.make_async_copy(x_hbm.at[pl.ds(tok_ref[base + j], 1)],
                                  stage_ref.at[pl.ds(f * per + j, 1)], sem).start()
        _swiglu_step(xb_ref, wg_ref, wu_ref, wd_ref, o_ref, chunk)

    @pl.when(jnp.logical_and(b == used - 1, f == n_f - 1))
    def _():
        _wait_row_gather(tm, x_hbm, stage_ref, sem)

    @pl.when(jnp.logical_and(jnp.logical_not(live), f == 0))
    def _():
        o_ref[...] = jnp.zeros(o_ref.shape, F32)


def _ffn_moe(x, buf_tok, blk_exp, n_used, wg, wu, wd, layer):
    P = buf_tok.shape[0]
    tm, tf = MOE_ROWS, MOE_FF_TILE
    nf = D_FF // tf
    assert tm % nf == 0 and tm % MOE_ROW_CHUNK == 0
    ff = lambda b, f, used: jnp.where(b < used[0], f, nf - 1)
    grid_spec = pltpu.PrefetchScalarGridSpec(
        num_scalar_prefetch=3,
        grid=(P // tm, nf),
        in_specs=[pl.BlockSpec(memory_space=pl.ANY),
                  pl.BlockSpec((None, 1, D_MODEL, tf), lambda b, f, e, u, t: (layer, e[b], 0, ff(b, f, u))),
                  pl.BlockSpec((None, 1, D_MODEL, tf), lambda b, f, e, u, t: (layer, e[b], 0, ff(b, f, u))),
                  pl.BlockSpec((None, 1, tf, D_MODEL), lambda b, f, e, u, t: (layer, e[b], ff(b, f, u), 0))],
        out_specs=pl.BlockSpec((tm, D_MODEL), lambda b, f, e, u, t: (b, 0)),
        scratch_shapes=[pltpu.VMEM((tm, D_MODEL), F32), pltpu.VMEM((tm, D_MODEL), BF16),
                        pltpu.SemaphoreType.DMA(())],
    )
    return pl.pallas_call(
        functools.partial(_ffn_moe_kernel, per=tm // nf, chunk=MOE_ROW_CHUNK),
        grid_spec=grid_spec,
        out_shape=jax.ShapeDtypeStruct((P, D_MODEL), F32),
        compiler_params=_cparams(("arbitrary", "arbitrary"), 56),
        name="ffn_moe",
    )(blk_exp, n_used, buf_tok, x, wg, wu, wd)


def _router_kernel(x_ref, r_ref, o_ref):
    xs = _split3(x_ref[...])
    rs = (r_ref[0], r_ref[1], r_ref[2])
    logits = None
    for i, j in ((0, 0), (0, 1), (1, 0), (1, 1), (0, 2), (2, 0)):
        t = jnp.dot(xs[i], rs[j], preferred_element_type=F32)
        logits = t if logits is None else logits + t
    lane = lax.broadcasted_iota(jnp.int32, logits.shape, 1)
    lane_f = lane.astype(F32)
    logits = jnp.where(lane < N_EXPERTS, logits, NEG_INF)
    m1 = jnp.max(logits, axis=1, keepdims=True)
    i1 = jnp.min(jnp.where(logits == m1, lane_f, 128.0), axis=1, keepdims=True)
    rest = jnp.where(lane_f == i1, NEG_INF, logits)
    m2 = jnp.max(rest, axis=1, keepdims=True)
    i2 = jnp.min(jnp.where(rest == m2, lane_f, 128.0), axis=1, keepdims=True)
    e2 = jnp.exp(m2 - m1)
    g1 = 1.0 / (1.0 + e2)
    g2 = e2 / (1.0 + e2)
    o_ref[...] = jnp.where(lane == 0, i1, jnp.where(lane == 1, i2, jnp.where(lane == 2, g1, jnp.where(lane == 3, g2, 0.0))))


def _router(x, router3):
    L = x.shape[0]
    return pl.pallas_call(
        _router_kernel,
        grid=(L // ROW_TILE,),
        in_specs=[pl.BlockSpec((ROW_TILE, D_MODEL), lambda i: (i, 0)),
                  pl.BlockSpec((3, D_MODEL, 128), lambda i: (0, 0, 0))],
        out_specs=pl.BlockSpec((ROW_TILE, 128), lambda i: (i, 0)),
        out_shape=jax.ShapeDtypeStruct((L, 128), F32),
        compiler_params=_cparams(("parallel",)),
        name="moe_router",
    )(x, router3)


def _combine_kernel(da_ref, db_ref, x_ref, gw_ref, g_ref, b_ref, y_hbm, o_ref, ob_ref, ya_ref, yb_ref, sem):
    i, n = pl.program_id(0), pl.num_programs(0)
    tm = x_ref.shape[0]
    slot = i % 2

    def start(step, slot):
        _start_row_gather(da_ref, step * tm, tm, y_hbm, ya_ref.at[slot], sem.at[slot])
        _start_row_gather(db_ref, step * tm, tm, y_hbm, yb_ref.at[slot], sem.at[slot])

    @pl.when(i == 0)
    def _():
        start(0, 0)

    @pl.when(i + 1 < n)
    def _():
        start(i + 1, 1 - slot)

    _wait_row_gather(tm, y_hbm, ya_ref.at[slot], sem.at[slot])
    _wait_row_gather(tm, y_hbm, yb_ref.at[slot], sem.at[slot])
    gw = gw_ref[...]
    f = gw[:, 2:3] * ya_ref[slot] + gw[:, 3:4] * yb_ref[slot]
    y = _layer_norm(ALPHA * x_ref[...] + f, g_ref[...], b_ref[...])
    o_ref[...] = y
    ob_ref[...] = y.astype(BF16)


def _combine(x, y_rows, dest_a, dest_b, route, g, b):
    L = x.shape[0]
    tm = COMBINE_ROWS
    row = pl.BlockSpec((tm, D_MODEL), lambda i, a, b: (i, 0))
    vec = pl.BlockSpec((1, D_MODEL), lambda i, a, b: (0, 0))
    grid_spec = pltpu.PrefetchScalarGridSpec(
        num_scalar_prefetch=2,
        grid=(L // tm,),
        in_specs=[row, pl.BlockSpec((tm, 128), lambda i, a, b: (i, 0)), vec, vec, pl.BlockSpec(memory_space=pl.ANY)],
        out_specs=[row, row],
        scratch_shapes=[pltpu.VMEM((2, tm, D_MODEL), F32), pltpu.VMEM((2, tm, D_MODEL), F32),
                        pltpu.SemaphoreType.DMA((2,))],
    )
    return pl.pallas_call(
        _combine_kernel,
        grid_spec=grid_spec,
        out_shape=[jax.ShapeDtypeStruct((L, D_MODEL), F32), jax.ShapeDtypeStruct((L, D_MODEL), BF16)],
        compiler_params=_cparams(("arbitrary",)),
        name="moe_combine_ln",
    )(dest_a, dest_b, x, route, g.reshape(1, -1), b.reshape(1, -1), y_rows)


def _moe(x, router, wg, wu, wd, layer, g, b):
    T = x.shape[0]
    r32 = jnp.pad(router, ((0, 0), (0, 128 - N_EXPERTS)))
    r1 = r32.astype(BF16)
    r2 = (r32 - r1.astype(F32)).astype(BF16)
    r3 = (r32 - r1.astype(F32) - r2.astype(F32)).astype(BF16)
    route = _router(x, jnp.stack([r1, r2, r3]))
    e_flat = route[:, :TOP_K].astype(jnp.int32).reshape(-1)
    onehot = (e_flat[:, None] == jnp.arange(N_EXPERTS, dtype=jnp.int32)[None, :]).astype(jnp.int32)
    csum = jnp.cumsum(onehot, axis=0)
    rank = jnp.sum(onehot * csum, axis=1) - 1
    counts = csum[-1]
    padded = (counts + MOE_ROWS - 1) // MOE_ROWS * MOE_ROWS
    pend = jnp.cumsum(padded)
    dest = (pend - padded)[e_flat] + rank
    n_blocks = -(-(T * TOP_K + N_EXPERTS * (MOE_ROWS - 1)) // MOE_ROWS)
    P = n_blocks * MOE_ROWS
    tok = jnp.repeat(jnp.arange(T, dtype=jnp.int32), TOP_K)
    buf_tok = jnp.zeros((P,), jnp.int32).at[dest].set(tok)
    n_used = (pend[-1] // MOE_ROWS).astype(jnp.int32).reshape(1)
    blk_start = jnp.arange(n_blocks, dtype=jnp.int32) * MOE_ROWS
    blk_exp = jnp.minimum(jnp.searchsorted(pend, jnp.minimum(blk_start, pend[-1] - 1), side='right'),
                          N_EXPERTS - 1).astype(jnp.int32)
    y_rows = _ffn_moe(x, buf_tok, blk_exp, n_used, wg, wu, wd, layer)
    d2 = dest.reshape(T, TOP_K)
    return _combine(x, y_rows, d2[:, 0], d2[:, 1], route, g, b)


def _rope_tables(L):
    inv_freq = 1.0 / (ROPE_THETA ** (jnp.arange(0, ROPE_DIM, 2, dtype=F32) / ROPE_DIM))
    ang = jnp.arange(L, dtype=F32)[:, None] * inv_freq[None, :]
    cos, sin = jnp.cos(ang), jnp.sin(ang)
    z = lambda n: jnp.zeros((L, n), F32)
    cosf = jnp.concatenate([cos, cos, jnp.ones((L, HEAD_DIM - ROPE_DIM), F32)], axis=1)
    sa = jnp.concatenate([-sin, z(HEAD_DIM - 16)], axis=1)
    sb = jnp.concatenate([z(16), sin, z(HEAD_DIM - ROPE_DIM)], axis=1)
    return cosf, sa, sb


def _w_in_pack_kernel(a_ref, b_ref, c_ref, o_ref):
    o_ref[:, :3328] = a_ref[...].astype(BF16)
    o_ref[:, 3328:3840] = b_ref[...].astype(BF16)
    o_ref[:, 3840:] = c_ref[...].astype(BF16)


def _reorder_w_in(w_all, layer):
    tail = w_all[layer, :, 3328:]
    gates = jnp.pad(tail[:, :12], ((0, 0), (0, 128 - 12)))
    rows = 256
    blk = lambda n: pl.BlockSpec((rows, n), lambda i: (i, 0))
    return pl.pallas_call(
        _w_in_pack_kernel,
        grid=(D_MODEL // rows,),
        in_specs=[pl.BlockSpec((None, rows, 3328), lambda i: (layer, i, 0)), blk(512), blk(128)],
        out_specs=blk(PROJ_WIDTH),
        out_shape=jax.ShapeDtypeStruct((D_MODEL, PROJ_WIDTH), BF16),
        compiler_params=_cparams(("parallel",)),
        name="w_in_pack",
    )(w_all, tail[:, 12:], gates)


def _trunk(x, mem, p):
    L = x.shape[0]
    ns, nc = L // NSA_SEL_BLOCK, L // NSA_CMP_STRIDE
    cosf, sa, sb = _rope_tables(L)
    lanes = -(-ns // HEAD_DIM) * HEAD_DIM
    cj = jnp.arange(nc)[:, None] - 4 * jnp.arange(lanes)[None, :]
    agg = jnp.logical_and(cj >= -1, cj <= 3).astype(BF16)
    ck = min(SEL_CHUNK, L)
    per = ck // NSA_SEL_BLOCK
    n_tag = min(HEAD_DIM // per, L // ck)
    blk_id = jnp.arange(n_tag)[:, None, None] * per + jnp.arange(ck)[None, :, None] // NSA_SEL_BLOCK
    tags = (blk_id == jnp.arange(HEAD_DIM)[None, None, :]).astype(BF16)
    mem_b = mem.astype(BF16)

    x, xb = _ln_in(x, p['ln_in_g'], p['ln_in_b'])
    for i in range(DEPTH):
        u, mq, mk, mv, nqc, nqr, cmp_kv, nkv, xq, gates = _proj(xb, _reorder_w_in(p['w_in'], i), cosf, sa, sb)
        bmat, cmat, pw = _s5_params(p['ssm_a_re'][i], p['ssm_a_im'][i], p['ssm_log_dt'][i], p['ssm_b_re'][i],
                                    p['ssm_b_im'][i], p['ssm_c_re'][i], p['ssm_c_im'][i])
        y_ssm = _s5(u, bmat, cmat, pw, p['ssm_d'][i], p['ssm_w_glu'][i])
        y_moba = _moba(mq, mk, mv, _kmean(mk))
        pos = jnp.stack([p['nsa_pos_k'][i].reshape(1, -1), p['nsa_pos_v'][i].reshape(1, -1)])
        pos = jnp.broadcast_to(pos, (2, 8, pos.shape[-1])).astype(BF16)
        kvc = _compress(cmp_kv, jnp.stack([p['nsa_ck1'][i], p['nsa_cv1'][i]]).astype(BF16),
                        jnp.stack([p['nsa_ck2'][i], p['nsa_cv2'][i]]).astype(BF16), pos)
        o_cmp, sel_bias = _nsa_cmp(nqc, kvc, agg)
        y_nsa = _nsa_sel(nqr, nkv, sel_bias, tags, o_cmp, gates)
        mem_kv = _mem_proj(mem_b, jnp.concatenate([p['mem_wk'][i], p['mem_wv'][i]], axis=1).astype(BF16))
        y_x = _xattn(xq, mem_kv)
        x, xb = _out_proj((y_ssm, y_moba, y_nsa, y_x), p['w_o'][i].astype(BF16), x, p['ln1_g'][i], p['ln1_b'][i])
        j = i // 2
        if i % 2 == 0:
            x, xb = _ffn_dense(xb, x, p['ffn_w_gate'][j].astype(BF16), p['ffn_w_up'][j].astype(BF16),
                               p['ffn_w_down'][j].astype(BF16), p['ln2_g'][i], p['ln2_b'][i])
        else:
            x, xb = _moe(x, p['moe_router'][j], p['moe_w_gate'], p['moe_w_up'], p['moe_w_down'], j,
                         p['ln2_g'][i], p['ln2_b'][i])
    return x


def kernel(x, mem, ln_in_g, ln_in_b, w_in, ssm_a_re, ssm_a_im, ssm_log_dt, ssm_b_re, ssm_b_im, ssm_c_re, ssm_c_im,
           ssm_d, ssm_w_glu, nsa_pos_k, nsa_pos_v, nsa_ck1, nsa_ck2, nsa_cv1, nsa_cv2, mem_wk, mem_wv, w_o, ln1_g,
           ln1_b, ln2_g, ln2_b, ffn_w_gate, ffn_w_up, ffn_w_down, moe_router, moe_w_gate, moe_w_up, moe_w_down):
    p = dict(ln_in_g=ln_in_g, ln_in_b=ln_in_b, w_in=w_in, ssm_a_re=ssm_a_re, ssm_a_im=ssm_a_im,
             ssm_log_dt=ssm_log_dt, ssm_b_re=ssm_b_re, ssm_b_im=ssm_b_im, ssm_c_re=ssm_c_re, ssm_c_im=ssm_c_im,
             ssm_d=ssm_d, ssm_w_glu=ssm_w_glu, nsa_pos_k=nsa_pos_k, nsa_pos_v=nsa_pos_v, nsa_ck1=nsa_ck1,
             nsa_ck2=nsa_ck2, nsa_cv1=nsa_cv1, nsa_cv2=nsa_cv2, mem_wk=mem_wk, mem_wv=mem_wv, w_o=w_o,
             ln1_g=ln1_g, ln1_b=ln1_b, ln2_g=ln2_g, ln2_b=ln2_b, ffn_w_gate=ffn_w_gate, ffn_w_up=ffn_w_up,
             ffn_w_down=ffn_w_down, moe_router=moe_router, moe_w_gate=moe_w_gate, moe_w_up=moe_w_up,
             moe_w_down=moe_w_down)
    assert x.shape[0] == 1 and mem.shape[0] == 1
    return _trunk(x[0], mem[0], p)[None]
```
